```python
import jax, jax.numpy as jnp
from jax import lax
import numpy as np

D_MODEL = 1024
BATCH = 4
SEQ = 8192
DEPTH = 1

PLE_DIM = 256
ML_HEADS = 4
ML_DQK = 128
ML_DV = 256
ML_CONV = 4
ML_CHUNK = 64
SW_Q_HEADS = 16
SW_KV_HEADS = 4
SW_HEAD_DIM = 64
SW_WINDOW = 128
D_FF = 4 * D_MODEL
EPS = 1e-6

ML_QK_W = ML_HEADS * ML_DQK
ML_V_W = ML_HEADS * ML_DV
SW_Q_W = SW_Q_HEADS * SW_HEAD_DIM
SW_KV_W = SW_KV_HEADS * SW_HEAD_DIM
SPLIT_SIZES = (2 * ML_QK_W, ML_V_W, ML_V_W, 2 * ML_HEADS, SW_Q_W, SW_KV_W, SW_KV_W, D_MODEL, D_MODEL)
N_IN = sum(SPLIT_SIZES)

kernel_name = "hybrid_mlstm_swa_sink_parallel_block"


def rmsnorm(x, g):
    xf = x.astype(jnp.float32)
    xf = xf * lax.rsqrt(jnp.mean(xf * xf, axis=-1, keepdims=True) + EPS)
    return xf.astype(x.dtype) * g.astype(x.dtype)


def causal_depthwise_conv(x, w):
    c = x.shape[-1]
    return lax.conv_general_dilated(
        x, w.astype(x.dtype)[:, None, :], window_strides=(1,), padding=((ML_CONV - 1, 0),),
        dimension_numbers=("NWC", "WIO", "NWC"), feature_group_count=c)


def mlstm_chunkwise(q, k, v, ig, lf):
    B, S = q.shape[:2]
    L = ML_CHUNK
    nc = S // L

    def to_chunks(t):
        t = t.astype(jnp.float32).reshape((B, nc, L) + t.shape[2:])
        return jnp.moveaxis(jnp.moveaxis(t, 1, 0), 3, 2)

    xs = (to_chunks(q), to_chunks(k), to_chunks(v), to_chunks(ig), to_chunks(lf))
    causal = jnp.tril(jnp.ones((L, L), dtype=bool))

    def step(carry, chunk):
        C, n, m = carry
        qc, kc, vc, ic, fc = chunk
        b = jnp.cumsum(fc, axis=-1)
        log_d = jnp.where(causal, b[..., :, None] - b[..., None, :] + ic[..., None, :], -jnp.inf)
        inter = b + m[..., None]
        m_t = jnp.maximum(inter, jnp.max(log_d, axis=-1))
        scores = jnp.einsum("bhtk,bhsk->bhts", qc, kc) * jnp.exp(log_d - m_t[..., None])
        w_inter = jnp.exp(inter - m_t)
        num = (w_inter[..., None] * jnp.einsum("bhvk,bhtk->bhtv", C, qc)
               + jnp.einsum("bhts,bhsv->bhtv", scores, vc))
        den = w_inter * jnp.einsum("bhk,bhtk->bht", n, qc) + jnp.sum(scores, axis=-1)
        h = num / jnp.maximum(jnp.abs(den), jnp.exp(-m_t))[..., None]
        b_last = b[..., -1]
        log_w = b_last[..., None] - b + ic
        m_new = jnp.maximum(b_last + m, jnp.max(log_w, axis=-1))
        w = jnp.exp(log_w - m_new[..., None])
        decay = jnp.exp(b_last + m - m_new)
        C_new = decay[..., None, None] * C + jnp.einsum("bhs,bhsv,bhsk->bhvk", w, vc, kc)
        n_new = decay[..., None] * n + jnp.einsum("bhs,bhsk->bhk", w, kc)
        return (C_new, n_new, m_new), h

    H, dk, dv = q.shape[2], q.shape[3], v.shape[3]
    init = (jnp.zeros((B, H, dv, dk), jnp.float32), jnp.zeros((B, H, dk), jnp.float32),
            jnp.zeros((B, H), jnp.float32))
    _, h = lax.scan(step, init, xs)
    h = jnp.moveaxis(jnp.moveaxis(h, 2, 3), 0, 1)
    return h.reshape(B, S, H, dv)


def swa_with_sinks(q, k, v, sinks):
    B, S = q.shape[:2]
    W = SW_WINDOW
    nb = S // W
    G = SW_Q_HEADS // SW_KV_HEADS
    qb = q.reshape(B, nb, W, SW_KV_HEADS, G, SW_HEAD_DIM)

    def band(t):
        tb = t.reshape(B, nb, W, SW_KV_HEADS, SW_HEAD_DIM)
        prev = jnp.pad(tb, ((0, 0), (1, 0), (0, 0), (0, 0), (0, 0)))[:, :-1]
        return jnp.concatenate([prev, tb], axis=2)

    kb, vb = band(k), band(v)
    logits = jnp.einsum("bnqhgd,bnkhd->bnhgqk", qb, kb).astype(jnp.float32) * (SW_HEAD_DIM ** -0.5)
    qi = jnp.arange(W)[:, None]
    ki = jnp.arange(2 * W)[None, :]
    diff = qi + W - ki
    band_mask = (diff >= 0) & (diff < W)
    valid = band_mask[None] & ((jnp.arange(nb)[:, None, None] > 0) | (ki >= W)[None])
    logits = jnp.where(valid[None, :, None, None], logits, -jnp.inf)
    sink = sinks.astype(jnp.float32).reshape(SW_KV_HEADS, G)[None, None, :, :, None, None]
    m = jnp.maximum(jnp.max(logits, axis=-1, keepdims=True), sink)
    pexp = jnp.exp(logits - m)
    probs = pexp / (jnp.sum(pexp, axis=-1, keepdims=True) + jnp.exp(sink - m))
    out = jnp.einsum("bnhgqk,bnkhd->bnqhgd", probs.astype(v.dtype), vb)
    return out.reshape(B, S, SW_Q_W)


def setup_inputs(seed: int = 0) -> dict:
    key = jax.random.key(seed)
    ks = jax.random.split(key, 20)
    f32 = jnp.float32
    nrm = lambda k, shape, s: jax.random.normal(k, shape, f32) * s
    gain = lambda k, shape: 1.0 + 0.05 * jax.random.normal(k, shape, f32)
    b_if = jnp.concatenate([
        0.1 * jax.random.normal(ks[3], (DEPTH, ML_HEADS), f32),
        3.0 + 0.5 * jax.random.normal(ks[4], (DEPTH, ML_HEADS), f32),
    ], axis=-1)
    return {
        "x": jax.random.normal(ks[0], (BATCH, SEQ, D_MODEL), f32),
        "p": jax.random.normal(ks[1], (DEPTH, BATCH, SEQ, PLE_DIM), f32),
        "norm_mix_g": gain(ks[2], (DEPTH, D_MODEL)),
        "w_in": nrm(ks[5], (DEPTH, D_MODEL, N_IN), D_MODEL ** -0.5),
        "conv_qk": nrm(ks[6], (DEPTH, ML_CONV, 2 * ML_QK_W), ML_CONV ** -0.5),
        "b_if": b_if,
        "mlstm_norm_g": gain(ks[7], (DEPTH, ML_V_W)),
        "sinks": nrm(ks[8], (DEPTH, SW_Q_HEADS), 0.5),
        "w_branch_a": nrm(ks[9], (DEPTH, ML_V_W, D_MODEL), ML_V_W ** -0.5),
        "w_branch_b": nrm(ks[10], (DEPTH, SW_Q_W, D_MODEL), SW_Q_W ** -0.5),
        "w_out": nrm(ks[11], (DEPTH, D_MODEL, D_MODEL), D_MODEL ** -0.5),
        "norm_mlp_g": gain(ks[12], (DEPTH, D_MODEL)),
        "w_up": nrm(ks[13], (DEPTH, D_MODEL, D_FF), D_MODEL ** -0.5),
        "w_down": nrm(ks[14], (DEPTH, D_FF, D_MODEL), D_FF ** -0.5),
        "norm_ple_g": gain(ks[15], (DEPTH, D_MODEL)),
        "w_ple_gate": nrm(ks[16], (DEPTH, D_MODEL, D_MODEL), D_MODEL ** -0.5),
        "w_ple_proj": nrm(ks[17], (DEPTH, PLE_DIM, D_MODEL), PLE_DIM ** -0.5),
        "final_norm_g": gain(ks[18], (D_MODEL,)),
    }


def reference(x, p, norm_mix_g, w_in, conv_qk, b_if, mlstm_norm_g, sinks, w_branch_a, w_branch_b,
              w_out, norm_mlp_g, w_up, w_down, norm_ple_g, w_ple_gate, w_ple_proj, final_norm_g):
    B, S, _ = x.shape
    split_idx = [int(v) for v in np.cumsum(SPLIT_SIZES)[:-1]]
    for i in range(DEPTH):
        h = rmsnorm(x, norm_mix_g[i])
        proj = h @ w_in[i]
        qk_ml, v_ml, o_ml, if_pre, q_sw, k_sw, v_sw, g_a, g_b = jnp.split(proj, split_idx, axis=-1)

        qk_ml = jax.nn.silu(causal_depthwise_conv(qk_ml, conv_qk[i]))
        q_ml, k_ml = jnp.split(qk_ml, 2, axis=-1)
        q_ml = q_ml.reshape(B, S, ML_HEADS, ML_DQK) * (ML_DQK ** -0.5)
        k_ml = k_ml.reshape(B, S, ML_HEADS, ML_DQK)
        gates = (if_pre + b_if[i].astype(if_pre.dtype)).astype(jnp.float32)
        ig = gates[..., :ML_HEADS]
        lf = jax.nn.log_sigmoid(gates[..., ML_HEADS:])
        h_ml = mlstm_chunkwise(q_ml, k_ml, v_ml.reshape(B, S, ML_HEADS, ML_DV), ig, lf)
        h_ml = h_ml * lax.rsqrt(jnp.mean(h_ml * h_ml, axis=-1, keepdims=True) + EPS)
        h_ml = h_ml.reshape(B, S, ML_V_W).astype(x.dtype) * mlstm_norm_g[i].astype(x.dtype)
        y_a = jax.nn.sigmoid(o_ml) * h_ml

        y_b = swa_with_sinks(q_sw.reshape(B, S, SW_Q_HEADS, SW_HEAD_DIM),
                             k_sw.reshape(B, S, SW_KV_HEADS, SW_HEAD_DIM),
                             v_sw.reshape(B, S, SW_KV_HEADS, SW_HEAD_DIM), sinks[i])

        merged = jax.nn.sigmoid(g_a) * (y_a @ w_branch_a[i]) + jax.nn.sigmoid(g_b) * (y_b @ w_branch_b[i])
        x = x + merged @ w_out[i]

        u = rmsnorm(x, norm_mlp_g[i]) @ w_up[i]
        x = x + jnp.square(jax.nn.relu(u)) @ w_down[i]

        gate = jax.nn.sigmoid(rmsnorm(x, norm_ple_g[i]) @ w_ple_gate[i])
        x = x + gate * (p[i].astype(x.dtype) @ w_ple_proj[i])
    return rmsnorm(x, final_norm_g)
```

```python
import functools

import jax
import jax.numpy as jnp
from jax import lax
from jax.experimental import pallas as pl
from jax.experimental.pallas import tpu as pltpu

D_MODEL = 1024
PLE_DIM = 256
ML_HEADS = 4
ML_DQK = 128
ML_DV = 256
ML_CONV = 4
SW_Q_HEADS = 16
SW_KV_HEADS = 4
SW_HEAD_DIM = 64
SW_WINDOW = 128
SW_GROUP = SW_Q_HEADS // SW_KV_HEADS
D_FF = 4 * D_MODEL
EPS = 1e-6

ML_QK_W = ML_HEADS * ML_DQK
ML_V_W = ML_HEADS * ML_DV
SW_Q_W = SW_Q_HEADS * SW_HEAD_DIM
SW_KV_W = SW_KV_HEADS * SW_HEAD_DIM

LANES = 128
BF16_SUBLANES = 16

COL_QK = 0
COL_V = COL_QK + 2 * ML_QK_W
COL_O = COL_V + ML_V_W
COL_QSW = COL_O + ML_V_W
COL_KSW = COL_QSW + SW_Q_W
COL_VSW = COL_KSW + 2 * SW_KV_W
COL_GA = COL_VSW + 2 * SW_KV_W
COL_GB = COL_GA + D_MODEL
N_PROJ = COL_GB + D_MODEL

ML_CHUNK = 128
TM_PROJ = 1024
TN_PROJ = 1024
TM_MERGE = 512
TM_MLP = 512
FF_CHUNK = 1024

F32 = jnp.float32
BF16 = jnp.bfloat16


def _dot(a, b):
    return jnp.dot(a, b, preferred_element_type=F32)


def _dot_nt(a, b):
    return lax.dot_general(a, b, (((1,), (1,)), ((), ())), preferred_element_type=F32)


def _rms(x, g):
    return x * lax.rsqrt(jnp.mean(x * x, axis=-1, keepdims=True) + EPS) * g


def _split_bf16(x, parts):
    out = []
    r = x
    for _ in range(parts - 1):
        t = r.astype(BF16)
        out.append(t)
        r = r - t.astype(F32)
    out.append(r.astype(BF16))
    return out


def _in_proj_kernel(x_ref, g_ref, w_ref, wg_hi_ref, wg_lo_ref, proj_ref, gates_ref, hn_ref):
    @pl.when(pl.program_id(1) == 0)
    def _():
        h = _rms(x_ref[...], g_ref[...])
        h_hi, h_lo = _split_bf16(h, 2)
        hn_ref[...] = h_hi
        gates_ref[...] = (_dot(h_hi, wg_hi_ref[...]) + _dot(h_lo, wg_hi_ref[...])
                          + _dot(h_hi, wg_lo_ref[...]))

    proj_ref[...] = _dot(hn_ref[...], w_ref[...]).astype(BF16)


def _in_proj(x2d, g, w_main, wg_hi, wg_lo):
    t = x2d.shape[0]
    return pl.pallas_call(
        _in_proj_kernel,
        out_shape=(jax.ShapeDtypeStruct((t, N_PROJ), BF16),
                   jax.ShapeDtypeStruct((t, LANES), F32)),
        grid=(t // TM_PROJ, N_PROJ // TN_PROJ),
        in_specs=[
            pl.BlockSpec((TM_PROJ, D_MODEL), lambda i, j: (i, 0)),
            pl.BlockSpec((1, D_MODEL), lambda i, j: (0, 0)),
            pl.BlockSpec((D_MODEL, TN_PROJ), lambda i, j: (0, j)),
            pl.BlockSpec((D_MODEL, LANES), lambda i, j: (0, 0)),
            pl.BlockSpec((D_MODEL, LANES), lambda i, j: (0, 0)),
        ],
        out_specs=(pl.BlockSpec((TM_PROJ, TN_PROJ), lambda i, j: (i, j)),
                   pl.BlockSpec((TM_PROJ, LANES), lambda i, j: (i, 0))),
        scratch_shapes=[pltpu.VMEM((TM_PROJ, D_MODEL), BF16)],
        compiler_params=pltpu.CompilerParams(
            dimension_semantics=("parallel", "arbitrary"),
            vmem_limit_bytes=40 * 1024 * 1024),
        name="in_proj",
    )(x2d, g, w_main, wg_hi, wg_lo)


def _mlstm_kernel(qk_ref, halo_ref, v_ref, o_ref, gates_ref, convw_ref, bif_ref, gn_ref, y_ref,
                  xbuf_ref, c_ref, n_ref, m_ref):
    L = ML_CHUNK
    H = BF16_SUBLANES
    chunk = pl.program_id(1)

    @pl.when(chunk == 0)
    def _():
        c_ref[...] = jnp.zeros_like(c_ref)
        n_ref[...] = jnp.zeros_like(n_ref)
        m_ref[...] = jnp.zeros_like(m_ref)

    xbuf_ref[0:H, :] = jnp.where(chunk > 0, halo_ref[...].astype(F32), 0.0)
    xbuf_ref[H:H + L, :] = qk_ref[...].astype(F32)
    cw = convw_ref[...]
    base = H - (ML_CONV - 1)
    acc = cw[0:1, :] * xbuf_ref[base:base + L, :]
    for j in range(1, ML_CONV):
        acc = acc + cw[j:j + 1, :] * xbuf_ref[base + j:base + j + L, :]
    act = acc * jax.nn.sigmoid(acc)
    q_all = act[:, :ML_QK_W] * (ML_DQK ** -0.5)
    k_all = act[:, ML_QK_W:]

    gates = gates_ref[...] + bif_ref[...]
    logf = jnp.minimum(gates, 0.0) - jnp.log1p(jnp.exp(-jnp.abs(gates)))
    row = lax.broadcasted_iota(jnp.int32, (L, L), 0)
    col = lax.broadcasted_iota(jnp.int32, (L, L), 1)
    causal = row >= col
    tril = jnp.where(causal, 1.0, 0.0).astype(BF16)
    bcum = sum(_dot(tril, part) for part in _split_bf16(logf, 3))
    gates_t = gates.T
    bcum_t = bcum.T

    for h in range(ML_HEADS):
        fl = ML_HEADS + h
        b_c = bcum[:, fl:fl + 1]
        i_c = gates[:, h:h + 1]
        b_r = bcum_t[fl:fl + 1, :]
        i_r = gates_t[h:h + 1, :]
        m_prev = m_ref[h]
        ct = c_ref[h]
        n_prev = n_ref[h]

        qf = q_all[:, h * ML_DQK:(h + 1) * ML_DQK]
        kf = k_all[:, h * ML_DQK:(h + 1) * ML_DQK]
        q = qf.astype(BF16)
        kt = kf.T.astype(BF16)
        v = v_ref[:, h * ML_DV:(h + 1) * ML_DV]

        log_d = jnp.where(causal, b_c - b_r + i_r, -jnp.inf)
        inter = b_c + m_prev
        m_t = jnp.maximum(inter, jnp.max(log_d, axis=1, keepdims=True))
        scores = _dot(q, kt) * jnp.exp(log_d - m_t)
        w_inter = jnp.exp(inter - m_t)
        num = w_inter * _dot(q, ct.astype(BF16)) + _dot(scores.astype(BF16), v)
        den = (w_inter * jnp.sum(qf * n_prev, axis=1, keepdims=True)
               + jnp.sum(scores, axis=1, keepdims=True))
        hraw = num * (1.0 / jnp.maximum(jnp.abs(den), jnp.exp(-m_t)))
        hn = hraw * lax.rsqrt(jnp.mean(hraw * hraw, axis=1, keepdims=True) + EPS)
        hn = hn * gn_ref[:, h * ML_DV:(h + 1) * ML_DV]
        og = jax.nn.sigmoid(o_ref[:, h * ML_DV:(h + 1) * ML_DV].astype(F32))
        y_ref[:, h * ML_DV:(h + 1) * ML_DV] = (og * hn).astype(BF16)

        b_last = bcum[L - 1:L, fl:fl + 1]
        log_w = b_last - b_c + i_c
        m_new = jnp.maximum(b_last + m_prev, jnp.max(log_w, axis=0, keepdims=True))
        w = jnp.exp(log_w - m_new)
        decay = jnp.exp(b_last + m_prev - m_new)
        wv = (w * v.astype(F32)).astype(BF16)
        c_ref[h] = decay * ct + _dot(kt, wv)
        n_ref[h] = decay * n_prev + jnp.sum(w * kf, axis=0, keepdims=True)
        m_ref[h] = m_new


def _mlstm(proj, gates, conv_w, b_if, gn, batch, seq):
    t = proj.shape[0]
    L = ML_CHUNK
    nc = seq // L
    hpb = L // BF16_SUBLANES
    wblk = 2 * ML_QK_W

    def rows(b, c):
        return b * nc + c

    return pl.pallas_call(
        _mlstm_kernel,
        out_shape=jax.ShapeDtypeStruct((t, ML_V_W), BF16),
        grid=(batch, nc),
        in_specs=[
            pl.BlockSpec((L, wblk), lambda b, c: (rows(b, c), COL_QK // wblk)),
            pl.BlockSpec((BF16_SUBLANES, wblk),
                         lambda b, c: (jnp.maximum(rows(b, c) * hpb - 1, 0), COL_QK // wblk)),
            pl.BlockSpec((L, ML_V_W), lambda b, c: (rows(b, c), COL_V // ML_V_W)),
            pl.BlockSpec((L, ML_V_W), lambda b, c: (rows(b, c), COL_O // ML_V_W)),
            pl.BlockSpec((L, LANES), lambda b, c: (rows(b, c), 0)),
            pl.BlockSpec((ML_CONV, wblk), lambda b, c: (0, 0)),
            pl.BlockSpec((1, LANES), lambda b, c: (0, 0)),
            pl.BlockSpec((1, ML_V_W), lambda b, c: (0, 0)),
        ],
        out_specs=pl.BlockSpec((L, ML_V_W), lambda b, c: (rows(b, c), 0)),
        scratch_shapes=[
            pltpu.VMEM((L + BF16_SUBLANES, wblk), F32),
            pltpu.VMEM((ML_HEADS, ML_DQK, ML_DV), F32),
            pltpu.VMEM((ML_HEADS, 1, ML_DQK), F32),
            pltpu.VMEM((ML_HEADS, 1, 1), F32),
        ],
        compiler_params=pltpu.CompilerParams(
            dimension_semantics=("parallel", "arbitrary"),
            vmem_limit_bytes=32 * 1024 * 1024),
        name="mlstm",
    )(proj, proj, proj, proj, gates, conv_w, b_if, gn)


def _swa_kernel(sinks_ref, q_ref, kvp_ref, kvc_ref, y_ref):
    W = SW_WINDOW
    hd = SW_HEAD_DIM
    blk = pl.program_id(1)
    qi = lax.broadcasted_iota(jnp.int32, (W, 2 * W), 0)
    ki = lax.broadcasted_iota(jnp.int32, (W, 2 * W), 1)
    diff = qi + W - ki
    valid = (diff >= 0) & (diff < W) & ((ki >= W) | (blk > 0))
    lane = lax.broadcasted_iota(jnp.int32, (2 * W, LANES), 1)
    lo_half = lane < hd
    out_lane = lax.broadcasted_iota(jnp.int32, (W, LANES), 1)

    kv = jnp.concatenate([kvp_ref[...], kvc_ref[...]], axis=0)
    zero = jnp.zeros((), BF16)
    for h in range(SW_KV_HEADS):
        kd = kv[:, h * LANES:(h + 1) * LANES]
        vd = kv[:, 2 * SW_KV_W + h * LANES:2 * SW_KV_W + (h + 1) * LANES]
        k_half = (jnp.where(lo_half, kd, zero), jnp.where(lo_half, zero, kd))
        v_half = (jnp.where(lo_half, vd, zero), jnp.where(lo_half, zero, vd))
        for pair in range(SW_GROUP // 2):
            c0 = (h * SW_GROUP + 2 * pair) * hd
            qp = q_ref[:, c0:c0 + LANES] * jnp.asarray(hd ** -0.5, BF16)
            acc = jnp.zeros((W, LANES), F32)
            for e in range(2):
                sink = sinks_ref[h * SW_GROUP + 2 * pair + e]
                logits = jnp.where(valid, _dot_nt(qp, k_half[e]), -jnp.inf)
                m = jnp.maximum(jnp.max(logits, axis=1, keepdims=True), sink)
                p = jnp.exp(logits - m)
                denom = jnp.sum(p, axis=1, keepdims=True) + jnp.exp(sink - m)
                acc = acc + _dot(p.astype(BF16), v_half[e]) * (1.0 / denom)
            y_ref[:, c0:c0 + LANES] = acc.astype(BF16)


def _swa(proj, sinks, batch, seq):
    t = proj.shape[0]
    W = SW_WINDOW
    nb = seq // W
    kvw = 4 * SW_KV_W

    def rows(b, n):
        return b * nb + n

    return pl.pallas_call(
        _swa_kernel,
        out_shape=jax.ShapeDtypeStruct((t, SW_Q_W), BF16),
        grid=(batch, nb),
        in_specs=[
            pl.BlockSpec(memory_space=pltpu.SMEM),
            pl.BlockSpec((W, SW_Q_W), lambda b, n: (rows(b, n), COL_QSW // SW_Q_W)),
            pl.BlockSpec((W, kvw), lambda b, n: (jnp.maximum(rows(b, n) - 1, 0), COL_KSW // kvw)),
            pl.BlockSpec((W, kvw), lambda b, n: (rows(b, n), COL_KSW // kvw)),
        ],
        out_specs=pl.BlockSpec((W, SW_Q_W), lambda b, n: (rows(b, n), 0)),
        compiler_params=pltpu.CompilerParams(
            dimension_semantics=("parallel", "arbitrary"),
            vmem_limit_bytes=32 * 1024 * 1024),
        name="swa",
    )(sinks, proj, proj, proj)


def _merge_kernel(x_ref, ya_ref, yb_ref, ga_ref, gb_ref, wa_ref, wb_ref, wo_ref, out_ref):
    a = jax.nn.sigmoid(ga_ref[...].astype(F32)) * _dot(ya_ref[...], wa_ref[...])
    b = jax.nn.sigmoid(gb_ref[...].astype(F32)) * _dot(yb_ref[...], wb_ref[...])
    out_ref[...] = x_ref[...] + _dot((a + b).astype(BF16), wo_ref[...])


def _merge(x2d, ya, yb, proj, wa, wb, wo):
    t = x2d.shape[0]
    tm = TM_MERGE
    wspec = pl.BlockSpec((D_MODEL, D_MODEL), lambda i: (0, 0))
    return pl.pallas_call(
        _merge_kernel,
        out_shape=jax.ShapeDtypeStruct((t, D_MODEL), F32),
        grid=(t // tm,),
        in_specs=[
            pl.BlockSpec((tm, D_MODEL), lambda i: (i, 0)),
            pl.BlockSpec((tm, D_MODEL), lambda i: (i, 0)),
            pl.BlockSpec((tm, D_MODEL), lambda i: (i, 0)),
            pl.BlockSpec((tm, D_MODEL), lambda i: (i, COL_GA // D_MODEL)),
            pl.BlockSpec((tm, D_MODEL), lambda i: (i, COL_GB // D_MODEL)),
            wspec, wspec, wspec,
        ],
        out_specs=pl.BlockSpec((tm, D_MODEL), lambda i: (i, 0)),
        compiler_params=pltpu.CompilerParams(
            dimension_semantics=("parallel",),
            vmem_limit_bytes=48 * 1024 * 1024),
        name="merge_out",
    )(x2d, ya, yb, proj, proj, wa, wb, wo)


def _mlp_ple_kernel(x_ref, p_ref, gm_ref, wup_ref, wdn_ref, gp_ref, wg_ref, wp_ref, gf_ref, out_ref,
                    *, final_norm):
    x = x_ref[...]
    hn = _rms(x, gm_ref[...]).astype(BF16)
    acc = x
    for c in range(D_FF // FF_CHUNK):
        u = _dot(hn, wup_ref[:, c * FF_CHUNK:(c + 1) * FF_CHUNK])
        r = jnp.maximum(u, 0.0)
        acc = acc + _dot((r * r).astype(BF16), wdn_ref[c * FF_CHUNK:(c + 1) * FF_CHUNK, :])
    x = acc
    gate = jax.nn.sigmoid(_dot(_rms(x, gp_ref[...]).astype(BF16), wg_ref[...]))
    x = x + gate * _dot(p_ref[...].astype(BF16), wp_ref[...])
    if final_norm:
        x = _rms(x, gf_ref[...])
    out_ref[...] = x


def _mlp_ple(x2d, p2d, gm, wup, wdn, gp, wg, wp, gf, final_norm):
    t = x2d.shape[0]
    tm = TM_MLP

    def const(shape):
        return pl.BlockSpec(shape, lambda i: (0, 0))

    return pl.pallas_call(
        functools.partial(_mlp_ple_kernel, final_norm=final_norm),
        out_shape=jax.ShapeDtypeStruct((t, D_MODEL), F32),
        grid=(t // tm,),
        in_specs=[
            pl.BlockSpec((tm, D_MODEL), lambda i: (i, 0)),
            pl.BlockSpec((tm, PLE_DIM), lambda i: (i, 0)),
            const((1, D_MODEL)),
            const((D_MODEL, D_FF)),
            const((D_FF, D_MODEL)),
            const((1, D_MODEL)),
            const((D_MODEL, D_MODEL)),
            const((PLE_DIM, D_MODEL)),
            const((1, D_MODEL)),
        ],
        out_specs=pl.BlockSpec((tm, D_MODEL), lambda i: (i, 0)),
        compiler_params=pltpu.CompilerParams(
            dimension_semantics=("parallel",),
            vmem_limit_bytes=56 * 1024 * 1024),
        name="mlp_ple",
    )(x2d, p2d, gm, wup, wdn, gp, wg, wp, gf)


def _prep_w_in(w_in):
    o = 0
    parts = {}
    for name, size in (("qk", 2 * ML_QK_W), ("v", ML_V_W), ("o", ML_V_W), ("if", 2 * ML_HEADS),
                       ("qsw", SW_Q_W), ("ksw", SW_KV_W), ("vsw", SW_KV_W), ("ga", D_MODEL), ("gb", D_MODEL)):
        parts[name] = w_in[:, o:o + size]
        o += size

    def dup_heads(w):
        w = w.reshape(D_MODEL, SW_KV_HEADS, 1, SW_HEAD_DIM)
        return jnp.broadcast_to(w, (D_MODEL, SW_KV_HEADS, 2, SW_HEAD_DIM)).reshape(D_MODEL, 2 * SW_KV_W)

    w_main = jnp.concatenate([parts["qk"], parts["v"], parts["o"], parts["qsw"], dup_heads(parts["ksw"]),
                              dup_heads(parts["vsw"]), parts["ga"], parts["gb"]], axis=1).astype(BF16)
    wg = jnp.pad(parts["if"], ((0, 0), (0, LANES - 2 * ML_HEADS)))
    wg_hi = wg.astype(BF16)
    wg_lo = (wg - wg_hi.astype(F32)).astype(BF16)
    return w_main, wg_hi, wg_lo


def kernel(x, p, norm_mix_g, w_in, conv_qk, b_if, mlstm_norm_g, sinks, w_branch_a, w_branch_b, w_out,
           norm_mlp_g, w_up, w_down, norm_ple_g, w_ple_gate, w_ple_proj, final_norm_g):
    batch, seq, _ = x.shape
    depth = w_in.shape[0]
    t = batch * seq
    x2d = x.reshape(t, D_MODEL)
    row = lambda v: v.reshape(1, -1)
    for i in range(depth):
        w_main, wg_hi, wg_lo = _prep_w_in(w_in[i])
        proj, gates = _in_proj(x2d, row(norm_mix_g[i]), w_main, wg_hi, wg_lo)
        bif = jnp.pad(b_if[i], (0, LANES - 2 * ML_HEADS)).reshape(1, LANES)
        ya = _mlstm(proj, gates, conv_qk[i], bif, row(mlstm_norm_g[i]), batch, seq)
        yb = _swa(proj, sinks[i], batch, seq)
        x2d = _merge(x2d, ya, yb, proj, w_branch_a[i].astype(BF16), w_branch_b[i].astype(BF16),
                     w_out[i].astype(BF16))
        x2d = _mlp_ple(x2d, p[i].reshape(t, PLE_DIM), row(norm_mlp_g[i]), w_up[i].astype(BF16),
                       w_down[i].astype(BF16), row(norm_ple_g[i]), w_ple_gate[i].astype(BF16),
                       w_ple_proj[i].astype(BF16), row(final_norm_g), final_norm=(i == depth - 1))
    return x2d.reshape(batch, seq, D_MODEL)
```

```python
import functools

import jax
import jax.numpy as jnp
from jax import lax
from jax.experimental import pallas as pl
from jax.experimental.pallas import tpu as pltpu

D_MODEL = 1024
PLE_DIM = 256
ML_HEADS = 4
ML_DQK = 128
ML_DV = 256
ML_CONV = 4
SW_Q_HEADS = 16
SW_KV_HEADS = 4
SW_HEAD_DIM = 64
SW_WINDOW = 128
SW_GROUP = SW_Q_HEADS // SW_KV_HEADS
D_FF = 4 * D_MODEL
EPS = 1e-6

ML_QK_W = ML_HEADS * ML_DQK
ML_V_W = ML_HEADS * ML_DV
SW_Q_W = SW_Q_HEADS * SW_HEAD_DIM
SW_KV_W = SW_KV_HEADS * SW_HEAD_DIM

LANES = 128
BF16_SUBLANES = 16

COL_QK = 0
COL_V = COL_QK + 2 * ML_QK_W
COL_O = COL_V + ML_V_W
COL_QSW = COL_O + ML_V_W
COL_KSW = COL_QSW + SW_Q_W
COL_VSW = COL_KSW + 2 * SW_KV_W
COL_GA = COL_VSW + 2 * SW_KV_W
COL_GB = COL_GA + D_MODEL
N_PROJ = COL_GB + D_MODEL

ML_CHUNK = 128
SW_BLOCKS = 4
SW_SKEW = 2
TM_PROJ = 1024
TN_PROJ = 1024
TM_MERGE = 512
TM_MLP = 512
FF_CHUNK = 1024

F32 = jnp.float32
BF16 = jnp.bfloat16


def _dot(a, b):
    return jnp.dot(a, b, preferred_element_type=F32)


def _dot_nt(a, b):
    return lax.dot_general(a, b, (((1,), (1,)), ((), ())), preferred_element_type=F32)


def _rms(x, g):
    return x * lax.rsqrt(jnp.mean(x * x, axis=-1, keepdims=True) + EPS) * g


def _split_bf16(x, parts):
    out = []
    r = x
    for _ in range(parts - 1):
        t = r.astype(BF16)
        out.append(t)
        r = r - t.astype(F32)
    out.append(r.astype(BF16))
    return out


def _in_proj_kernel(x_ref, g_ref, w_ref, wg_hi_ref, wg_lo_ref, proj_ref, gates_ref, hn_ref):
    @pl.when(pl.program_id(1) == 0)
    def _():
        h = _rms(x_ref[...], g_ref[...])
        h_hi, h_lo = _split_bf16(h, 2)
        hn_ref[...] = h_hi
        gates_ref[...] = (_dot(h_hi, wg_hi_ref[...]) + _dot(h_lo, wg_hi_ref[...])
                          + _dot(h_hi, wg_lo_ref[...]))

    proj_ref[...] = _dot(hn_ref[...], w_ref[...]).astype(BF16)


def _in_proj(x2d, g, w_main, wg_hi, wg_lo):
    t = x2d.shape[0]
    return pl.pallas_call(
        _in_proj_kernel,
        out_shape=(jax.ShapeDtypeStruct((t, N_PROJ), BF16),
                   jax.ShapeDtypeStruct((t, LANES), F32)),
        grid=(t // TM_PROJ, N_PROJ // TN_PROJ),
        in_specs=[
            pl.BlockSpec((TM_PROJ, D_MODEL), lambda i, j: (i, 0)),
            pl.BlockSpec((1, D_MODEL), lambda i, j: (0, 0)),
            pl.BlockSpec((D_MODEL, TN_PROJ), lambda i, j: (0, j)),
            pl.BlockSpec((D_MODEL, LANES), lambda i, j: (0, 0)),
            pl.BlockSpec((D_MODEL, LANES), lambda i, j: (0, 0)),
        ],
        out_specs=(pl.BlockSpec((TM_PROJ, TN_PROJ), lambda i, j: (i, j)),
                   pl.BlockSpec((TM_PROJ, LANES), lambda i, j: (i, 0))),
        scratch_shapes=[pltpu.VMEM((TM_PROJ, D_MODEL), BF16)],
        compiler_params=pltpu.CompilerParams(
            dimension_semantics=("parallel", "arbitrary"),
            vmem_limit_bytes=40 * 1024 * 1024),
        name="in_proj",
    )(x2d, g, w_main, wg_hi, wg_lo)


def _mlstm_kernel(qk_ref, halo_ref, v_ref, o_ref, gates_ref, convw_ref, bif_ref, gn_ref, y_ref,
                  shift_ref, c_ref, n_ref, m_ref):
    L = ML_CHUNK
    H = BF16_SUBLANES
    taps = ML_CONV - 1
    chunk = pl.program_id(1)

    @pl.when(chunk == 0)
    def _():
        c_ref[...] = jnp.zeros_like(c_ref)
        n_ref[...] = jnp.zeros_like(n_ref)
        m_ref[...] = jnp.zeros_like(m_ref)
        r = lax.broadcasted_iota(jnp.int32, (taps * L, H + L), 0)
        u = lax.broadcasted_iota(jnp.int32, (taps * L, H + L), 1)
        src = (r % L) + (r // L) + (H - taps)
        shift_ref[...] = jnp.where(u == src, 1.0, 0.0).astype(BF16)

    x_cur = qk_ref[...]
    halo = jnp.where(chunk > 0, halo_ref[...], jnp.zeros((), BF16))
    shifted = _dot(shift_ref[...], jnp.concatenate([halo, x_cur], axis=0))
    cw = convw_ref[...]
    acc = cw[taps:taps + 1, :] * x_cur.astype(F32)
    for j in range(taps):
        acc = acc + cw[j:j + 1, :] * shifted[j * L:(j + 1) * L, :]
    act = acc * jax.nn.sigmoid(acc)
    q_all = act[:, :ML_QK_W] * (ML_DQK ** -0.5)
    k_all = act[:, ML_QK_W:]

    gates = gates_ref[...] + bif_ref[...]
    logf = jnp.minimum(gates, 0.0) - jnp.log1p(jnp.exp(-jnp.abs(gates)))
    row = lax.broadcasted_iota(jnp.int32, (L, L), 0)
    col = lax.broadcasted_iota(jnp.int32, (L, L), 1)
    causal = row >= col
    tril = jnp.where(causal, 1.0, 0.0).astype(BF16)
    bsplit = _dot(tril, jnp.concatenate(_split_bf16(logf, 3), axis=1))
    bcum = bsplit[:, :LANES] + bsplit[:, LANES:2 * LANES] + bsplit[:, 2 * LANES:]
    gates_t = gates.T
    bcum_t = bcum.T

    for h in range(ML_HEADS):
        fl = ML_HEADS + h
        b_c = bcum[:, fl:fl + 1]
        i_c = gates[:, h:h + 1]
        b_r = bcum_t[fl:fl + 1, :]
        i_r = gates_t[h:h + 1, :]
        m_prev = m_ref[h]
        ct = c_ref[h]
        n_prev = n_ref[h]

        qf = q_all[:, h * ML_DQK:(h + 1) * ML_DQK]
        kf = k_all[:, h * ML_DQK:(h + 1) * ML_DQK]
        q = qf.astype(BF16)
        kt = kf.T.astype(BF16)
        v = v_ref[:, h * ML_DV:(h + 1) * ML_DV]

        log_d = jnp.where(causal, b_c - b_r + i_r, -jnp.inf)
        inter = b_c + m_prev
        m_t = jnp.maximum(inter, jnp.max(log_d, axis=1, keepdims=True))
        scores = _dot(q, kt) * jnp.exp(log_d - m_t)
        w_inter = jnp.exp(inter - m_t)
        num = w_inter * _dot(q, ct.astype(BF16)) + _dot(scores.astype(BF16), v)
        den = (w_inter * jnp.sum(qf * n_prev, axis=1, keepdims=True)
               + jnp.sum(scores, axis=1, keepdims=True))
        hraw = num * (1.0 / jnp.maximum(jnp.abs(den), jnp.exp(-m_t)))
        hn = hraw * lax.rsqrt(jnp.mean(hraw * hraw, axis=1, keepdims=True) + EPS)
        hn = hn * gn_ref[:, h * ML_DV:(h + 1) * ML_DV]
        og = jax.nn.sigmoid(o_ref[:, h * ML_DV:(h + 1) * ML_DV].astype(F32))
        y_ref[:, h * ML_DV:(h + 1) * ML_DV] = (og * hn).astype(BF16)

        b_last = bcum[L - 1:L, fl:fl + 1]
        log_w = b_last - b_c + i_c
        m_new = jnp.maximum(b_last + m_prev, jnp.max(log_w, axis=0, keepdims=True))
        w = jnp.exp(log_w - m_new)
        decay = jnp.exp(b_last + m_prev - m_new)
        wv = (w * v.astype(F32)).astype(BF16)
        c_ref[h] = decay * ct + _dot(kt, wv)
        n_ref[h] = decay * n_prev + jnp.sum(w * kf, axis=0, keepdims=True)
        m_ref[h] = m_new


def _mlstm(proj, gates, conv_w, b_if, gn, batch, seq):
    t = proj.shape[0]
    L = ML_CHUNK
    nc = seq // L
    hpb = L // BF16_SUBLANES
    wblk = 2 * ML_QK_W

    def rows(b, c):
        return b * nc + c

    return pl.pallas_call(
        _mlstm_kernel,
        out_shape=jax.ShapeDtypeStruct((t, ML_V_W), BF16),
        grid=(batch, nc),
        in_specs=[
            pl.BlockSpec((L, wblk), lambda b, c: (rows(b, c), COL_QK // wblk)),
            pl.BlockSpec((BF16_SUBLANES, wblk),
                         lambda b, c: (jnp.maximum(rows(b, c) * hpb - 1, 0), COL_QK // wblk)),
            pl.BlockSpec((L, ML_V_W), lambda b, c: (rows(b, c), COL_V // ML_V_W)),
            pl.BlockSpec((L, ML_V_W), lambda b, c: (rows(b, c), COL_O // ML_V_W)),
            pl.BlockSpec((L, LANES), lambda b, c: (rows(b, c), 0)),
            pl.BlockSpec((ML_CONV, wblk), lambda b, c: (0, 0)),
            pl.BlockSpec((1, LANES), lambda b, c: (0, 0)),
            pl.BlockSpec((1, ML_V_W), lambda b, c: (0, 0)),
        ],
        out_specs=pl.BlockSpec((L, ML_V_W), lambda b, c: (rows(b, c), 0)),
        scratch_shapes=[
            pltpu.VMEM(((ML_CONV - 1) * L, BF16_SUBLANES + L), BF16),
            pltpu.VMEM((ML_HEADS, ML_DQK, ML_DV), F32),
            pltpu.VMEM((ML_HEADS, 1, ML_DQK), F32),
            pltpu.VMEM((ML_HEADS, 1, 1), F32),
        ],
        compiler_params=pltpu.CompilerParams(
            dimension_semantics=("parallel", "arbitrary"),
            vmem_limit_bytes=32 * 1024 * 1024),
        name="mlstm",
    )(proj, proj, proj, proj, gates, conv_w, b_if, gn)


def _swa_kernel(sinks_ref, q_ref, kvp_ref, kvc_ref, y_ref):
    W = SW_WINDOW
    hd = SW_HEAD_DIM
    qi = lax.broadcasted_iota(jnp.int32, (W, 2 * W), 0)
    key = lax.broadcasted_iota(jnp.int32, (W, 2 * W), 1) % W
    from_prev = key > qi
    lo_half = lax.broadcasted_iota(jnp.int32, (W, LANES), 1) < hd
    zero = jnp.zeros((), BF16)
    prev_bias = jnp.where(pl.program_id(1) == 0, -jnp.inf, 0.0).astype(F32)

    def block_diag(d):
        return jnp.concatenate([jnp.where(lo_half, d, zero), jnp.where(lo_half, zero, d)], axis=0)

    def kv_block(j, cols):
        if j < 0:
            return block_diag(kvp_ref[:, cols])
        return block_diag(kvc_ref[j * W:(j + 1) * W, cols])

    kbd, vbd = {}, {}
    for h in range(SW_KV_HEADS):
        for j in range(-1, SW_BLOCKS):
            kbd[j, h] = kv_block(j, slice(h * LANES, (h + 1) * LANES))
            vbd[j, h] = kv_block(j, slice(2 * SW_KV_W + h * LANES, 2 * SW_KV_W + (h + 1) * LANES))

    units = [(j, h, pair) for j in range(SW_BLOCKS) for h in range(SW_KV_HEADS)
             for pair in range(SW_GROUP // 2)]

    def logits_of(j, h, pair):
        c0 = (h * SW_GROUP + 2 * pair) * hd
        qp = q_ref[j * W:(j + 1) * W, c0:c0 + LANES] * jnp.asarray(hd ** -0.5, BF16)
        s_prev = _dot_nt(qp, kbd[j - 1, h])
        if j == 0:
            s_prev = s_prev + prev_bias
        return jnp.where(from_prev, s_prev, _dot_nt(qp, kbd[j, h]))

    def probs_of(lg, h, pair):
        halves = []
        for e in range(2):
            sink = sinks_ref[h * SW_GROUP + 2 * pair + e]
            le = lg[:, e * W:(e + 1) * W]
            m = jnp.maximum(jnp.max(le, axis=1, keepdims=True), sink)
            p = jnp.exp(le - m)
            denom = jnp.sum(p, axis=1, keepdims=True) + jnp.exp(sink - m)
            halves.append((p * (1.0 / denom)).astype(BF16))
        return jnp.concatenate(halves, axis=1)

    def write_out(pb, j, h, pair):
        c0 = (h * SW_GROUP + 2 * pair) * hd
        out = (_dot(jnp.where(from_prev, pb, zero), vbd[j - 1, h])
               + _dot(jnp.where(from_prev, zero, pb), vbd[j, h]))
        y_ref[j * W:(j + 1) * W, c0:c0 + LANES] = out.astype(BF16)

    logits, probs = {}, {}
    n = len(units)
    for i in range(n + 2 * SW_SKEW):
        if i < n:
            logits[i] = logits_of(*units[i])
        if 0 <= i - SW_SKEW < n:
            _, h, pair = units[i - SW_SKEW]
            probs[i - SW_SKEW] = probs_of(logits.pop(i - SW_SKEW), h, pair)
        if 0 <= i - 2 * SW_SKEW < n:
            write_out(probs.pop(i - 2 * SW_SKEW), *units[i - 2 * SW_SKEW])


def _swa(proj, sinks, batch, seq):
    t = proj.shape[0]
    W = SW_WINDOW
    tq = SW_BLOCKS * W
    ns = seq // tq
    kvw = 4 * SW_KV_W

    def rows(b, n):
        return b * ns + n

    return pl.pallas_call(
        _swa_kernel,
        out_shape=jax.ShapeDtypeStruct((t, SW_Q_W), BF16),
        grid=(batch, ns),
        in_specs=[
            pl.BlockSpec(memory_space=pltpu.SMEM),
            pl.BlockSpec((tq, SW_Q_W), lambda b, n: (rows(b, n), COL_QSW // SW_Q_W)),
            pl.BlockSpec((W, kvw), lambda b, n: (jnp.maximum(rows(b, n) * SW_BLOCKS - 1, 0), COL_KSW // kvw)),
            pl.BlockSpec((tq, kvw), lambda b, n: (rows(b, n), COL_KSW // kvw)),
        ],
        out_specs=pl.BlockSpec((tq, SW_Q_W), lambda b, n: (rows(b, n), 0)),
        compiler_params=pltpu.CompilerParams(
            dimension_semantics=("parallel", "arbitrary"),
            vmem_limit_bytes=32 * 1024 * 1024),
        name="swa",
    )(sinks, proj, proj, proj)


def _merge_kernel(x_ref, ya_ref, yb_ref, ga_ref, gb_ref, wa_ref, wb_ref, wo_ref, out_ref):
    a = jax.nn.sigmoid(ga_ref[...].astype(F32)) * _dot(ya_ref[...], wa_ref[...])
    b = jax.nn.sigmoid(gb_ref[...].astype(F32)) * _dot(yb_ref[...], wb_ref[...])
    out_ref[...] = x_ref[...] + _dot((a + b).astype(BF16), wo_ref[...])


def _merge(x2d, ya, yb, proj, wa, wb, wo):
    t = x2d.shape[0]
    tm = TM_MERGE
    wspec = pl.BlockSpec((D_MODEL, D_MODEL), lambda i: (0, 0))
    return pl.pallas_call(
        _merge_kernel,
        out_shape=jax.ShapeDtypeStruct((t, D_MODEL), F32),
        grid=(t // tm,),
        in_specs=[
            pl.BlockSpec((tm, D_MODEL), lambda i: (i, 0)),
            pl.BlockSpec((tm, D_MODEL), lambda i: (i, 0)),
            pl.BlockSpec((tm, D_MODEL), lambda i: (i, 0)),
            pl.BlockSpec((tm, D_MODEL), lambda i: (i, COL_GA // D_MODEL)),
            pl.BlockSpec((tm, D_MODEL), lambda i: (i, COL_GB // D_MODEL)),
            wspec, wspec, wspec,
        ],
        out_specs=pl.BlockSpec((tm, D_MODEL), lambda i: (i, 0)),
        compiler_params=pltpu.CompilerParams(
            dimension_semantics=("parallel",),
            vmem_limit_bytes=48 * 1024 * 1024),
        name="merge_out",
    )(x2d, ya, yb, proj, proj, wa, wb, wo)


def _mlp_ple_kernel(x_ref, p_ref, gm_ref, wup_ref, wdn_ref, gp_ref, wg_ref, wp_ref, gf_ref, out_ref,
                    *, final_norm):
    x = x_ref[...]
    hn = _rms(x, gm_ref[...]).astype(BF16)
    acc = x
    for c in range(D_FF // FF_CHUNK):
        u = _dot(hn, wup_ref[:, c * FF_CHUNK:(c + 1) * FF_CHUNK])
        r = jnp.maximum(u, 0.0)
        acc = acc + _dot((r * r).astype(BF16), wdn_ref[c * FF_CHUNK:(c + 1) * FF_CHUNK, :])
    x = acc
    gate = jax.nn.sigmoid(_dot(_rms(x, gp_ref[...]).astype(BF16), wg_ref[...]))
    x = x + gate * _dot(p_ref[...].astype(BF16), wp_ref[...])
    if final_norm:
        x = _rms(x, gf_ref[...])
    out_ref[...] = x


def _mlp_ple(x2d, p2d, gm, wup, wdn, gp, wg, wp, gf, final_norm):
    t = x2d.shape[0]
    tm = TM_MLP

    def const(shape):
        return pl.BlockSpec(shape, lambda i: (0, 0))

    return pl.pallas_call(
        functools.partial(_mlp_ple_kernel, final_norm=final_norm),
        out_shape=jax.ShapeDtypeStruct((t, D_MODEL), F32),
        grid=(t // tm,),
        in_specs=[
            pl.BlockSpec((tm, D_MODEL), lambda i: (i, 0)),
            pl.BlockSpec((tm, PLE_DIM), lambda i: (i, 0)),
            const((1, D_MODEL)),
            const((D_MODEL, D_FF)),
            const((D_FF, D_MODEL)),
            const((1, D_MODEL)),
            const((D_MODEL, D_MODEL)),
            const((PLE_DIM, D_MODEL)),
            const((1, D_MODEL)),
        ],
        out_specs=pl.BlockSpec((tm, D_MODEL), lambda i: (i, 0)),
        compiler_params=pltpu.CompilerParams(
            dimension_semantics=("parallel",),
            vmem_limit_bytes=56 * 1024 * 1024),
        name="mlp_ple",
    )(x2d, p2d, gm, wup, wdn, gp, wg, wp, gf)


def _prep_w_in(w_in):
    o = 0
    parts = {}
    for name, size in (("qk", 2 * ML_QK_W), ("v", ML_V_W), ("o", ML_V_W), ("if", 2 * ML_HEADS),
                       ("qsw", SW_Q_W), ("ksw", SW_KV_W), ("vsw", SW_KV_W), ("ga", D_MODEL), ("gb", D_MODEL)):
        parts[name] = w_in[:, o:o + size]
        o += size

    def dup_heads(w):
        w = w.reshape(D_MODEL, SW_KV_HEADS, 1, SW_HEAD_DIM)
        return jnp.broadcast_to(w, (D_MODEL, SW_KV_HEADS, 2, SW_HEAD_DIM)).reshape(D_MODEL, 2 * SW_KV_W)

    w_main = jnp.concatenate([parts["qk"], parts["v"], parts["o"], parts["qsw"], dup_heads(parts["ksw"]),
                              dup_heads(parts["vsw"]), parts["ga"], parts["gb"]], axis=1).astype(BF16)
    wg = jnp.pad(parts["if"], ((0, 0), (0, LANES - 2 * ML_HEADS)))
    wg_hi = wg.astype(BF16)
    wg_lo = (wg - wg_hi.astype(F32)).astype(BF16)
    return w_main, wg_hi, wg_lo


def kernel(x, p, norm_mix_g, w_in, conv_qk, b_if, mlstm_norm_g, sinks, w_branch_a, w_branch_b, w_out,
           norm_mlp_g, w_up, w_down, norm_ple_g, w_ple_gate, w_ple_proj, final_norm_g):
    batch, seq, _ = x.shape
    depth = w_in.shape[0]
    t = batch * seq
    x2d = x.reshape(t, D_MODEL)
    row = lambda v: v.reshape(1, -1)
    for i in range(depth):
        w_main, wg_hi, wg_lo = _prep_w_in(w_in[i])
        proj, gates = _in_proj(x2d, row(norm_mix_g[i]), w_main, wg_hi, wg_lo)
        bif = jnp.pad(b_if[i], (0, LANES - 2 * ML_HEADS)).reshape(1, LANES)
        ya = _mlstm(proj, gates, conv_qk[i], bif, row(mlstm_norm_g[i]), batch, seq)
        yb = _swa(proj, sinks[i], batch, seq)
        x2d = _merge(x2d, ya, yb, proj, w_branch_a[i].astype(BF16), w_branch_b[i].astype(BF16),
                     w_out[i].astype(BF16))
        x2d = _mlp_ple(x2d, p[i].reshape(t, PLE_DIM), row(norm_mlp_g[i]), w_up[i].astype(BF16),
                       w_down[i].astype(BF16), row(norm_ple_g[i]), w_ple_gate[i].astype(BF16),
                       w_ple_proj[i].astype(BF16), row(final_norm_g), final_norm=(i == depth - 1))
    return x2d.reshape(batch, seq, D_MODEL)
```

```python
import functools

import jax
import jax.numpy as jnp
from jax import lax
from jax.experimental import pallas as pl
from jax.experimental.pallas import tpu as pltpu

D_MODEL = 1024
PLE_DIM = 256
ML_HEADS = 4
ML_DQK = 128
ML_DV = 256
ML_CONV = 4
SW_Q_HEADS = 16
SW_KV_HEADS = 4
SW_HEAD_DIM = 64
SW_WINDOW = 128
SW_GROUP = SW_Q_HEADS // SW_KV_HEADS
D_FF = 4 * D_MODEL
EPS = 1e-6

ML_QK_W = ML_HEADS * ML_DQK
ML_V_W = ML_HEADS * ML_DV
SW_Q_W = SW_Q_HEADS * SW_HEAD_DIM
SW_KV_W = SW_KV_HEADS * SW_HEAD_DIM

LANES = 128
BF16_SUBLANES = 16
MXU_N = 256

COL_QK = 0
COL_V = COL_QK + 2 * ML_QK_W
COL_O = COL_V + ML_V_W
COL_QSW = COL_O + ML_V_W
COL_KSW = COL_QSW + SW_Q_W
COL_VSW = COL_KSW + 2 * SW_KV_W
N_SLAB = COL_VSW + 2 * SW_KV_W
COL_GA = N_SLAB
COL_GB = COL_GA + D_MODEL
N_PROJ = COL_GB + D_MODEL
N_GATES = 2 * D_MODEL

ML_CHUNK = 128
ML_SKEW = 2
SW_SKEW = 2
TT_FRONT = 512
PROJ_PIECE = 256
TM_MERGE = 512
TM_MLP = 512
FF_CHUNK = 1024

F32 = jnp.float32
BF16 = jnp.bfloat16


def _dot(a, b):
    return jnp.dot(a, b, preferred_element_type=F32)


def _dot_nt(a, b):
    return lax.dot_general(a, b, (((1,), (1,)), ((), ())), preferred_element_type=F32)


def _rms(x, g):
    return x * lax.rsqrt(jnp.mean(x * x, axis=-1, keepdims=True) + EPS) * g


def _split_bf16(x, parts):
    out = []
    r = x
    for _ in range(parts - 1):
        t = r.astype(BF16)
        out.append(t)
        r = r - t.astype(F32)
    out.append(r.astype(BF16))
    return out


def _interleave(*lists):
    tagged = []
    for k, lst in enumerate(lists):
        tagged += [((i + 0.5) / len(lst), k, i, f) for i, f in enumerate(lst)]
    return [f for _, _, _, f in sorted(tagged, key=lambda e: e[:3])]


def _projection_steps(x_ref, g_ref, w_ref, wg_hi_ref, wg_lo_ref, hn_ref, slab_ref, gates_ref, gab_ref):
    def norm():
        h = _rms(x_ref[...], g_ref[...])
        h_hi, h_lo = _split_bf16(h, 2)
        hn_ref[...] = h_hi
        gates_ref[...] = (_dot(h_hi, wg_hi_ref[...]) + _dot(h_lo, wg_hi_ref[...])
                          + _dot(h_hi, wg_lo_ref[...]))

    def piece(c0):
        def run():
            val = _dot(hn_ref[...], w_ref[:, c0:c0 + PROJ_PIECE]).astype(BF16)
            if c0 < N_SLAB:
                slab_ref[:, c0:c0 + PROJ_PIECE] = val
            else:
                gab_ref[:, c0 - N_SLAB:c0 - N_SLAB + PROJ_PIECE] = val
        return run

    return norm, [piece(c0) for c0 in range(0, N_PROJ, PROJ_PIECE)]


def _mlstm_steps(slab_ref, gates_ref, qk_tail_ref, first, convw_ref, bif_ref, gn_ref, y_ref,
                 shift_ref, c_ref, n_ref, m_ref, n_chunks):
    L = ML_CHUNK
    H = BF16_SUBLANES
    taps = ML_CONV - 1
    row = lax.broadcasted_iota(jnp.int32, (L, L), 0)
    col = lax.broadcasted_iota(jnp.int32, (L, L), 1)
    causal = row >= col
    tril = jnp.where(causal, 1.0, 0.0).astype(BF16)
    cw = convw_ref[...]
    bif = bif_ref[...]

    def chunk_prep(c):
        r0 = c * L
        x_cur = slab_ref[r0:r0 + L, COL_QK:COL_QK + 2 * ML_QK_W]
        if c == 0:
            halo = jnp.where(first, jnp.zeros((), BF16), qk_tail_ref[...])
            x_ext = jnp.concatenate([halo, x_cur], axis=0)
        else:
            x_ext = slab_ref[r0 - H:r0 + L, COL_QK:COL_QK + 2 * ML_QK_W]
        shifted = _dot(shift_ref[...], x_ext)
        acc = cw[taps:taps + 1, :] * x_cur.astype(F32)
        for j in range(taps):
            acc = acc + cw[j:j + 1, :] * shifted[j * L:(j + 1) * L, :]
        act = acc * jax.nn.sigmoid(acc)
        gates = gates_ref[r0:r0 + L, :] + bif
        logf = jnp.minimum(gates, 0.0) - jnp.log1p(jnp.exp(-jnp.abs(gates)))
        bsplit = _dot(tril, jnp.concatenate(_split_bf16(logf, 3), axis=1))
        bcum = bsplit[:, :LANES] + bsplit[:, LANES:2 * LANES] + bsplit[:, 2 * LANES:]
        return dict(q=act[:, :ML_QK_W] * (ML_DQK ** -0.5), k=act[:, ML_QK_W:], gates=gates, bcum=bcum,
                    gates_t=gates.T, bcum_t=bcum.T)

    def local_part(c, h, pre):
        r0 = c * L
        fl = ML_HEADS + h
        b_c = pre["bcum"][:, fl:fl + 1]
        i_c = pre["gates"][:, h:h + 1]
        r_r = pre["gates_t"][h:h + 1, :] - pre["bcum_t"][fl:fl + 1, :]
        qf = pre["q"][:, h * ML_DQK:(h + 1) * ML_DQK]
        kf = pre["k"][:, h * ML_DQK:(h + 1) * ML_DQK]
        q = qf.astype(BF16)
        kt = kf.T.astype(BF16)
        v = slab_ref[r0:r0 + L, COL_V + h * ML_DV:COL_V + (h + 1) * ML_DV]

        log_d = jnp.where(causal, b_c + r_r, -jnp.inf)
        a = jnp.max(log_d, axis=1, keepdims=True)
        s_loc = _dot(q, kt) * jnp.exp(log_d - a)
        b_last = pre["bcum"][L - 1:L, fl:fl + 1]
        log_w = b_last - b_c + i_c
        aw = jnp.max(log_w, axis=0, keepdims=True)
        w_loc = jnp.exp(log_w - aw)
        return dict(q=q, qf=qf, b_c=b_c, a=a, b_last=b_last, aw=aw,
                    sv=_dot(s_loc.astype(BF16), v), rs=jnp.sum(s_loc, axis=1, keepdims=True),
                    u=_dot(kt, (w_loc * v.astype(F32)).astype(BF16)),
                    nu=jnp.sum(w_loc * kf, axis=0, keepdims=True))

    def carried_part(c, h, loc, state):
        r0 = c * L
        ct, n_prev, m_prev = state
        inter = loc["b_c"] + m_prev
        m_t = jnp.maximum(inter, loc["a"])
        w_inter = jnp.exp(inter - m_t)
        f_loc = jnp.exp(loc["a"] - m_t)
        num = w_inter * _dot(loc["q"], ct.astype(BF16)) + f_loc * loc["sv"]
        den = w_inter * jnp.sum(loc["qf"] * n_prev, axis=1, keepdims=True) + f_loc * loc["rs"]
        rden = 1.0 / jnp.maximum(jnp.abs(den), jnp.exp(-m_t))
        ms = jnp.mean(num * num, axis=1, keepdims=True)
        scale = rden * lax.rsqrt(rden * rden * ms + EPS)
        og = jax.nn.sigmoid(slab_ref[r0:r0 + L, COL_O + h * ML_DV:COL_O + (h + 1) * ML_DV].astype(F32))
        y_ref[r0:r0 + L, h * ML_DV:(h + 1) * ML_DV] = (
            og * (num * scale * gn_ref[:, h * ML_DV:(h + 1) * ML_DV])).astype(BF16)
        m_new = jnp.maximum(loc["b_last"] + m_prev, loc["aw"])
        decay = jnp.exp(loc["b_last"] + m_prev - m_new)
        g_loc = jnp.exp(loc["aw"] - m_new)
        return decay * ct + g_loc * loc["u"], decay * n_prev + g_loc * loc["nu"], m_new

    units = [(c, h) for c in range(n_chunks) for h in range(ML_HEADS)]
    n_units = len(units)
    state, pre, loc = {}, {}, {}

    def step(i):
        def run():
            if i == 0:
                for h in range(ML_HEADS):
                    state[h] = (c_ref[h], n_ref[h], m_ref[h])
            if i < n_units:
                c, h = units[i]
                if h == 0:
                    pre[c] = chunk_prep(c)
                loc[i] = local_part(c, h, pre[c])
            if i - ML_SKEW >= 0:
                c, h = units[i - ML_SKEW]
                state[h] = carried_part(c, h, loc.pop(i - ML_SKEW), state[h])
            if i == n_units + ML_SKEW - 1:
                for h in range(ML_HEADS):
                    c_ref[h], n_ref[h], m_ref[h] = state[h]
        return run

    return [step(i) for i in range(n_units + ML_SKEW)]


def _swa_steps(slab_ref, kv_tail_ref, first, sinks_ref, y_ref, n_blocks):
    W = SW_WINDOW
    hd = SW_HEAD_DIM
    qi = lax.broadcasted_iota(jnp.int32, (W, 2 * W), 0)
    key = lax.broadcasted_iota(jnp.int32, (W, 2 * W), 1) % W
    from_prev = key > qi
    lo_half = lax.broadcasted_iota(jnp.int32, (W, LANES), 1) < hd
    zero = jnp.zeros((), BF16)
    prev_bias = jnp.where(first, -jnp.inf, 0.0).astype(F32)

    def block_diag(d):
        return jnp.concatenate([jnp.where(lo_half, d, zero), jnp.where(lo_half, zero, d)], axis=0)

    cache = {}

    def kv_block(j, col0, h):
        if (j, col0, h) not in cache:
            if j < 0:
                d = kv_tail_ref[:, col0 - COL_KSW + h * LANES:col0 - COL_KSW + (h + 1) * LANES]
            else:
                d = slab_ref[j * W:(j + 1) * W, col0 + h * LANES:col0 + (h + 1) * LANES]
            cache[j, col0, h] = block_diag(d)
        return cache[j, col0, h]

    units = [(j, h, pair) for j in range(n_blocks) for h in range(SW_KV_HEADS)
             for pair in range(SW_GROUP // 2)]

    def logits_of(j, h, pair):
        c0 = COL_QSW + (h * SW_GROUP + 2 * pair) * hd
        qp = slab_ref[j * W:(j + 1) * W, c0:c0 + LANES] * jnp.asarray(hd ** -0.5, BF16)
        s_prev = _dot_nt(qp, kv_block(j - 1, COL_KSW, h))
        if j == 0:
            s_prev = s_prev + prev_bias
        return jnp.where(from_prev, s_prev, _dot_nt(qp, kv_block(j, COL_KSW, h)))

    def probs_of(lg, h, pair):
        halves = []
        for e in range(2):
            sink = sinks_ref[h * SW_GROUP + 2 * pair + e]
            le = lg[:, e * W:(e + 1) * W]
            m = jnp.maximum(jnp.max(le, axis=1, keepdims=True), sink)
            p = jnp.exp(le - m)
            denom = jnp.sum(p, axis=1, keepdims=True) + jnp.exp(sink - m)
            halves.append((p * (1.0 / denom)).astype(BF16))
        return jnp.concatenate(halves, axis=1)

    def write_out(pb, j, h, pair):
        c0 = (h * SW_GROUP + 2 * pair) * hd
        out = (_dot(jnp.where(from_prev, pb, zero), kv_block(j - 1, COL_VSW, h))
               + _dot(jnp.where(from_prev, zero, pb), kv_block(j, COL_VSW, h)))
        y_ref[j * W:(j + 1) * W, c0:c0 + LANES] = out.astype(BF16)

    logits, probs = {}, {}
    n = len(units)

    def step(i):
        def run():
            if i < n:
                logits[i] = logits_of(*units[i])
            if 0 <= i - SW_SKEW < n:
                _, h, pair = units[i - SW_SKEW]
                probs[i - SW_SKEW] = probs_of(logits.pop(i - SW_SKEW), h, pair)
            if 0 <= i - 2 * SW_SKEW < n:
                write_out(probs.pop(i - 2 * SW_SKEW), *units[i - 2 * SW_SKEW])
        return run

    return [step(i) for i in range(n + 2 * SW_SKEW)]


def _front_kernel(sinks_ref, x_ref, g_ref, w_ref, wg_hi_ref, wg_lo_ref, convw_ref, bif_ref, gn_ref,
                  ya_ref, yb_ref, gab_ref,
                  hn_ref, slab_new_ref, slab_ref, gates_new_ref, gates_ref, kv_tail_ref, qk_tail_ref,
                  shift_ref, c_ref, n_ref, m_ref, *, tiles_per_seq):
    tt = TT_FRONT
    L = ML_CHUNK
    H = BF16_SUBLANES
    taps = ML_CONV - 1
    s = pl.program_id(0)
    first = (jnp.maximum(s - 1, 0) % tiles_per_seq) == 0

    @pl.when(s == 0)
    def _():
        slab_ref[...] = jnp.zeros_like(slab_ref)
        gates_ref[...] = jnp.zeros_like(gates_ref)
        kv_tail_ref[...] = jnp.zeros_like(kv_tail_ref)
        qk_tail_ref[...] = jnp.zeros_like(qk_tail_ref)
        r = lax.broadcasted_iota(jnp.int32, (taps * L, H + L), 0)
        u = lax.broadcasted_iota(jnp.int32, (taps * L, H + L), 1)
        src = (r % L) + (r // L) + (H - taps)
        shift_ref[...] = jnp.where(u == src, 1.0, 0.0).astype(BF16)

    @pl.when(first)
    def _():
        c_ref[...] = jnp.zeros_like(c_ref)
        n_ref[...] = jnp.zeros_like(n_ref)
        m_ref[...] = jnp.zeros_like(m_ref)

    norm, pieces = _projection_steps(x_ref, g_ref, w_ref, wg_hi_ref, wg_lo_ref, hn_ref,
                                     slab_new_ref, gates_new_ref, gab_ref)
    ml = _mlstm_steps(slab_ref, gates_ref, qk_tail_ref, first, convw_ref, bif_ref, gn_ref, ya_ref,
                      shift_ref, c_ref, n_ref, m_ref, tt // L)
    sw = _swa_steps(slab_ref, kv_tail_ref, first, sinks_ref, yb_ref, tt // SW_WINDOW)
    norm()
    for thunk in _interleave(pieces, ml, sw):
        thunk()

    kv_tail_ref[...] = slab_ref[tt - SW_WINDOW:tt, COL_KSW:N_SLAB]
    qk_tail_ref[...] = slab_ref[tt - H:tt, COL_QK:COL_QK + 2 * ML_QK_W]
    slab_ref[...] = slab_new_ref[...]
    gates_ref[...] = gates_new_ref[...]


def _front(x2d, sinks, g, w_main, wg_hi, wg_lo, conv_w, b_if, gn, seq):
    t = x2d.shape[0]
    tt = TT_FRONT
    nt = t // tt
    W = SW_WINDOW

    def const(shape):
        return pl.BlockSpec(shape, lambda s: (0, 0), pipeline_mode=pl.Buffered(1))

    proj_tile = lambda s: (jnp.minimum(s, nt - 1), 0)
    mix_tile = lambda s: (jnp.maximum(s - 1, 0), 0)
    return pl.pallas_call(
        functools.partial(_front_kernel, tiles_per_seq=seq // tt),
        out_shape=(jax.ShapeDtypeStruct((t, ML_V_W), BF16),
                   jax.ShapeDtypeStruct((t, SW_Q_W), BF16),
                   jax.ShapeDtypeStruct((t, N_GATES), BF16)),
        grid=(nt + 1,),
        in_specs=[
            pl.BlockSpec(memory_space=pltpu.SMEM),
            pl.BlockSpec((tt, D_MODEL), proj_tile),
            const((1, D_MODEL)),
            const((D_MODEL, N_PROJ)),
            const((D_MODEL, LANES)),
            const((D_MODEL, LANES)),
            const((ML_CONV, 2 * ML_QK_W)),
            const((1, LANES)),
            const((1, ML_V_W)),
        ],
        out_specs=(pl.BlockSpec((tt, ML_V_W), mix_tile),
                   pl.BlockSpec((tt, SW_Q_W), mix_tile),
                   pl.BlockSpec((tt, N_GATES), proj_tile)),
        scratch_shapes=[
            pltpu.VMEM((tt, D_MODEL), BF16),
            pltpu.VMEM((tt, N_SLAB), BF16),
            pltpu.VMEM((tt, N_SLAB), BF16),
            pltpu.VMEM((tt, LANES), F32),
            pltpu.VMEM((tt, LANES), F32),
            pltpu.VMEM((W, N_SLAB - COL_KSW), BF16),
            pltpu.VMEM((BF16_SUBLANES, 2 * ML_QK_W), BF16),
            pltpu.VMEM(((ML_CONV - 1) * ML_CHUNK, BF16_SUBLANES + ML_CHUNK), BF16),
            pltpu.VMEM((ML_HEADS, ML_DQK, ML_DV), F32),
            pltpu.VMEM((ML_HEADS, 1, ML_DQK), F32),
            pltpu.VMEM((ML_HEADS, 1, 1), F32),
        ],
        compiler_params=pltpu.CompilerParams(
            dimension_semantics=("arbitrary",),
            vmem_limit_bytes=56 * 1024 * 1024),
        name="front",
    )(sinks, x2d, g, w_main, wg_hi, wg_lo, conv_w, b_if, gn)


def _merge_kernel(x_ref, ya_ref, yb_ref, ga_ref, gb_ref, wa_ref, wb_ref, wo_ref, out_ref):
    a = jax.nn.sigmoid(ga_ref[...].astype(F32)) * _dot(ya_ref[...], wa_ref[...])
    b = jax.nn.sigmoid(gb_ref[...].astype(F32)) * _dot(yb_ref[...], wb_ref[...])
    out_ref[...] = x_ref[...] + _dot((a + b).astype(BF16), wo_ref[...])


def _merge(x2d, ya, yb, gab, wa, wb, wo):
    t = x2d.shape[0]
    tm = TM_MERGE
    wspec = pl.BlockSpec((D_MODEL, D_MODEL), lambda i: (0, 0), pipeline_mode=pl.Buffered(1))
    return pl.pallas_call(
        _merge_kernel,
        out_shape=jax.ShapeDtypeStruct((t, D_MODEL), F32),
        grid=(t // tm,),
        in_specs=[
            pl.BlockSpec((tm, D_MODEL), lambda i: (i, 0)),
            pl.BlockSpec((tm, D_MODEL), lambda i: (i, 0)),
            pl.BlockSpec((tm, D_MODEL), lambda i: (i, 0)),
            pl.BlockSpec((tm, D_MODEL), lambda i: (i, 0)),
            pl.BlockSpec((tm, D_MODEL), lambda i: (i, 1)),
            wspec, wspec, wspec,
        ],
        out_specs=pl.BlockSpec((tm, D_MODEL), lambda i: (i, 0)),
        compiler_params=pltpu.CompilerParams(
            dimension_semantics=("parallel",),
            vmem_limit_bytes=48 * 1024 * 1024),
        name="merge_out",
    )(x2d, ya, yb, gab, gab, wa, wb, wo)


def _mlp_ple_kernel(x_ref, p_ref, gm_ref, wup_ref, wdn_ref, gp_ref, wg_ref, wp_ref, gf_ref, out_ref,
                    *, final_norm):
    x = x_ref[...]
    hn = _rms(x, gm_ref[...]).astype(BF16)
    acc = x
    for c in range(D_FF // FF_CHUNK):
        u = _dot(hn, wup_ref[:, c * FF_CHUNK:(c + 1) * FF_CHUNK])
        r = jnp.maximum(u, 0.0)
        acc = acc + _dot((r * r).astype(BF16), wdn_ref[c * FF_CHUNK:(c + 1) * FF_CHUNK, :])
    x = acc
    gate = jax.nn.sigmoid(_dot(_rms(x, gp_ref[...]).astype(BF16), wg_ref[...]))
    x = x + gate * _dot(p_ref[...].astype(BF16), wp_ref[...])
    if final_norm:
        x = _rms(x, gf_ref[...])
    out_ref[...] = x


def _mlp_ple(x2d, p2d, gm, wup, wdn, gp, wg, wp, gf, final_norm):
    t = x2d.shape[0]
    tm = TM_MLP

    def const(shape):
        return pl.BlockSpec(shape, lambda i: (0, 0), pipeline_mode=pl.Buffered(1))

    return pl.pallas_call(
        functools.partial(_mlp_ple_kernel, final_norm=final_norm),
        out_shape=jax.ShapeDtypeStruct((t, D_MODEL), F32),
        grid=(t // tm,),
        in_specs=[
            pl.BlockSpec((tm, D_MODEL), lambda i: (i, 0)),
            pl.BlockSpec((tm, PLE_DIM), lambda i: (i, 0)),
            const((1, D_MODEL)),
            const((D_MODEL, D_FF)),
            const((D_FF, D_MODEL)),
            const((1, D_MODEL)),
            const((D_MODEL, D_MODEL)),
            const((PLE_DIM, D_MODEL)),
            const((1, D_MODEL)),
        ],
        out_specs=pl.BlockSpec((tm, D_MODEL), lambda i: (i, 0)),
        compiler_params=pltpu.CompilerParams(
            dimension_semantics=("parallel",),
            vmem_limit_bytes=56 * 1024 * 1024),
        name="mlp_ple",
    )(x2d, p2d, gm, wup, wdn, gp, wg, wp, gf)


def _prep_w_in(w_in):
    o = 0
    parts = {}
    for name, size in (("qk", 2 * ML_QK_W), ("v", ML_V_W), ("o", ML_V_W), ("if", 2 * ML_HEADS),
                       ("qsw", SW_Q_W), ("ksw", SW_KV_W), ("vsw", SW_KV_W), ("ga", D_MODEL), ("gb", D_MODEL)):
        parts[name] = w_in[:, o:o + size]
        o += size

    def dup_heads(w):
        w = w.reshape(D_MODEL, SW_KV_HEADS, 1, SW_HEAD_DIM)
        return jnp.broadcast_to(w, (D_MODEL, SW_KV_HEADS, 2, SW_HEAD_DIM)).reshape(D_MODEL, 2 * SW_KV_W)

    w_main = jnp.concatenate([parts["qk"], parts["v"], parts["o"], parts["qsw"], dup_heads(parts["ksw"]),
                              dup_heads(parts["vsw"]), parts["ga"], parts["gb"]], axis=1).astype(BF16)
    wg = jnp.pad(parts["if"], ((0, 0), (0, LANES - 2 * ML_HEADS)))
    wg_hi = wg.astype(BF16)
    wg_lo = (wg - wg_hi.astype(F32)).astype(BF16)
    return w_main, wg_hi, wg_lo


def kernel(x, p, norm_mix_g, w_in, conv_qk, b_if, mlstm_norm_g, sinks, w_branch_a, w_branch_b, w_out,
           norm_mlp_g, w_up, w_down, norm_ple_g, w_ple_gate, w_ple_proj, final_norm_g):
    batch, seq, _ = x.shape
    depth = w_in.shape[0]
    t = batch * seq
    x2d = x.reshape(t, D_MODEL)
    row = lambda v: v.reshape(1, -1)
    for i in range(depth):
        w_main, wg_hi, wg_lo = _prep_w_in(w_in[i])
        bif = jnp.pad(b_if[i], (0, LANES - 2 * ML_HEADS)).reshape(1, LANES)
        ya, yb, gab = _front(x2d, sinks[i], row(norm_mix_g[i]), w_main, wg_hi, wg_lo, conv_qk[i], bif,
                             row(mlstm_norm_g[i]), seq)
        x2d = _merge(x2d, ya, yb, gab, w_branch_a[i].astype(BF16), w_branch_b[i].astype(BF16),
                     w_out[i].astype(BF16))
        x2d = _mlp_ple(x2d, p[i].reshape(t, PLE_DIM), row(norm_mlp_g[i]), w_up[i].astype(BF16),
                       w_down[i].astype(BF16), row(norm_ple_g[i]), w_ple_gate[i].astype(BF16),
                       w_ple_proj[i].astype(BF16), row(final_norm_g), final_norm=(i == depth - 1))
    return x2d.reshape(batch, seq, D_MODEL)
```

```python
import functools

import jax
import jax.numpy as jnp
from jax import lax
from jax.experimental import pallas as pl
from jax.experimental.pallas import tpu as pltpu

D_MODEL = 1024
PLE_DIM = 256
ML_HEADS = 4
ML_DQK = 128
ML_DV = 256
ML_CONV = 4
SW_Q_HEADS = 16
SW_KV_HEADS = 4
SW_HEAD_DIM = 64
SW_WINDOW = 128
SW_GROUP = SW_Q_HEADS // SW_KV_HEADS
D_FF = 4 * D_MODEL
EPS = 1e-6

ML_QK_W = ML_HEADS * ML_DQK
ML_V_W = ML_HEADS * ML_DV
SW_Q_W = SW_Q_HEADS * SW_HEAD_DIM
SW_KV_W = SW_KV_HEADS * SW_HEAD_DIM

LANES = 128
BF16_SUBLANES = 16
MXU_N = 256

COL_QK = 0
COL_V = COL_QK + 2 * ML_QK_W
COL_O = COL_V + ML_V_W
COL_QSW = COL_O + ML_V_W
COL_KSW = COL_QSW + SW_Q_W
COL_VSW = COL_KSW + 2 * SW_KV_W
N_SLAB = COL_VSW + 2 * SW_KV_W
COL_GA = N_SLAB
COL_GB = COL_GA + D_MODEL
N_PROJ = COL_GB + D_MODEL
N_GATES = 2 * D_MODEL
WCOL_KSW = COL_KSW
WCOL_VSW = WCOL_KSW + SW_KV_W
WCOL_GA = WCOL_VSW + SW_KV_W
N_W = WCOL_GA + N_GATES

ML_CHUNK = 128
ML_SKEW = 1
SW_SKEW = 2
TT_FRONT = 512
PROJ_PIECE = 256
TM_MERGE = 512
TM_MLP = 512
FF_CHUNK = 1024

F32 = jnp.float32
BF16 = jnp.bfloat16


def _dot(a, b):
    return jnp.dot(a, b, preferred_element_type=F32)


def _dot_nt(a, b):
    return lax.dot_general(a, b, (((1,), (1,)), ((), ())), preferred_element_type=F32)


def _rms(x, g):
    return x * lax.rsqrt(jnp.mean(x * x, axis=-1, keepdims=True) + EPS) * g


def _split_bf16(x, parts):
    out = []
    r = x
    for _ in range(parts - 1):
        t = r.astype(BF16)
        out.append(t)
        r = r - t.astype(F32)
    out.append(r.astype(BF16))
    return out


def _interleave(*lists):
    tagged = []
    for k, lst in enumerate(lists):
        tagged += [((i + 0.5) / len(lst), k, i, f) for i, f in enumerate(lst)]
    return [f for _, _, _, f in sorted(tagged, key=lambda e: e[:3])]


def _projection_steps(x_ref, g_ref, w_ref, wg_ref, hn_ref, hlo_ref, slab_ref, gates_ref, gab_ref):
    lo_half = lax.broadcasted_iota(jnp.int32, (TT_FRONT, LANES), 1) < SW_HEAD_DIM

    def norm():
        h = _rms(x_ref[...], g_ref[...])
        h_hi, h_lo = _split_bf16(h, 2)
        hn_ref[...] = h_hi
        hlo_ref[...] = h_lo

    def gates():
        hi = _dot(hn_ref[...], wg_ref[...])
        gates_ref[...] = hi[:, :LANES] + hi[:, LANES:] + _dot(hlo_ref[...], wg_ref[:, :LANES])

    def dup_heads(val):
        out = []
        for j in range(val.shape[1] // LANES):
            vj = val[:, j * LANES:(j + 1) * LANES]
            rj = pltpu.roll(vj, SW_HEAD_DIM, axis=1)
            out += [jnp.where(lo_half, vj, rj), jnp.where(lo_half, rj, vj)]
        return jnp.concatenate(out, axis=1)

    def piece(c0):
        def run():
            val = _dot(hn_ref[...], w_ref[:, c0:c0 + PROJ_PIECE])
            if c0 == WCOL_KSW:
                slab_ref[:, COL_KSW:COL_VSW] = dup_heads(val).astype(BF16)
            elif c0 == WCOL_VSW:
                slab_ref[:, COL_VSW:N_SLAB] = dup_heads(val).astype(BF16)
            elif c0 < WCOL_KSW:
                slab_ref[:, c0:c0 + PROJ_PIECE] = val.astype(BF16)
            else:
                gab_ref[:, c0 - WCOL_GA:c0 - WCOL_GA + PROJ_PIECE] = val.astype(BF16)
        return run

    return norm, [gates] + [piece(c0) for c0 in range(0, N_W, PROJ_PIECE)]


def _mlstm_steps(slab_ref, gates_ref, qk_tail_ref, first, convw_ref, bif_ref, gn_ref, y_ref,
                 c_ref, n_ref, m_ref, n_chunks):
    L = ML_CHUNK
    H = BF16_SUBLANES
    taps = ML_CONV - 1
    row = lax.broadcasted_iota(jnp.int32, (L, L), 0)
    col = lax.broadcasted_iota(jnp.int32, (L, L), 1)
    causal = row >= col
    tril = jnp.where(causal, 1.0, 0.0).astype(BF16)
    cw = convw_ref[...]
    bif = bif_ref[...]
    row8 = lax.broadcasted_iota(jnp.int32, (8, 2 * ML_QK_W), 0)

    def chunk_prep(c):
        r0 = c * L
        x_cur = slab_ref[r0:r0 + L, COL_QK:COL_QK + 2 * ML_QK_W].astype(F32)
        if c == 0:
            halo = jnp.where(first, 0.0, qk_tail_ref[...].astype(F32))[H - 8:, :]
        else:
            halo = slab_ref[r0 - H:r0, COL_QK:COL_QK + 2 * ML_QK_W].astype(F32)[H - 8:, :]
        acc = cw[taps:taps + 1, :] * x_cur
        for k in range(1, taps + 1):
            rolled = pltpu.roll(x_cur, k, axis=0)
            top = jnp.where(row8 < k, pltpu.roll(halo, k, axis=0), rolled[:8, :])
            acc = acc + cw[taps - k:taps - k + 1, :] * jnp.concatenate([top, rolled[8:, :]], axis=0)
        act = acc * jax.nn.sigmoid(acc)
        gates = gates_ref[r0:r0 + L, :] + bif
        logf = jnp.minimum(gates, 0.0) - jnp.log1p(jnp.exp(-jnp.abs(gates)))
        bsplit = _dot(tril, jnp.concatenate(_split_bf16(logf, 3), axis=1))
        bcum = bsplit[:, :LANES] + bsplit[:, LANES:2 * LANES] + bsplit[:, 2 * LANES:]
        return dict(q=act[:, :ML_QK_W] * (ML_DQK ** -0.5), k=act[:, ML_QK_W:], gates=gates, bcum=bcum,
                    gates_t=gates.T, bcum_t=bcum.T)

    def qk_part(c, h, pre):
        qf = pre["q"][:, h * ML_DQK:(h + 1) * ML_DQK]
        kf = pre["k"][:, h * ML_DQK:(h + 1) * ML_DQK]
        q = qf.astype(BF16)
        kt = kf.T.astype(BF16)
        return dict(q=q, qf=qf, kf=kf, kt=kt, qk=_dot(q, kt))

    def local_part(c, h, pre, qk):
        r0 = c * L
        fl = ML_HEADS + h
        b_c = pre["bcum"][:, fl:fl + 1]
        i_c = pre["gates"][:, h:h + 1]
        r_r = pre["gates_t"][h:h + 1, :] - pre["bcum_t"][fl:fl + 1, :]
        v = slab_ref[r0:r0 + L, COL_V + h * ML_DV:COL_V + (h + 1) * ML_DV]

        log_d = jnp.where(causal, b_c + r_r, -jnp.inf)
        a = jnp.max(log_d, axis=1, keepdims=True)
        s_loc = qk["qk"] * jnp.exp(log_d - a)
        b_last = pre["bcum"][L - 1:L, fl:fl + 1]
        log_w = b_last - b_c + i_c
        aw = jnp.max(log_w, axis=0, keepdims=True)
        w_loc = jnp.exp(log_w - aw)
        return dict(q=qk["q"], qf=qk["qf"], b_c=b_c, a=a, b_last=b_last, aw=aw,
                    sv=_dot(s_loc.astype(BF16), v), rs=jnp.sum(s_loc, axis=1, keepdims=True),
                    u=_dot(qk["kt"], (w_loc * v.astype(F32)).astype(BF16)),
                    nu=jnp.sum(w_loc * qk["kf"], axis=0, keepdims=True))

    def carried_part(c, h, loc, state):
        r0 = c * L
        ct, n_prev, m_prev = state
        inter = loc["b_c"] + m_prev
        m_t = jnp.maximum(inter, loc["a"])
        w_inter = jnp.exp(inter - m_t)
        f_loc = jnp.exp(loc["a"] - m_t)
        num = w_inter * _dot(loc["q"], ct.astype(BF16)) + f_loc * loc["sv"]
        den = w_inter * jnp.sum(loc["qf"] * n_prev, axis=1, keepdims=True) + f_loc * loc["rs"]
        rden = 1.0 / jnp.maximum(jnp.abs(den), jnp.exp(-m_t))
        ms = jnp.mean(num * num, axis=1, keepdims=True)
        scale = rden * lax.rsqrt(rden * rden * ms + EPS)
        og = jax.nn.sigmoid(slab_ref[r0:r0 + L, COL_O + h * ML_DV:COL_O + (h + 1) * ML_DV].astype(F32))
        y_ref[r0:r0 + L, h * ML_DV:(h + 1) * ML_DV] = (
            og * (num * scale * gn_ref[:, h * ML_DV:(h + 1) * ML_DV])).astype(BF16)
        m_new = jnp.maximum(loc["b_last"] + m_prev, loc["aw"])
        decay = jnp.exp(loc["b_last"] + m_prev - m_new)
        g_loc = jnp.exp(loc["aw"] - m_new)
        return decay * ct + g_loc * loc["u"], decay * n_prev + g_loc * loc["nu"], m_new

    units = [(c, h) for c in range(n_chunks) for h in range(ML_HEADS)]
    n_units = len(units)
    state, pre, qk, loc = {}, {}, {}, {}
    n_ticks = n_units + 2 * ML_SKEW

    def step(i):
        def run():
            if i == 0:
                for h in range(ML_HEADS):
                    state[h] = (c_ref[h], n_ref[h], m_ref[h])
                pre[0] = chunk_prep(0)
            if i + 1 < n_units and units[i + 1][1] == 0:
                pre[units[i + 1][0]] = chunk_prep(units[i + 1][0])
            if i < n_units:
                c, h = units[i]
                qk[i] = qk_part(c, h, pre[c])
            j = i - ML_SKEW
            if 0 <= j < n_units:
                c, h = units[j]
                loc[j] = local_part(c, h, pre[c], qk.pop(j))
            j = i - 2 * ML_SKEW
            if 0 <= j < n_units:
                c, h = units[j]
                state[h] = carried_part(c, h, loc.pop(j), state[h])
            if i == n_ticks - 1:
                for h in range(ML_HEADS):
                    c_ref[h], n_ref[h], m_ref[h] = state[h]
        return run

    return [step(i) for i in range(n_ticks)]


def _swa_steps(slab_ref, kv_tail_ref, first, sinks_ref, y_ref, n_blocks):
    W = SW_WINDOW
    hd = SW_HEAD_DIM
    qi = lax.broadcasted_iota(jnp.int32, (W, 2 * W), 0)
    key = lax.broadcasted_iota(jnp.int32, (W, 2 * W), 1) % W
    from_prev = key > qi
    lo_half = lax.broadcasted_iota(jnp.int32, (W, LANES), 1) < hd
    zero = jnp.zeros((), BF16)
    prev_bias = jnp.where(first, -jnp.inf, 0.0).astype(F32)

    def block_diag(d):
        return jnp.concatenate([jnp.where(lo_half, d, zero), jnp.where(lo_half, zero, d)], axis=0)

    cache = {}

    def kv_block(j, col0, h):
        if (j, col0, h) not in cache:
            if j < 0:
                d = kv_tail_ref[:, col0 - COL_KSW + h * LANES:col0 - COL_KSW + (h + 1) * LANES]
            else:
                d = slab_ref[j * W:(j + 1) * W, col0 + h * LANES:col0 + (h + 1) * LANES]
            cache[j, col0, h] = block_diag(d)
        return cache[j, col0, h]

    units = [(j, h, pair) for j in range(n_blocks) for h in range(SW_KV_HEADS)
             for pair in range(SW_GROUP // 2)]

    def logits_of(j, h, pair):
        c0 = COL_QSW + (h * SW_GROUP + 2 * pair) * hd
        qp = slab_ref[j * W:(j + 1) * W, c0:c0 + LANES] * jnp.asarray(hd ** -0.5, BF16)
        s_prev = _dot_nt(qp, kv_block(j - 1, COL_KSW, h))
        if j == 0:
            s_prev = s_prev + prev_bias
        return jnp.where(from_prev, s_prev, _dot_nt(qp, kv_block(j, COL_KSW, h)))

    def probs_of(lg, h, pair):
        halves = []
        for e in range(2):
            sink = sinks_ref[h * SW_GROUP + 2 * pair + e]
            le = lg[:, e * W:(e + 1) * W]
            m = jnp.maximum(jnp.max(le, axis=1, keepdims=True), sink)
            p = jnp.exp(le - m)
            denom = jnp.sum(p, axis=1, keepdims=True) + jnp.exp(sink - m)
            halves.append((p * (1.0 / denom)).astype(BF16))
        return jnp.concatenate(halves, axis=1)

    def write_out(pb, j, h, pair):
        c0 = (h * SW_GROUP + 2 * pair) * hd
        out = (_dot(jnp.where(from_prev, pb, zero), kv_block(j - 1, COL_VSW, h))
               + _dot(jnp.where(from_prev, zero, pb), kv_block(j, COL_VSW, h)))
        y_ref[j * W:(j + 1) * W, c0:c0 + LANES] = out.astype(BF16)

    logits, probs = {}, {}
    n = len(units)

    def step(i):
        def run():
            if i < n:
                logits[i] = logits_of(*units[i])
            if 0 <= i - SW_SKEW < n:
                _, h, pair = units[i - SW_SKEW]
                probs[i - SW_SKEW] = probs_of(logits.pop(i - SW_SKEW), h, pair)
            if 0 <= i - 2 * SW_SKEW < n:
                write_out(probs.pop(i - 2 * SW_SKEW), *units[i - 2 * SW_SKEW])
        return run

    return [step(i) for i in range(n + 2 * SW_SKEW)]


def _front_kernel(sinks_ref, x_ref, g_ref, w_ref, wg_ref, convw_ref, bif_ref, gn_ref,
                  ya_ref, yb_ref, gab_ref,
                  hn_ref, hlo_ref, slab_new_ref, slab_ref, gates_new_ref, gates_ref, kv_tail_ref, qk_tail_ref,
                  c_ref, n_ref, m_ref, *, tiles_per_seq):
    tt = TT_FRONT
    s = pl.program_id(0)
    first = (jnp.maximum(s - 1, 0) % tiles_per_seq) == 0

    @pl.when(s == 0)
    def _():
        slab_ref[...] = jnp.zeros_like(slab_ref)
        gates_ref[...] = jnp.zeros_like(gates_ref)
        kv_tail_ref[...] = jnp.zeros_like(kv_tail_ref)
        qk_tail_ref[...] = jnp.zeros_like(qk_tail_ref)

    @pl.when(first)
    def _():
        c_ref[...] = jnp.zeros_like(c_ref)
        n_ref[...] = jnp.zeros_like(n_ref)
        m_ref[...] = jnp.zeros_like(m_ref)

    norm, pieces = _projection_steps(x_ref, g_ref, w_ref, wg_ref, hn_ref, hlo_ref,
                                     slab_new_ref, gates_new_ref, gab_ref)
    ml = _mlstm_steps(slab_ref, gates_ref, qk_tail_ref, first, convw_ref, bif_ref, gn_ref, ya_ref,
                      c_ref, n_ref, m_ref, tt // ML_CHUNK)
    sw = _swa_steps(slab_ref, kv_tail_ref, first, sinks_ref, yb_ref, tt // SW_WINDOW)
    norm()
    for thunk in _interleave(pieces, ml, sw):
        thunk()

    kv_tail_ref[...] = slab_ref[tt - SW_WINDOW:tt, COL_KSW:N_SLAB]
    qk_tail_ref[...] = slab_ref[tt - BF16_SUBLANES:tt, COL_QK:COL_QK + 2 * ML_QK_W]
    slab_ref[...] = slab_new_ref[...]
    gates_ref[...] = gates_new_ref[...]


def _front(x2d, sinks, g, w_main, wg, conv_w, b_if, gn, seq):
    t = x2d.shape[0]
    tt = TT_FRONT
    nt = t // tt
    W = SW_WINDOW

    def const(shape):
        return pl.BlockSpec(shape, lambda s: (0, 0), pipeline_mode=pl.Buffered(1))

    proj_tile = lambda s: (jnp.minimum(s, nt - 1), 0)
    mix_tile = lambda s: (jnp.maximum(s - 1, 0), 0)
    return pl.pallas_call(
        functools.partial(_front_kernel, tiles_per_seq=seq // tt),
        out_shape=(jax.ShapeDtypeStruct((t, ML_V_W), BF16),
                   jax.ShapeDtypeStruct((t, SW_Q_W), BF16),
                   jax.ShapeDtypeStruct((t, N_GATES), BF16)),
        grid=(nt + 1,),
        in_specs=[
            pl.BlockSpec(memory_space=pltpu.SMEM),
            pl.BlockSpec((tt, D_MODEL), proj_tile),
            const((1, D_MODEL)),
            const((D_MODEL, N_W)),
            const((D_MODEL, 2 * LANES)),
            const((ML_CONV, 2 * ML_QK_W)),
            const((1, LANES)),
            const((1, ML_V_W)),
        ],
        out_specs=(pl.BlockSpec((tt, ML_V_W), mix_tile),
                   pl.BlockSpec((tt, SW_Q_W), mix_tile),
                   pl.BlockSpec((tt, N_GATES), proj_tile)),
        scratch_shapes=[
            pltpu.VMEM((tt, D_MODEL), BF16),
            pltpu.VMEM((tt, D_MODEL), BF16),
            pltpu.VMEM((tt, N_SLAB), BF16),
            pltpu.VMEM((tt, N_SLAB), BF16),
            pltpu.VMEM((tt, LANES), F32),
            pltpu.VMEM((tt, LANES), F32),
            pltpu.VMEM((W, N_SLAB - COL_KSW), BF16),
            pltpu.VMEM((BF16_SUBLANES, 2 * ML_QK_W), BF16),
            pltpu.VMEM((ML_HEADS, ML_DQK, ML_DV), F32),
            pltpu.VMEM((ML_HEADS, 1, ML_DQK), F32),
            pltpu.VMEM((ML_HEADS, 1, 1), F32),
        ],
        compiler_params=pltpu.CompilerParams(
            dimension_semantics=("arbitrary",),
            vmem_limit_bytes=56 * 1024 * 1024),
        name="front",
    )(sinks, x2d, g, w_main, wg, conv_w, b_if, gn)


def _merge_kernel(x_ref, ya_ref, yb_ref, ga_ref, gb_ref, wa_ref, wb_ref, wo_ref, out_ref):
    a = jax.nn.sigmoid(ga_ref[...].astype(F32)) * _dot(ya_ref[...], wa_ref[...])
    b = jax.nn.sigmoid(gb_ref[...].astype(F32)) * _dot(yb_ref[...], wb_ref[...])
    out_ref[...] = x_ref[...] + _dot((a + b).astype(BF16), wo_ref[...])


def _merge(x2d, ya, yb, gab, wa, wb, wo):
    t = x2d.shape[0]
    tm = TM_MERGE
    wspec = pl.BlockSpec((D_MODEL, D_MODEL), lambda i: (0, 0), pipeline_mode=pl.Buffered(1))
    return pl.pallas_call(
        _merge_kernel,
        out_shape=jax.ShapeDtypeStruct((t, D_MODEL), F32),
        grid=(t // tm,),
        in_specs=[
            pl.BlockSpec((tm, D_MODEL), lambda i: (i, 0)),
            pl.BlockSpec((tm, D_MODEL), lambda i: (i, 0)),
            pl.BlockSpec((tm, D_MODEL), lambda i: (i, 0)),
            pl.BlockSpec((tm, D_MODEL), lambda i: (i, 0)),
            pl.BlockSpec((tm, D_MODEL), lambda i: (i, 1)),
            wspec, wspec, wspec,
        ],
        out_specs=pl.BlockSpec((tm, D_MODEL), lambda i: (i, 0)),
        compiler_params=pltpu.CompilerParams(
            dimension_semantics=("parallel",),
            vmem_limit_bytes=48 * 1024 * 1024),
        name="merge_out",
    )(x2d, ya, yb, gab, gab, wa, wb, wo)


def _mlp_ple_kernel(x_ref, p_ref, gm_ref, wup_ref, wdn_ref, gp_ref, wg_ref, wp_ref, gf_ref, out_ref,
                    *, final_norm):
    x = x_ref[...]
    hn = _rms(x, gm_ref[...]).astype(BF16)
    acc = x
    for c in range(D_FF // FF_CHUNK):
        u = _dot(hn, wup_ref[:, c * FF_CHUNK:(c + 1) * FF_CHUNK])
        r = jnp.maximum(u, 0.0)
        acc = acc + _dot((r * r).astype(BF16), wdn_ref[c * FF_CHUNK:(c + 1) * FF_CHUNK, :])
    x = acc
    gate = jax.nn.sigmoid(_dot(_rms(x, gp_ref[...]).astype(BF16), wg_ref[...]))
    x = x + gate * _dot(p_ref[...].astype(BF16), wp_ref[...])
    if final_norm:
        x = _rms(x, gf_ref[...])
    out_ref[...] = x


def _mlp_ple(x2d, p2d, gm, wup, wdn, gp, wg, wp, gf, final_norm):
    t = x2d.shape[0]
    tm = TM_MLP

    def const(shape):
        return pl.BlockSpec(shape, lambda i: (0, 0), pipeline_mode=pl.Buffered(1))

    return pl.pallas_call(
        functools.partial(_mlp_ple_kernel, final_norm=final_norm),
        out_shape=jax.ShapeDtypeStruct((t, D_MODEL), F32),
        grid=(t // tm,),
        in_specs=[
            pl.BlockSpec((tm, D_MODEL), lambda i: (i, 0)),
            pl.BlockSpec((tm, PLE_DIM), lambda i: (i, 0)),
            const((1, D_MODEL)),
            const((D_MODEL, D_FF)),
            const((D_FF, D_MODEL)),
            const((1, D_MODEL)),
            const((D_MODEL, D_MODEL)),
            const((PLE_DIM, D_MODEL)),
            const((1, D_MODEL)),
        ],
        out_specs=pl.BlockSpec((tm, D_MODEL), lambda i: (i, 0)),
        compiler_params=pltpu.CompilerParams(
            dimension_semantics=("parallel",),
            vmem_limit_bytes=56 * 1024 * 1024),
        name="mlp_ple",
    )(x2d, p2d, gm, wup, wdn, gp, wg, wp, gf)


def _prep_w_in(w_in):
    o = 0
    parts = {}
    for name, size in (("qk", 2 * ML_QK_W), ("v", ML_V_W), ("o", ML_V_W), ("if", 2 * ML_HEADS),
                       ("qsw", SW_Q_W), ("ksw", SW_KV_W), ("vsw", SW_KV_W), ("ga", D_MODEL), ("gb", D_MODEL)):
        parts[name] = w_in[:, o:o + size]
        o += size

    w_main = jnp.concatenate([parts["qk"], parts["v"], parts["o"], parts["qsw"], parts["ksw"],
                              parts["vsw"], parts["ga"], parts["gb"]], axis=1).astype(BF16)
    wg = jnp.pad(parts["if"], ((0, 0), (0, LANES - 2 * ML_HEADS)))
    wg_hi = wg.astype(BF16)
    wg_lo = (wg - wg_hi.astype(F32)).astype(BF16)
    return w_main, jnp.concatenate([wg_hi, wg_lo], axis=1)


def kernel(x, p, norm_mix_g, w_in, conv_qk, b_if, mlstm_norm_g, sinks, w_branch_a, w_branch_b, w_out,
           norm_mlp_g, w_up, w_down, norm_ple_g, w_ple_gate, w_ple_proj, final_norm_g):
    batch, seq, _ = x.shape
    depth = w_in.shape[0]
    t = batch * seq
    x2d = x.reshape(t, D_MODEL)
    row = lambda v: v.reshape(1, -1)
    for i in range(depth):
        w_main, wg = _prep_w_in(w_in[i])
        bif = jnp.pad(b_if[i], (0, LANES - 2 * ML_HEADS)).reshape(1, LANES)
        ya, yb, gab = _front(x2d, sinks[i], row(norm_mix_g[i]), w_main, wg, conv_qk[i], bif,
                             row(mlstm_norm_g[i]), seq)
        x2d = _merge(x2d, ya, yb, gab, w_branch_a[i].astype(BF16), w_branch_b[i].astype(BF16),
                     w_out[i].astype(BF16))
        x2d = _mlp_ple(x2d, p[i].reshape(t, PLE_DIM), row(norm_mlp_g[i]), w_up[i].astype(BF16),
                       w_down[i].astype(BF16), row(norm_ple_g[i]), w_ple_gate[i].astype(BF16),
                       w_ple_proj[i].astype(BF16), row(final_norm_g), final_norm=(i == depth - 1))
    return x2d.reshape(batch, seq, D_MODEL)
```

```python
import functools

import jax
import jax.numpy as jnp
from jax import lax
from jax.experimental import pallas as pl
from jax.experimental.pallas import tpu as pltpu

D_MODEL = 1024
PLE_DIM = 256
ML_HEADS = 4
ML_DQK = 128
ML_DV = 256
ML_CONV = 4
SW_Q_HEADS = 16
SW_KV_HEADS = 4
SW_HEAD_DIM = 64
SW_WINDOW = 128
SW_GROUP = SW_Q_HEADS // SW_KV_HEADS
D_FF = 4 * D_MODEL
EPS = 1e-6

ML_QK_W = ML_HEADS * ML_DQK
ML_V_W = ML_HEADS * ML_DV
SW_Q_W = SW_Q_HEADS * SW_HEAD_DIM
SW_KV_W = SW_KV_HEADS * SW_HEAD_DIM

LANES = 128
BF16_SUBLANES = 16
MXU_N = 256

COL_QK = 0
COL_V = COL_QK + 2 * ML_QK_W
COL_O = COL_V + ML_V_W
COL_QSW = COL_O + ML_V_W
COL_KSW = COL_QSW + SW_Q_W
COL_VSW = COL_KSW + 2 * SW_KV_W
N_SLAB = COL_VSW + 2 * SW_KV_W
COL_GA = N_SLAB
COL_GB = COL_GA + D_MODEL
N_PROJ = COL_GB + D_MODEL
N_GATES = 2 * D_MODEL
WCOL_KSW = COL_KSW
WCOL_VSW = WCOL_KSW + SW_KV_W
WCOL_GA = WCOL_VSW + SW_KV_W
N_W = WCOL_GA + N_GATES

ML_CHUNK = 128
ML_SKEW = 1
SW_SKEW = 2
TT_FRONT = 512
PROJ_PIECE = 256
TM_MERGE = 1024
TM_MLP = 1024
ROW_GROUP = 512
MLP_STAGE_LAG = 2
FF_CHUNK = 1024

F32 = jnp.float32
BF16 = jnp.bfloat16


def _dot(a, b):
    return jnp.dot(a, b, preferred_element_type=F32)


def _dot_nt(a, b):
    return lax.dot_general(a, b, (((1,), (1,)), ((), ())), preferred_element_type=F32)


def _rms(x, g):
    return x * lax.rsqrt(jnp.mean(x * x, axis=-1, keepdims=True) + EPS) * g


def _split_bf16(x, parts):
    out = []
    r = x
    for _ in range(parts - 1):
        t = r.astype(BF16)
        out.append(t)
        r = r - t.astype(F32)
    out.append(r.astype(BF16))
    return out


def _interleave(*lists):
    tagged = []
    for k, lst in enumerate(lists):
        tagged += [((i + 0.5) / len(lst), k, i, f) for i, f in enumerate(lst)]
    return [f for _, _, _, f in sorted(tagged, key=lambda e: e[:3])]


def _projection_steps(g_ref, w_ref, wg_ref, hn_ref, hlo_ref, slab_ref, gates_ref, gab_ref):
    lo_half = lax.broadcasted_iota(jnp.int32, (TT_FRONT, LANES), 1) < SW_HEAD_DIM

    def norm(src_ref, hi_ref, lo_ref, r0, nrows):
        def run():
            h = _rms(src_ref[r0:r0 + nrows, :], g_ref[...])
            h_hi, h_lo = _split_bf16(h, 2)
            hi_ref[r0:r0 + nrows, :] = h_hi
            lo_ref[r0:r0 + nrows, :] = h_lo
        return run

    def gates():
        hi = _dot(hn_ref[...], wg_ref[...])
        gates_ref[...] = hi[:, :LANES] + hi[:, LANES:] + _dot(hlo_ref[...], wg_ref[:, :LANES])

    def dup_heads(val):
        out = []
        for j in range(val.shape[1] // LANES):
            vj = val[:, j * LANES:(j + 1) * LANES]
            rj = pltpu.roll(vj, SW_HEAD_DIM, axis=1)
            out += [jnp.where(lo_half, vj, rj), jnp.where(lo_half, rj, vj)]
        return jnp.concatenate(out, axis=1)

    def piece(c0):
        def run():
            val = _dot(hn_ref[...], w_ref[:, c0:c0 + PROJ_PIECE])
            if c0 == WCOL_KSW:
                slab_ref[:, COL_KSW:COL_VSW] = dup_heads(val).astype(BF16)
            elif c0 == WCOL_VSW:
                slab_ref[:, COL_VSW:N_SLAB] = dup_heads(val).astype(BF16)
            elif c0 < WCOL_KSW:
                slab_ref[:, c0:c0 + PROJ_PIECE] = val.astype(BF16)
            else:
                gab_ref[:, c0 - WCOL_GA:c0 - WCOL_GA + PROJ_PIECE] = val.astype(BF16)
        return run

    return norm, [gates] + [piece(c0) for c0 in range(0, N_W, PROJ_PIECE)]


def _mlstm_steps(slab_ref, gates_ref, qk_tail_ref, first, convw_ref, bif_ref, gn_ref, y_ref,
                 c_ref, n_ref, m_ref, n_chunks):
    L = ML_CHUNK
    H = BF16_SUBLANES
    taps = ML_CONV - 1
    row = lax.broadcasted_iota(jnp.int32, (L, L), 0)
    col = lax.broadcasted_iota(jnp.int32, (L, L), 1)
    causal = row >= col
    tril = jnp.where(causal, 1.0, 0.0).astype(BF16)
    cw = convw_ref[...]
    bif = bif_ref[...]
    row8 = lax.broadcasted_iota(jnp.int32, (8, 2 * ML_QK_W), 0)

    def chunk_prep(c):
        r0 = c * L
        x_cur = slab_ref[r0:r0 + L, COL_QK:COL_QK + 2 * ML_QK_W].astype(F32)
        if c == 0:
            halo = jnp.where(first, 0.0, qk_tail_ref[...].astype(F32))[H - 8:, :]
        else:
            halo = slab_ref[r0 - H:r0, COL_QK:COL_QK + 2 * ML_QK_W].astype(F32)[H - 8:, :]
        acc = cw[taps:taps + 1, :] * x_cur
        for k in range(1, taps + 1):
            rolled = pltpu.roll(x_cur, k, axis=0)
            top = jnp.where(row8 < k, pltpu.roll(halo, k, axis=0), rolled[:8, :])
            acc = acc + cw[taps - k:taps - k + 1, :] * jnp.concatenate([top, rolled[8:, :]], axis=0)
        act = acc * jax.nn.sigmoid(acc)
        gates = gates_ref[r0:r0 + L, :] + bif
        logf = jnp.minimum(gates, 0.0) - jnp.log1p(jnp.exp(-jnp.abs(gates)))
        bsplit = _dot(tril, jnp.concatenate(_split_bf16(logf, 3), axis=1))
        bcum = bsplit[:, :LANES] + bsplit[:, LANES:2 * LANES] + bsplit[:, 2 * LANES:]
        return dict(q=act[:, :ML_QK_W] * (ML_DQK ** -0.5), k=act[:, ML_QK_W:], gates=gates, bcum=bcum,
                    gates_t=gates.T, bcum_t=bcum.T)

    def qk_part(c, h, pre):
        qf = pre["q"][:, h * ML_DQK:(h + 1) * ML_DQK]
        kf = pre["k"][:, h * ML_DQK:(h + 1) * ML_DQK]
        q = qf.astype(BF16)
        kt = kf.T.astype(BF16)
        return dict(q=q, qf=qf, kf=kf, kt=kt, qk=_dot(q, kt))

    def local_part(c, h, pre, qk):
        r0 = c * L
        fl = ML_HEADS + h
        b_c = pre["bcum"][:, fl:fl + 1]
        i_c = pre["gates"][:, h:h + 1]
        r_r = pre["gates_t"][h:h + 1, :] - pre["bcum_t"][fl:fl + 1, :]
        v = slab_ref[r0:r0 + L, COL_V + h * ML_DV:COL_V + (h + 1) * ML_DV]

        log_d = jnp.where(causal, b_c + r_r, -jnp.inf)
        a = jnp.max(log_d, axis=1, keepdims=True)
        s_loc = qk["qk"] * jnp.exp(log_d - a)
        b_last = pre["bcum"][L - 1:L, fl:fl + 1]
        log_w = b_last - b_c + i_c
        aw = jnp.max(log_w, axis=0, keepdims=True)
        w_loc = jnp.exp(log_w - aw)
        return dict(q=qk["q"], qf=qk["qf"], b_c=b_c, a=a, b_last=b_last, aw=aw,
                    sv=_dot(s_loc.astype(BF16), v), rs=jnp.sum(s_loc, axis=1, keepdims=True),
                    u=_dot(qk["kt"], (w_loc * v.astype(F32)).astype(BF16)),
                    nu=jnp.sum(w_loc * qk["kf"], axis=0, keepdims=True))

    def carried_part(c, h, loc, state):
        r0 = c * L
        ct, n_prev, m_prev = state
        inter = loc["b_c"] + m_prev
        m_t = jnp.maximum(inter, loc["a"])
        w_inter = jnp.exp(inter - m_t)
        f_loc = jnp.exp(loc["a"] - m_t)
        num = w_inter * _dot(loc["q"], ct.astype(BF16)) + f_loc * loc["sv"]
        den = w_inter * jnp.sum(loc["qf"] * n_prev, axis=1, keepdims=True) + f_loc * loc["rs"]
        rden = 1.0 / jnp.maximum(jnp.abs(den), jnp.exp(-m_t))
        ms = jnp.mean(num * num, axis=1, keepdims=True)
        scale = rden * lax.rsqrt(rden * rden * ms + EPS)
        og = jax.nn.sigmoid(slab_ref[r0:r0 + L, COL_O + h * ML_DV:COL_O + (h + 1) * ML_DV].astype(F32))
        y_ref[r0:r0 + L, h * ML_DV:(h + 1) * ML_DV] = (
            og * (num * scale * gn_ref[:, h * ML_DV:(h + 1) * ML_DV])).astype(BF16)
        m_new = jnp.maximum(loc["b_last"] + m_prev, loc["aw"])
        decay = jnp.exp(loc["b_last"] + m_prev - m_new)
        g_loc = jnp.exp(loc["aw"] - m_new)
        return decay * ct + g_loc * loc["u"], decay * n_prev + g_loc * loc["nu"], m_new

    units = [(c, h) for c in range(n_chunks) for h in range(ML_HEADS)]
    n_units = len(units)
    state, pre, qk, loc = {}, {}, {}, {}
    n_ticks = n_units + 2 * ML_SKEW

    def step(i):
        def run():
            if i == 0:
                for h in range(ML_HEADS):
                    state[h] = (c_ref[h], n_ref[h], m_ref[h])
                pre[0] = chunk_prep(0)
            if i + 1 < n_units and units[i + 1][1] == 0:
                pre[units[i + 1][0]] = chunk_prep(units[i + 1][0])
            if i < n_units:
                c, h = units[i]
                qk[i] = qk_part(c, h, pre[c])
            j = i - ML_SKEW
            if 0 <= j < n_units:
                c, h = units[j]
                loc[j] = local_part(c, h, pre[c], qk.pop(j))
            j = i - 2 * ML_SKEW
            if 0 <= j < n_units:
                c, h = units[j]
                state[h] = carried_part(c, h, loc.pop(j), state[h])
            if i == n_ticks - 1:
                for h in range(ML_HEADS):
                    c_ref[h], n_ref[h], m_ref[h] = state[h]
        return run

    return [step(i) for i in range(n_ticks)]


def _swa_steps(slab_ref, kv_tail_ref, first, sinks_ref, y_ref, n_blocks):
    W = SW_WINDOW
    hd = SW_HEAD_DIM
    qi = lax.broadcasted_iota(jnp.int32, (W, 2 * W), 0)
    key = lax.broadcasted_iota(jnp.int32, (W, 2 * W), 1) % W
    from_prev = key > qi
    lo_half = lax.broadcasted_iota(jnp.int32, (W, LANES), 1) < hd
    zero = jnp.zeros((), BF16)
    prev_bias = jnp.where(first, -jnp.inf, 0.0).astype(F32)

    def block_diag(d):
        return jnp.concatenate([jnp.where(lo_half, d, zero), jnp.where(lo_half, zero, d)], axis=0)

    cache = {}

    def kv_block(j, col0, h):
        if (j, col0, h) not in cache:
            if j < 0:
                d = kv_tail_ref[:, col0 - COL_KSW + h * LANES:col0 - COL_KSW + (h + 1) * LANES]
            else:
                d = slab_ref[j * W:(j + 1) * W, col0 + h * LANES:col0 + (h + 1) * LANES]
            cache[j, col0, h] = block_diag(d)
        return cache[j, col0, h]

    units = [(j, h, pair) for j in range(n_blocks) for h in range(SW_KV_HEADS)
             for pair in range(SW_GROUP // 2)]

    def logits_of(j, h, pair):
        c0 = COL_QSW + (h * SW_GROUP + 2 * pair) * hd
        qp = slab_ref[j * W:(j + 1) * W, c0:c0 + LANES] * jnp.asarray(hd ** -0.5, BF16)
        s_prev = _dot_nt(qp, kv_block(j - 1, COL_KSW, h))
        if j == 0:
            s_prev = s_prev + prev_bias
        return jnp.where(from_prev, s_prev, _dot_nt(qp, kv_block(j, COL_KSW, h)))

    def probs_of(lg, h, pair):
        halves = []
        for e in range(2):
            sink = sinks_ref[h * SW_GROUP + 2 * pair + e]
            le = lg[:, e * W:(e + 1) * W]
            m = jnp.maximum(jnp.max(le, axis=1, keepdims=True), sink)
            p = jnp.exp(le - m)
            denom = jnp.sum(p, axis=1, keepdims=True) + jnp.exp(sink - m)
            halves.append((p * (1.0 / denom)).astype(BF16))
        return jnp.concatenate(halves, axis=1)

    def write_out(pb, j, h, pair):
        c0 = (h * SW_GROUP + 2 * pair) * hd
        out = (_dot(jnp.where(from_prev, pb, zero), kv_block(j - 1, COL_VSW, h))
               + _dot(jnp.where(from_prev, zero, pb), kv_block(j, COL_VSW, h)))
        y_ref[j * W:(j + 1) * W, c0:c0 + LANES] = out.astype(BF16)

    logits, probs = {}, {}
    n = len(units)

    def step(i):
        def run():
            if i < n:
                logits[i] = logits_of(*units[i])
            if 0 <= i - SW_SKEW < n:
                _, h, pair = units[i - SW_SKEW]
                probs[i - SW_SKEW] = probs_of(logits.pop(i - SW_SKEW), h, pair)
            if 0 <= i - 2 * SW_SKEW < n:
                write_out(probs.pop(i - 2 * SW_SKEW), *units[i - 2 * SW_SKEW])
        return run

    return [step(i) for i in range(n + 2 * SW_SKEW)]


def _front_kernel(sinks_ref, x_ref, g_ref, w_ref, wg_ref, convw_ref, bif_ref, gn_ref,
                  ya_ref, yb_ref, gab_ref,
                  hn_ref, hlo_ref, slab_new_ref, slab_ref, gates_new_ref, gates_ref, kv_tail_ref, qk_tail_ref,
                  c_ref, n_ref, m_ref, *, tiles_per_seq):
    tt = TT_FRONT
    s = pl.program_id(0)
    first = (jnp.maximum(s - 1, 0) % tiles_per_seq) == 0

    @pl.when(s == 0)
    def _():
        slab_ref[...] = jnp.zeros_like(slab_ref)
        gates_ref[...] = jnp.zeros_like(gates_ref)
        kv_tail_ref[...] = jnp.zeros_like(kv_tail_ref)
        qk_tail_ref[...] = jnp.zeros_like(qk_tail_ref)

    @pl.when(first)
    def _():
        c_ref[...] = jnp.zeros_like(c_ref)
        n_ref[...] = jnp.zeros_like(n_ref)
        m_ref[...] = jnp.zeros_like(m_ref)

    norm, pieces = _projection_steps(g_ref, w_ref, wg_ref, hn_ref, hlo_ref, slab_new_ref, gates_new_ref, gab_ref)
    ml = _mlstm_steps(slab_ref, gates_ref, qk_tail_ref, first, convw_ref, bif_ref, gn_ref, ya_ref,
                      c_ref, n_ref, m_ref, tt // ML_CHUNK)
    sw = _swa_steps(slab_ref, kv_tail_ref, first, sinks_ref, yb_ref, tt // SW_WINDOW)
    norm(x_ref, hn_ref, hlo_ref, 0, tt)()
    for thunk in _interleave(pieces, ml, sw):
        thunk()

    kv_tail_ref[...] = slab_ref[tt - SW_WINDOW:tt, COL_KSW:N_SLAB]
    qk_tail_ref[...] = slab_ref[tt - BF16_SUBLANES:tt, COL_QK:COL_QK + 2 * ML_QK_W]
    slab_ref[...] = slab_new_ref[...]
    gates_ref[...] = gates_new_ref[...]


def _front(x2d, sinks, g, w_main, wg, conv_w, b_if, gn, seq):
    t = x2d.shape[0]
    tt = TT_FRONT
    nt = t // tt
    W = SW_WINDOW

    def const(shape):
        return pl.BlockSpec(shape, lambda s: (0, 0), pipeline_mode=pl.Buffered(1))

    proj_tile = lambda s: (jnp.minimum(s, nt - 1), 0)
    mix_tile = lambda s: (jnp.maximum(s - 1, 0), 0)
    return pl.pallas_call(
        functools.partial(_front_kernel, tiles_per_seq=seq // tt),
        out_shape=(jax.ShapeDtypeStruct((t, ML_V_W), BF16),
                   jax.ShapeDtypeStruct((t, SW_Q_W), BF16),
                   jax.ShapeDtypeStruct((t, N_GATES), BF16)),
        grid=(nt + 1,),
        in_specs=[
            pl.BlockSpec(memory_space=pltpu.SMEM),
            pl.BlockSpec((tt, D_MODEL), proj_tile),
            const((1, D_MODEL)),
            const((D_MODEL, N_W)),
            const((D_MODEL, 2 * LANES)),
            const((ML_CONV, 2 * ML_QK_W)),
            const((1, LANES)),
            const((1, ML_V_W)),
        ],
        out_specs=(pl.BlockSpec((tt, ML_V_W), mix_tile),
                   pl.BlockSpec((tt, SW_Q_W), mix_tile),
                   pl.BlockSpec((tt, N_GATES), proj_tile)),
        scratch_shapes=[
            pltpu.VMEM((tt, D_MODEL), BF16),
            pltpu.VMEM((tt, D_MODEL), BF16),
            pltpu.VMEM((tt, N_SLAB), BF16),
            pltpu.VMEM((tt, N_SLAB), BF16),
            pltpu.VMEM((tt, LANES), F32),
            pltpu.VMEM((tt, LANES), F32),
            pltpu.VMEM((W, N_SLAB - COL_KSW), BF16),
            pltpu.VMEM((BF16_SUBLANES, 2 * ML_QK_W), BF16),
            pltpu.VMEM((ML_HEADS, ML_DQK, ML_DV), F32),
            pltpu.VMEM((ML_HEADS, 1, ML_DQK), F32),
            pltpu.VMEM((ML_HEADS, 1, 1), F32),
        ],
        compiler_params=pltpu.CompilerParams(
            dimension_semantics=("arbitrary",),
            vmem_limit_bytes=56 * 1024 * 1024),
        name="front",
    )(sinks, x2d, g, w_main, wg, conv_w, b_if, gn)


def _merge_kernel(x_ref, ya_ref, yb_ref, ga_ref, gb_ref, wa_ref, wb_ref, wo_ref, out_ref):
    groups = [slice(r, r + ROW_GROUP) for r in range(0, TM_MERGE, ROW_GROUP)]
    merged = {}

    def branches(rows):
        a = jax.nn.sigmoid(ga_ref[rows, :].astype(F32)) * _dot(ya_ref[rows, :], wa_ref[...])
        b = jax.nn.sigmoid(gb_ref[rows, :].astype(F32)) * _dot(yb_ref[rows, :], wb_ref[...])
        merged[rows.start] = (a + b).astype(BF16)

    def project(rows):
        out_ref[rows, :] = x_ref[rows, :] + _dot(merged.pop(rows.start), wo_ref[...])

    for i in range(len(groups) + 1):
        if i < len(groups):
            branches(groups[i])
        if i >= 1:
            project(groups[i - 1])


def _merge(x2d, ya, yb, gab, wa, wb, wo):
    t = x2d.shape[0]
    tm = TM_MERGE
    wspec = pl.BlockSpec((D_MODEL, D_MODEL), lambda i: (0, 0), pipeline_mode=pl.Buffered(1))
    return pl.pallas_call(
        _merge_kernel,
        out_shape=jax.ShapeDtypeStruct((t, D_MODEL), F32),
        grid=(t // tm,),
        in_specs=[
            pl.BlockSpec((tm, D_MODEL), lambda i: (i, 0)),
            pl.BlockSpec((tm, D_MODEL), lambda i: (i, 0)),
            pl.BlockSpec((tm, D_MODEL), lambda i: (i, 0)),
            pl.BlockSpec((tm, D_MODEL), lambda i: (i, 0)),
            pl.BlockSpec((tm, D_MODEL), lambda i: (i, 1)),
            wspec, wspec, wspec,
        ],
        out_specs=pl.BlockSpec((tm, D_MODEL), lambda i: (i, 0)),
        compiler_params=pltpu.CompilerParams(
            dimension_semantics=("parallel",),
            vmem_limit_bytes=48 * 1024 * 1024),
        name="merge_out",
    )(x2d, ya, yb, gab, gab, wa, wb, wo)


def _mlp_ple_kernel(x_ref, p_ref, gm_ref, wup_ref, wdn_ref, gp_ref, wg_ref, wp_ref, gf_ref, out_ref,
                    *, final_norm):
    groups = [slice(r, r + ROW_GROUP) for r in range(0, TM_MLP, ROW_GROUP)]
    n_ff = D_FF // FF_CHUNK
    hn, acc = {}, {}

    def stage(g, k):
        rows = groups[g]
        if k == 0:
            x = x_ref[rows, :]
            hn[g] = _rms(x, gm_ref[...]).astype(BF16)
            acc[g] = x
        elif k <= n_ff:
            c = k - 1
            u = _dot(hn[g], wup_ref[:, c * FF_CHUNK:(c + 1) * FF_CHUNK])
            r = jnp.maximum(u, 0.0)
            acc[g] = acc[g] + _dot((r * r).astype(BF16), wdn_ref[c * FF_CHUNK:(c + 1) * FF_CHUNK, :])
        else:
            x = acc.pop(g)
            gate = jax.nn.sigmoid(_dot(_rms(x, gp_ref[...]).astype(BF16), wg_ref[...]))
            x = x + gate * _dot(p_ref[rows, :].astype(BF16), wp_ref[...])
            if final_norm:
                x = _rms(x, gf_ref[...])
            out_ref[rows, :] = x

    n_stages = n_ff + 2
    for tick in range(n_stages + MLP_STAGE_LAG * (len(groups) - 1)):
        for g in reversed(range(len(groups))):
            k = tick - MLP_STAGE_LAG * g
            if 0 <= k < n_stages:
                stage(g, k)


def _mlp_ple(x2d, p2d, gm, wup, wdn, gp, wg, wp, gf, final_norm):
    t = x2d.shape[0]
    tm = TM_MLP

    def const(shape):
        return pl.BlockSpec(shape, lambda i: (0, 0), pipeline_mode=pl.Buffered(1))

    return pl.pallas_call(
        functools.partial(_mlp_ple_kernel, final_norm=final_norm),
        out_shape=jax.ShapeDtypeStruct((t, D_MODEL), F32),
        grid=(t // tm,),
        in_specs=[
            pl.BlockSpec((tm, D_MODEL), lambda i: (i, 0)),
            pl.BlockSpec((tm, PLE_DIM), lambda i: (i, 0)),
            const((1, D_MODEL)),
            const((D_MODEL, D_FF)),
            const((D_FF, D_MODEL)),
            const((1, D_MODEL)),
            const((D_MODEL, D_MODEL)),
            const((PLE_DIM, D_MODEL)),
            const((1, D_MODEL)),
        ],
        out_specs=pl.BlockSpec((tm, D_MODEL), lambda i: (i, 0)),
        compiler_params=pltpu.CompilerParams(
            dimension_semantics=("parallel",),
            vmem_limit_bytes=56 * 1024 * 1024),
        name="mlp_ple",
    )(x2d, p2d, gm, wup, wdn, gp, wg, wp, gf)


def _prep_w_in(w_in):
    o = 0
    parts = {}
    for name, size in (("qk", 2 * ML_QK_W), ("v", ML_V_W), ("o", ML_V_W), ("if", 2 * ML_HEADS),
                       ("qsw", SW_Q_W), ("ksw", SW_KV_W), ("vsw", SW_KV_W), ("ga", D_MODEL), ("gb", D_MODEL)):
        parts[name] = w_in[:, o:o + size]
        o += size

    w_main = jnp.concatenate([parts["qk"], parts["v"], parts["o"], parts["qsw"], parts["ksw"],
                              parts["vsw"], parts["ga"], parts["gb"]], axis=1).astype(BF16)
    wg = jnp.pad(parts["if"], ((0, 0), (0, LANES - 2 * ML_HEADS)))
    wg_hi = wg.astype(BF16)
    wg_lo = (wg - wg_hi.astype(F32)).astype(BF16)
    return w_main, jnp.concatenate([wg_hi, wg_lo], axis=1)


def kernel(x, p, norm_mix_g, w_in, conv_qk, b_if, mlstm_norm_g, sinks, w_branch_a, w_branch_b, w_out,
           norm_mlp_g, w_up, w_down, norm_ple_g, w_ple_gate, w_ple_proj, final_norm_g):
    batch, seq, _ = x.shape
    depth = w_in.shape[0]
    t = batch * seq
    x2d = x.reshape(t, D_MODEL)
    row = lambda v: v.reshape(1, -1)
    for i in range(depth):
        w_main, wg = _prep_w_in(w_in[i])
        bif = jnp.pad(b_if[i], (0, LANES - 2 * ML_HEADS)).reshape(1, LANES)
        ya, yb, gab = _front(x2d, sinks[i], row(norm_mix_g[i]), w_main, wg, conv_qk[i], bif,
                             row(mlstm_norm_g[i]), seq)
        x2d = _merge(x2d, ya, yb, gab, w_branch_a[i].astype(BF16), w_branch_b[i].astype(BF16),
                     w_out[i].astype(BF16))
        x2d = _mlp_ple(x2d, p[i].reshape(t, PLE_DIM), row(norm_mlp_g[i]), w_up[i].astype(BF16),
                       w_down[i].astype(BF16), row(norm_ple_g[i]), w_ple_gate[i].astype(BF16),
                       w_ple_proj[i].astype(BF16), row(final_norm_g), final_norm=(i == depth - 1))
    return x2d.reshape(batch, seq, D_MODEL)
```

```python
import functools

import jax
import jax.numpy as jnp
from jax import lax
from jax.experimental import pallas as pl
from jax.experimental.pallas import tpu as pltpu

D_MODEL = 1024
PLE_DIM = 256
ML_HEADS = 4
ML_DQK = 128
ML_DV = 256
ML_CONV = 4
SW_Q_HEADS = 16
SW_KV_HEADS = 4
SW_HEAD_DIM = 64
SW_WINDOW = 128
SW_GROUP = SW_Q_HEADS // SW_KV_HEADS
D_FF = 4 * D_MODEL
EPS = 1e-6

ML_QK_W = ML_HEADS * ML_DQK
ML_V_W = ML_HEADS * ML_DV
SW_Q_W = SW_Q_HEADS * SW_HEAD_DIM
SW_KV_W = SW_KV_HEADS * SW_HEAD_DIM

LANES = 128
BF16_SUBLANES = 16
MXU_N = 256

COL_QK = 0
COL_V = COL_QK + 2 * ML_QK_W
COL_O = COL_V + ML_V_W
COL_QSW = COL_O + ML_V_W
COL_KSW = COL_QSW + SW_Q_W
COL_VSW = COL_KSW + 2 * SW_KV_W
N_SLAB = COL_VSW + 2 * SW_KV_W
COL_GA = N_SLAB
COL_GB = COL_GA + D_MODEL
N_PROJ = COL_GB + D_MODEL
N_GATES = 2 * D_MODEL
WCOL_KSW = COL_KSW
WCOL_VSW = WCOL_KSW + SW_KV_W
WCOL_GA = WCOL_VSW + SW_KV_W
N_W = WCOL_GA + N_GATES

ML_CHUNK = 128
ML_SKEW = 1
SW_SKEW = 2
TT_FRONT = 512
PROJ_PIECE = 256
TM_MERGE = 1024
TM_MLP = 1024
ROW_GROUP = 512
MLP_STAGE_LAG = 2
FF_CHUNK = 1024
PREP_COLS = 256

F32 = jnp.float32
BF16 = jnp.bfloat16


def _dot(a, b):
    return jnp.dot(a, b, preferred_element_type=F32)


def _dot_nt(a, b):
    return lax.dot_general(a, b, (((1,), (1,)), ((), ())), preferred_element_type=F32)


def _rms(x, g):
    return x * lax.rsqrt(jnp.mean(x * x, axis=-1, keepdims=True) + EPS) * g


def _split_bf16(x, parts):
    out = []
    r = x
    for _ in range(parts - 1):
        t = r.astype(BF16)
        out.append(t)
        r = r - t.astype(F32)
    out.append(r.astype(BF16))
    return out


def _interleave(*lists):
    tagged = []
    for k, lst in enumerate(lists):
        tagged += [((i + 0.5) / len(lst), k, i, f) for i, f in enumerate(lst)]
    return [f for _, _, _, f in sorted(tagged, key=lambda e: e[:3])]


def _projection_steps(g_ref, w_ref, wg_ref, hn_ref, hlo_ref, slab_ref, gates_ref, gab_ref):
    lo_half = lax.broadcasted_iota(jnp.int32, (TT_FRONT, LANES), 1) < SW_HEAD_DIM

    def norm(src_ref, hi_ref, lo_ref, r0, nrows):
        def run():
            h = _rms(src_ref[r0:r0 + nrows, :], g_ref[...])
            h_hi, h_lo = _split_bf16(h, 2)
            hi_ref[r0:r0 + nrows, :] = h_hi
            lo_ref[r0:r0 + nrows, :] = h_lo
        return run

    def gates():
        hi = _dot(hn_ref[...], wg_ref[...])
        gates_ref[...] = hi[:, :LANES] + hi[:, LANES:] + _dot(hlo_ref[...], wg_ref[:, :LANES])

    def dup_heads(val):
        out = []
        for j in range(val.shape[1] // LANES):
            vj = val[:, j * LANES:(j + 1) * LANES]
            rj = pltpu.roll(vj, SW_HEAD_DIM, axis=1)
            out += [jnp.where(lo_half, vj, rj), jnp.where(lo_half, rj, vj)]
        return jnp.concatenate(out, axis=1)

    def piece(c0):
        def run():
            val = _dot(hn_ref[...], w_ref[:, c0:c0 + PROJ_PIECE])
            if c0 == WCOL_KSW:
                slab_ref[:, COL_KSW:COL_VSW] = dup_heads(val).astype(BF16)
            elif c0 == WCOL_VSW:
                slab_ref[:, COL_VSW:N_SLAB] = dup_heads(val).astype(BF16)
            elif c0 < WCOL_KSW:
                slab_ref[:, c0:c0 + PROJ_PIECE] = val.astype(BF16)
            else:
                gab_ref[:, c0 - WCOL_GA:c0 - WCOL_GA + PROJ_PIECE] = val.astype(BF16)
        return run

    return norm, [gates] + [piece(c0) for c0 in range(0, N_W, PROJ_PIECE)]


def _mlstm_steps(slab_ref, gates_ref, qk_tail_ref, first, convw_ref, bif_ref, gn_ref, y_ref,
                 c_ref, n_ref, m_ref, n_chunks):
    L = ML_CHUNK
    H = BF16_SUBLANES
    taps = ML_CONV - 1
    row = lax.broadcasted_iota(jnp.int32, (L, L), 0)
    col = lax.broadcasted_iota(jnp.int32, (L, L), 1)
    causal = row >= col
    tril = jnp.where(causal, 1.0, 0.0).astype(BF16)
    cw = convw_ref[...]
    bif = bif_ref[...]
    row8 = lax.broadcasted_iota(jnp.int32, (8, 2 * ML_QK_W), 0)

    def chunk_prep(c):
        r0 = c * L
        x_cur = slab_ref[r0:r0 + L, COL_QK:COL_QK + 2 * ML_QK_W].astype(F32)
        if c == 0:
            halo = jnp.where(first, 0.0, qk_tail_ref[...].astype(F32))[H - 8:, :]
        else:
            halo = slab_ref[r0 - H:r0, COL_QK:COL_QK + 2 * ML_QK_W].astype(F32)[H - 8:, :]
        acc = cw[taps:taps + 1, :] * x_cur
        for k in range(1, taps + 1):
            rolled = pltpu.roll(x_cur, k, axis=0)
            top = jnp.where(row8 < k, pltpu.roll(halo, k, axis=0), rolled[:8, :])
            acc = acc + cw[taps - k:taps - k + 1, :] * jnp.concatenate([top, rolled[8:, :]], axis=0)
        act = acc * jax.nn.sigmoid(acc)
        gates = gates_ref[r0:r0 + L, :] + bif
        logf = jnp.minimum(gates, 0.0) - jnp.log1p(jnp.exp(-jnp.abs(gates)))
        bsplit = _dot(tril, jnp.concatenate(_split_bf16(logf, 3), axis=1))
        bcum = bsplit[:, :LANES] + bsplit[:, LANES:2 * LANES] + bsplit[:, 2 * LANES:]
        return dict(q=act[:, :ML_QK_W] * (ML_DQK ** -0.5), k=act[:, ML_QK_W:], gates=gates, bcum=bcum,
                    gates_t=gates.T, bcum_t=bcum.T)

    def qk_part(c, h, pre):
        qf = pre["q"][:, h * ML_DQK:(h + 1) * ML_DQK]
        kf = pre["k"][:, h * ML_DQK:(h + 1) * ML_DQK]
        q = qf.astype(BF16)
        kt = kf.T.astype(BF16)
        return dict(q=q, qf=qf, kf=kf, kt=kt, qk=_dot(q, kt))

    def local_part(c, h, pre, qk):
        r0 = c * L
        fl = ML_HEADS + h
        b_c = pre["bcum"][:, fl:fl + 1]
        i_c = pre["gates"][:, h:h + 1]
        r_r = pre["gates_t"][h:h + 1, :] - pre["bcum_t"][fl:fl + 1, :]
        v = slab_ref[r0:r0 + L, COL_V + h * ML_DV:COL_V + (h + 1) * ML_DV]

        log_d = jnp.where(causal, b_c + r_r, -jnp.inf)
        a = jnp.max(log_d, axis=1, keepdims=True)
        s_loc = qk["qk"] * jnp.exp(log_d - a)
        b_last = pre["bcum"][L - 1:L, fl:fl + 1]
        log_w = b_last - b_c + i_c
        aw = jnp.max(log_w, axis=0, keepdims=True)
        w_loc = jnp.exp(log_w - aw)
        return dict(q=qk["q"], qf=qk["qf"], b_c=b_c, a=a, b_last=b_last, aw=aw,
                    sv=_dot(s_loc.astype(BF16), v), rs=jnp.sum(s_loc, axis=1, keepdims=True),
                    u=_dot(qk["kt"], (w_loc * v.astype(F32)).astype(BF16)),
                    nu=jnp.sum(w_loc * qk["kf"], axis=0, keepdims=True))

    def carried_part(c, h, loc, state):
        r0 = c * L
        ct, n_prev, m_prev = state
        inter = loc["b_c"] + m_prev
        m_t = jnp.maximum(inter, loc["a"])
        w_inter = jnp.exp(inter - m_t)
        f_loc = jnp.exp(loc["a"] - m_t)
        num = w_inter * _dot(loc["q"], ct.astype(BF16)) + f_loc * loc["sv"]
        den = w_inter * jnp.sum(loc["qf"] * n_prev, axis=1, keepdims=True) + f_loc * loc["rs"]
        rden = 1.0 / jnp.maximum(jnp.abs(den), jnp.exp(-m_t))
        ms = jnp.mean(num * num, axis=1, keepdims=True)
        scale = rden * lax.rsqrt(rden * rden * ms + EPS)
        og = jax.nn.sigmoid(slab_ref[r0:r0 + L, COL_O + h * ML_DV:COL_O + (h + 1) * ML_DV].astype(F32))
        y_ref[r0:r0 + L, h * ML_DV:(h + 1) * ML_DV] = (
            og * (num * scale * gn_ref[:, h * ML_DV:(h + 1) * ML_DV])).astype(BF16)
        m_new = jnp.maximum(loc["b_last"] + m_prev, loc["aw"])
        decay = jnp.exp(loc["b_last"] + m_prev - m_new)
        g_loc = jnp.exp(loc["aw"] - m_new)
        return decay * ct + g_loc * loc["u"], decay * n_prev + g_loc * loc["nu"], m_new

    units = [(c, h) for c in range(n_chunks) for h in range(ML_HEADS)]
    n_units = len(units)
    state, pre, qk, loc = {}, {}, {}, {}
    n_ticks = n_units + 2 * ML_SKEW

    def step(i):
        def run():
            if i == 0:
                for h in range(ML_HEADS):
                    state[h] = (c_ref[h], n_ref[h], m_ref[h])
                pre[0] = chunk_prep(0)
            if i + 1 < n_units and units[i + 1][1] == 0:
                pre[units[i + 1][0]] = chunk_prep(units[i + 1][0])
            if i < n_units:
                c, h = units[i]
                qk[i] = qk_part(c, h, pre[c])
            j = i - ML_SKEW
            if 0 <= j < n_units:
                c, h = units[j]
                loc[j] = local_part(c, h, pre[c], qk.pop(j))
            j = i - 2 * ML_SKEW
            if 0 <= j < n_units:
                c, h = units[j]
                state[h] = carried_part(c, h, loc.pop(j), state[h])
            if i == n_ticks - 1:
                for h in range(ML_HEADS):
                    c_ref[h], n_ref[h], m_ref[h] = state[h]
        return run

    return [step(i) for i in range(n_ticks)]


def _swa_steps(slab_ref, kv_tail_ref, first, sinks_ref, y_ref, n_blocks):
    W = SW_WINDOW
    hd = SW_HEAD_DIM
    qi = lax.broadcasted_iota(jnp.int32, (W, 2 * W), 0)
    key = lax.broadcasted_iota(jnp.int32, (W, 2 * W), 1) % W
    from_prev = key > qi
    lo_half = lax.broadcasted_iota(jnp.int32, (W, LANES), 1) < hd
    zero = jnp.zeros((), BF16)
    prev_bias = jnp.where(first, -jnp.inf, 0.0).astype(F32)

    def block_diag(d):
        return jnp.concatenate([jnp.where(lo_half, d, zero), jnp.where(lo_half, zero, d)], axis=0)

    cache = {}

    def kv_block(j, col0, h):
        if (j, col0, h) not in cache:
            if j < 0:
                d = kv_tail_ref[:, col0 - COL_KSW + h * LANES:col0 - COL_KSW + (h + 1) * LANES]
            else:
                d = slab_ref[j * W:(j + 1) * W, col0 + h * LANES:col0 + (h + 1) * LANES]
            cache[j, col0, h] = block_diag(d)
        return cache[j, col0, h]

    units = [(j, h, pair) for j in range(n_blocks) for h in range(SW_KV_HEADS)
             for pair in range(SW_GROUP // 2)]

    def logits_of(j, h, pair):
        c0 = COL_QSW + (h * SW_GROUP + 2 * pair) * hd
        qp = slab_ref[j * W:(j + 1) * W, c0:c0 + LANES] * jnp.asarray(hd ** -0.5, BF16)
        s_prev = _dot_nt(qp, kv_block(j - 1, COL_KSW, h))
        if j == 0:
            s_prev = s_prev + prev_bias
        return jnp.where(from_prev, s_prev, _dot_nt(qp, kv_block(j, COL_KSW, h)))

    def probs_of(lg, h, pair):
        halves = []
        for e in range(2):
            sink = sinks_ref[h * SW_GROUP + 2 * pair + e]
            le = lg[:, e * W:(e + 1) * W]
            m = jnp.maximum(jnp.max(le, axis=1, keepdims=True), sink)
            p = jnp.exp(le - m)
            denom = jnp.sum(p, axis=1, keepdims=True) + jnp.exp(sink - m)
            halves.append((p * (1.0 / denom)).astype(BF16))
        return jnp.concatenate(halves, axis=1)

    def write_out(pb, j, h, pair):
        c0 = (h * SW_GROUP + 2 * pair) * hd
        out = (_dot(jnp.where(from_prev, pb, zero), kv_block(j - 1, COL_VSW, h))
               + _dot(jnp.where(from_prev, zero, pb), kv_block(j, COL_VSW, h)))
        y_ref[j * W:(j + 1) * W, c0:c0 + LANES] = out.astype(BF16)

    logits, probs = {}, {}
    n = len(units)

    def step(i):
        def run():
            if i < n:
                logits[i] = logits_of(*units[i])
            if 0 <= i - SW_SKEW < n:
                _, h, pair = units[i - SW_SKEW]
                probs[i - SW_SKEW] = probs_of(logits.pop(i - SW_SKEW), h, pair)
            if 0 <= i - 2 * SW_SKEW < n:
                write_out(probs.pop(i - 2 * SW_SKEW), *units[i - 2 * SW_SKEW])
        return run

    return [step(i) for i in range(n + 2 * SW_SKEW)]


def _front_kernel(sinks_ref, x_ref, g_ref, w_ref, wg_ref, convw_ref, bif_ref, gn_ref,
                  ya_ref, yb_ref, gab_ref,
                  hn_ref, hlo_ref, slab_new_ref, slab_ref, gates_new_ref, gates_ref, kv_tail_ref, qk_tail_ref,
                  c_ref, n_ref, m_ref, *, tiles_per_seq):
    tt = TT_FRONT
    s = pl.program_id(0)
    first = (jnp.maximum(s - 1, 0) % tiles_per_seq) == 0

    @pl.when(s == 0)
    def _():
        slab_ref[...] = jnp.zeros_like(slab_ref)
        gates_ref[...] = jnp.zeros_like(gates_ref)
        kv_tail_ref[...] = jnp.zeros_like(kv_tail_ref)
        qk_tail_ref[...] = jnp.zeros_like(qk_tail_ref)

    @pl.when(first)
    def _():
        c_ref[...] = jnp.zeros_like(c_ref)
        n_ref[...] = jnp.zeros_like(n_ref)
        m_ref[...] = jnp.zeros_like(m_ref)

    norm, pieces = _projection_steps(g_ref, w_ref, wg_ref, hn_ref, hlo_ref, slab_new_ref, gates_new_ref, gab_ref)
    ml = _mlstm_steps(slab_ref, gates_ref, qk_tail_ref, first, convw_ref, bif_ref, gn_ref, ya_ref,
                      c_ref, n_ref, m_ref, tt // ML_CHUNK)
    sw = _swa_steps(slab_ref, kv_tail_ref, first, sinks_ref, yb_ref, tt // SW_WINDOW)
    norm(x_ref, hn_ref, hlo_ref, 0, tt)()
    for thunk in _interleave(pieces, ml, sw):
        thunk()

    kv_tail_ref[...] = slab_ref[tt - SW_WINDOW:tt, COL_KSW:N_SLAB]
    qk_tail_ref[...] = slab_ref[tt - BF16_SUBLANES:tt, COL_QK:COL_QK + 2 * ML_QK_W]
    slab_ref[...] = slab_new_ref[...]
    gates_ref[...] = gates_new_ref[...]


def _front(x2d, sinks, g, w_main, wg, conv_w, b_if, gn, seq):
    t = x2d.shape[0]
    tt = TT_FRONT
    nt = t // tt
    W = SW_WINDOW

    def const(shape):
        return pl.BlockSpec(shape, lambda s: (0, 0), pipeline_mode=pl.Buffered(1))

    proj_tile = lambda s: (jnp.minimum(s, nt - 1), 0)
    mix_tile = lambda s: (jnp.maximum(s - 1, 0), 0)
    return pl.pallas_call(
        functools.partial(_front_kernel, tiles_per_seq=seq // tt),
        out_shape=(jax.ShapeDtypeStruct((t, ML_V_W), BF16),
                   jax.ShapeDtypeStruct((t, SW_Q_W), BF16),
                   jax.ShapeDtypeStruct((t, N_GATES), BF16)),
        grid=(nt + 1,),
        in_specs=[
            pl.BlockSpec(memory_space=pltpu.SMEM),
            pl.BlockSpec((tt, D_MODEL), proj_tile),
            const((1, D_MODEL)),
            const((D_MODEL, N_W)),
            const((D_MODEL, 2 * LANES)),
            const((ML_CONV, 2 * ML_QK_W)),
            const((1, LANES)),
            const((1, ML_V_W)),
        ],
        out_specs=(pl.BlockSpec((tt, ML_V_W), mix_tile),
                   pl.BlockSpec((tt, SW_Q_W), mix_tile),
                   pl.BlockSpec((tt, N_GATES), proj_tile)),
        scratch_shapes=[
            pltpu.VMEM((tt, D_MODEL), BF16),
            pltpu.VMEM((tt, D_MODEL), BF16),
            pltpu.VMEM((tt, N_SLAB), BF16),
            pltpu.VMEM((tt, N_SLAB), BF16),
            pltpu.VMEM((tt, LANES), F32),
            pltpu.VMEM((tt, LANES), F32),
            pltpu.VMEM((W, N_SLAB - COL_KSW), BF16),
            pltpu.VMEM((BF16_SUBLANES, 2 * ML_QK_W), BF16),
            pltpu.VMEM((ML_HEADS, ML_DQK, ML_DV), F32),
            pltpu.VMEM((ML_HEADS, 1, ML_DQK), F32),
            pltpu.VMEM((ML_HEADS, 1, 1), F32),
        ],
        compiler_params=pltpu.CompilerParams(
            dimension_semantics=("arbitrary",),
            vmem_limit_bytes=56 * 1024 * 1024),
        name="front",
    )(sinks, x2d, g, w_main, wg, conv_w, b_if, gn)


def _merge_kernel(x_ref, ya_ref, yb_ref, ga_ref, gb_ref, wa_ref, wb_ref, wo_ref, out_ref):
    groups = [slice(r, r + ROW_GROUP) for r in range(0, TM_MERGE, ROW_GROUP)]
    merged = {}

    def branches(rows):
        a = jax.nn.sigmoid(ga_ref[rows, :].astype(F32)) * _dot(ya_ref[rows, :], wa_ref[...])
        b = jax.nn.sigmoid(gb_ref[rows, :].astype(F32)) * _dot(yb_ref[rows, :], wb_ref[...])
        merged[rows.start] = (a + b).astype(BF16)

    def project(rows):
        out_ref[rows, :] = x_ref[rows, :] + _dot(merged.pop(rows.start), wo_ref[...])

    for i in range(len(groups) + 1):
        if i < len(groups):
            branches(groups[i])
        if i >= 1:
            project(groups[i - 1])


def _merge(x2d, ya, yb, gab, wa, wb, wo):
    t = x2d.shape[0]
    tm = TM_MERGE
    wspec = pl.BlockSpec((D_MODEL, D_MODEL), lambda i: (0, 0), pipeline_mode=pl.Buffered(1))
    return pl.pallas_call(
        _merge_kernel,
        out_shape=jax.ShapeDtypeStruct((t, D_MODEL), F32),
        grid=(t // tm,),
        in_specs=[
            pl.BlockSpec((tm, D_MODEL), lambda i: (i, 0)),
            pl.BlockSpec((tm, D_MODEL), lambda i: (i, 0)),
            pl.BlockSpec((tm, D_MODEL), lambda i: (i, 0)),
            pl.BlockSpec((tm, D_MODEL), lambda i: (i, 0)),
            pl.BlockSpec((tm, D_MODEL), lambda i: (i, 1)),
            wspec, wspec, wspec,
        ],
        out_specs=pl.BlockSpec((tm, D_MODEL), lambda i: (i, 0)),
        compiler_params=pltpu.CompilerParams(
            dimension_semantics=("parallel",),
            vmem_limit_bytes=48 * 1024 * 1024),
        name="merge_out",
    )(x2d, ya, yb, gab, gab, wa, wb, wo)


def _mlp_ple_kernel(x_ref, p_ref, gm_ref, wup_ref, wdn_ref, gp_ref, wg_ref, wp_ref, gf_ref, out_ref,
                    *, final_norm):
    groups = [slice(r, r + ROW_GROUP) for r in range(0, TM_MLP, ROW_GROUP)]
    n_ff = D_FF // FF_CHUNK
    hn, acc = {}, {}

    def stage(g, k):
        rows = groups[g]
        if k == 0:
            x = x_ref[rows, :]
            hn[g] = _rms(x, gm_ref[...]).astype(BF16)
            acc[g] = x
        elif k <= n_ff:
            c = k - 1
            u = _dot(hn[g], wup_ref[:, c * FF_CHUNK:(c + 1) * FF_CHUNK])
            r = jnp.maximum(u, 0.0)
            acc[g] = acc[g] + _dot((r * r).astype(BF16), wdn_ref[c * FF_CHUNK:(c + 1) * FF_CHUNK, :])
        else:
            x = acc.pop(g)
            gate = jax.nn.sigmoid(_dot(_rms(x, gp_ref[...]).astype(BF16), wg_ref[...]))
            x = x + gate * _dot(p_ref[rows, :].astype(BF16), wp_ref[...])
            if final_norm:
                x = _rms(x, gf_ref[...])
            out_ref[rows, :] = x

    n_stages = n_ff + 2
    for tick in range(n_stages + MLP_STAGE_LAG * (len(groups) - 1)):
        for g in reversed(range(len(groups))):
            k = tick - MLP_STAGE_LAG * g
            if 0 <= k < n_stages:
                stage(g, k)


def _mlp_ple(x2d, p2d, gm, wup, wdn, gp, wg, wp, gf, final_norm):
    t = x2d.shape[0]
    tm = TM_MLP

    def const(shape):
        return pl.BlockSpec(shape, lambda i: (0, 0), pipeline_mode=pl.Buffered(1))

    return pl.pallas_call(
        functools.partial(_mlp_ple_kernel, final_norm=final_norm),
        out_shape=jax.ShapeDtypeStruct((t, D_MODEL), F32),
        grid=(t // tm,),
        in_specs=[
            pl.BlockSpec((tm, D_MODEL), lambda i: (i, 0)),
            pl.BlockSpec((tm, PLE_DIM), lambda i: (i, 0)),
            const((1, D_MODEL)),
            const((D_MODEL, D_FF)),
            const((D_FF, D_MODEL)),
            const((1, D_MODEL)),
            const((D_MODEL, D_MODEL)),
            const((PLE_DIM, D_MODEL)),
            const((1, D_MODEL)),
        ],
        out_specs=pl.BlockSpec((tm, D_MODEL), lambda i: (i, 0)),
        compiler_params=pltpu.CompilerParams(
            dimension_semantics=("parallel",),
            vmem_limit_bytes=56 * 1024 * 1024),
        name="mlp_ple",
    )(x2d, p2d, gm, wup, wdn, gp, wg, wp, gf)


def _prep_w_in_kernel(wa_ref, wb_ref, wif_ref, main_ref, wg_ref):
    i = pl.program_id(0)
    n_a = COL_QSW // PREP_COLS

    @pl.when(i < n_a)
    def _():
        main_ref[...] = wa_ref[...].T.astype(BF16)

    @pl.when(i >= n_a)
    def _():
        main_ref[...] = wb_ref[...].T.astype(BF16)

    @pl.when(i == 0)
    def _():
        lane = lax.broadcasted_iota(jnp.int32, (D_MODEL, LANES), 1)
        wg = jnp.where(lane < 2 * ML_HEADS, wif_ref[...].T, 0.0)
        wg_hi, wg_lo = _split_bf16(wg, 2)
        wg_ref[:, :LANES] = wg_hi
        wg_ref[:, LANES:] = wg_lo


def _prep_w_in(w_in):
    n_if = 2 * ML_HEADS
    cols = PREP_COLS
    n_a = COL_QSW // cols
    w_t = jnp.swapaxes(w_in, 0, 1)
    return pl.pallas_call(
        _prep_w_in_kernel,
        out_shape=(jax.ShapeDtypeStruct((D_MODEL, N_W), BF16),
                   jax.ShapeDtypeStruct((D_MODEL, 2 * LANES), BF16)),
        grid=(N_W // cols,),
        in_specs=[
            pl.BlockSpec((cols, D_MODEL), lambda i: (jnp.minimum(i, n_a - 1), 0)),
            pl.BlockSpec((pl.Element(cols), pl.Element(D_MODEL)),
                         lambda i: (pl.multiple_of(COL_QSW + n_if + jnp.maximum(i - n_a, 0) * cols, n_if), 0)),
            pl.BlockSpec((LANES, D_MODEL), lambda i: (COL_QSW // LANES, 0)),
        ],
        out_specs=(pl.BlockSpec((D_MODEL, cols), lambda i: (0, i)),
                   pl.BlockSpec((D_MODEL, 2 * LANES), lambda i: (0, 0))),
        compiler_params=pltpu.CompilerParams(
            dimension_semantics=("arbitrary",),
            vmem_limit_bytes=32 * 1024 * 1024),
        name="prep_w_in",
    )(w_t, w_t, w_t)


def kernel(x, p, norm_mix_g, w_in, conv_qk, b_if, mlstm_norm_g, sinks, w_branch_a, w_branch_b, w_out,
           norm_mlp_g, w_up, w_down, norm_ple_g, w_ple_gate, w_ple_proj, final_norm_g):
    batch, seq, _ = x.shape
    depth = w_in.shape[0]
    t = batch * seq
    x2d = x.reshape(t, D_MODEL)
    row = lambda v: v.reshape(1, -1)
    for i in range(depth):
        w_main, wg = _prep_w_in(w_in[i])
        bif = jnp.pad(b_if[i], (0, LANES - 2 * ML_HEADS)).reshape(1, LANES)
        ya, yb, gab = _front(x2d, sinks[i], row(norm_mix_g[i]), w_main, wg, conv_qk[i], bif,
                             row(mlstm_norm_g[i]), seq)
        x2d = _merge(x2d, ya, yb, gab, w_branch_a[i].astype(BF16), w_branch_b[i].astype(BF16),
                     w_out[i].astype(BF16))
        x2d = _mlp_ple(x2d, p[i].reshape(t, PLE_DIM), row(norm_mlp_g[i]), w_up[i].astype(BF16),
                       w_down[i].astype(BF16), row(norm_ple_g[i]), w_ple_gate[i].astype(BF16),
                       w_ple_proj[i].astype(BF16), row(final_norm_g), final_norm=(i == depth - 1))
    return x2d.reshape(batch, seq, D_MODEL)
```

```python
import functools

import jax
import jax.numpy as jnp
from jax import lax
from jax.experimental import pallas as pl
from jax.experimental.pallas import tpu as pltpu

D_MODEL = 1024
PLE_DIM = 256
ML_HEADS = 4
ML_DQK = 128
ML_DV = 256
ML_CONV = 4
SW_Q_HEADS = 16
SW_KV_HEADS = 4
SW_HEAD_DIM = 64
SW_WINDOW = 128
SW_GROUP = SW_Q_HEADS // SW_KV_HEADS
D_FF = 4 * D_MODEL
EPS = 1e-6

ML_QK_W = ML_HEADS * ML_DQK
ML_V_W = ML_HEADS * ML_DV
SW_Q_W = SW_Q_HEADS * SW_HEAD_DIM
SW_KV_W = SW_KV_HEADS * SW_HEAD_DIM

LANES = 128
BF16_SUBLANES = 16
MXU_N = 256

COL_QK = 0
COL_V = COL_QK + 2 * ML_QK_W
COL_O = COL_V + ML_V_W
COL_QSW = COL_O + ML_V_W
COL_KSW = COL_QSW + SW_Q_W
COL_VSW = COL_KSW + 2 * SW_KV_W
N_SLAB = COL_VSW + 2 * SW_KV_W
COL_GA = N_SLAB
COL_GB = COL_GA + D_MODEL
N_PROJ = COL_GB + D_MODEL
N_GATES = 2 * D_MODEL
WCOL_KSW = COL_KSW
WCOL_VSW = WCOL_KSW + SW_KV_W
WCOL_GA = WCOL_VSW + SW_KV_W
N_W = WCOL_GA + N_GATES

ML_CHUNK = 128
ML_SKEW = 1
SW_SKEW = 2
TT_FRONT = 512
PROJ_PIECE = 256
TM_MERGE = 1024
TM_MLP = 1024
ROW_GROUP = 512
MLP_STAGE_LAG = 2
FF_CHUNK = 1024
PREP_COLS = 256

F32 = jnp.float32
BF16 = jnp.bfloat16


def _dot(a, b):
    return jnp.dot(a, b, preferred_element_type=F32)


def _dot_nt(a, b):
    return lax.dot_general(a, b, (((1,), (1,)), ((), ())), preferred_element_type=F32)


def _rms(x, g):
    return x * lax.rsqrt(jnp.mean(x * x, axis=-1, keepdims=True) + EPS) * g


def _split_bf16(x, parts):
    out = []
    r = x
    for _ in range(parts - 1):
        t = r.astype(BF16)
        out.append(t)
        r = r - t.astype(F32)
    out.append(r.astype(BF16))
    return out


def _interleave(*lists):
    tagged = []
    for k, lst in enumerate(lists):
        tagged += [((i + 0.5) / len(lst), k, i, f) for i, f in enumerate(lst)]
    return [f for _, _, _, f in sorted(tagged, key=lambda e: e[:3])]


def _projection_steps(g_ref, w_ref, wg_ref, hn_ref, slab_ref, gates_ref, gab_ref):
    lo_half = lax.broadcasted_iota(jnp.int32, (TT_FRONT, LANES), 1) < SW_HEAD_DIM

    def norm(src_ref):
        hn_ref[...] = _rms(src_ref[...], g_ref[...]).astype(BF16)

    def gates():
        hi = _dot(hn_ref[...], wg_ref[...])
        gates_ref[...] = hi[:, :LANES] + hi[:, LANES:]

    def dup_heads(val):
        out = []
        for j in range(val.shape[1] // LANES):
            vj = val[:, j * LANES:(j + 1) * LANES]
            rj = pltpu.roll(vj, SW_HEAD_DIM, axis=1)
            out += [jnp.where(lo_half, vj, rj), jnp.where(lo_half, rj, vj)]
        return jnp.concatenate(out, axis=1)

    def piece(c0):
        def run():
            val = _dot(hn_ref[...], w_ref[:, c0:c0 + PROJ_PIECE])
            if c0 == WCOL_KSW:
                slab_ref[:, COL_KSW:COL_VSW] = dup_heads(val).astype(BF16)
            elif c0 == WCOL_VSW:
                slab_ref[:, COL_VSW:N_SLAB] = dup_heads(val).astype(BF16)
            elif c0 < WCOL_KSW:
                slab_ref[:, c0:c0 + PROJ_PIECE] = val.astype(BF16)
            else:
                gab_ref[:, c0 - WCOL_GA:c0 - WCOL_GA + PROJ_PIECE] = val.astype(BF16)
        return run

    return norm, [gates] + [piece(c0) for c0 in range(0, N_W, PROJ_PIECE)]


def _mlstm_steps(slab_ref, gates_ref, qk_tail_ref, first, convw_ref, bif_ref, gn_ref, y_ref,
                 c_ref, n_ref, m_ref, n_chunks):
    L = ML_CHUNK
    H = BF16_SUBLANES
    taps = ML_CONV - 1
    row = lax.broadcasted_iota(jnp.int32, (L, L), 0)
    col = lax.broadcasted_iota(jnp.int32, (L, L), 1)
    causal = row >= col
    tril = jnp.where(causal, 1.0, 0.0).astype(BF16)
    cw = convw_ref[...]
    bif = bif_ref[...]
    row8 = lax.broadcasted_iota(jnp.int32, (8, 2 * ML_QK_W), 0)

    def chunk_prep(c):
        r0 = c * L
        x_cur = slab_ref[r0:r0 + L, COL_QK:COL_QK + 2 * ML_QK_W].astype(F32)
        if c == 0:
            halo = jnp.where(first, 0.0, qk_tail_ref[...].astype(F32))[H - 8:, :]
        else:
            halo = slab_ref[r0 - H:r0, COL_QK:COL_QK + 2 * ML_QK_W].astype(F32)[H - 8:, :]
        acc = cw[taps:taps + 1, :] * x_cur
        for k in range(1, taps + 1):
            rolled = pltpu.roll(x_cur, k, axis=0)
            top = jnp.where(row8 < k, pltpu.roll(halo, k, axis=0), rolled[:8, :])
            acc = acc + cw[taps - k:taps - k + 1, :] * jnp.concatenate([top, rolled[8:, :]], axis=0)
        act = acc * jax.nn.sigmoid(acc)
        gates = gates_ref[r0:r0 + L, :] + bif
        logf = jnp.minimum(gates, 0.0) - jnp.log1p(jnp.exp(-jnp.abs(gates)))
        bsplit = _dot(tril, jnp.concatenate(_split_bf16(logf, 3), axis=1))
        bcum = bsplit[:, :LANES] + bsplit[:, LANES:2 * LANES] + bsplit[:, 2 * LANES:]
        return dict(q=act[:, :ML_QK_W] * (ML_DQK ** -0.5), k=act[:, ML_QK_W:], gates=gates, bcum=bcum,
                    gates_t=gates.T, bcum_t=bcum.T)

    def qk_part(c, h, pre):
        qf = pre["q"][:, h * ML_DQK:(h + 1) * ML_DQK]
        kf = pre["k"][:, h * ML_DQK:(h + 1) * ML_DQK]
        q = qf.astype(BF16)
        kt = kf.T.astype(BF16)
        return dict(q=q, qf=qf, kf=kf, kt=kt, qk=_dot(q, kt))

    def local_part(c, h, pre, qk):
        r0 = c * L
        fl = ML_HEADS + h
        b_c = pre["bcum"][:, fl:fl + 1]
        i_c = pre["gates"][:, h:h + 1]
        r_r = pre["gates_t"][h:h + 1, :] - pre["bcum_t"][fl:fl + 1, :]
        v = slab_ref[r0:r0 + L, COL_V + h * ML_DV:COL_V + (h + 1) * ML_DV]

        log_d = jnp.where(causal, b_c + r_r, -jnp.inf)
        a = jnp.max(log_d, axis=1, keepdims=True)
        s_loc = qk["qk"] * jnp.exp(log_d - a)
        b_last = pre["bcum"][L - 1:L, fl:fl + 1]
        log_w = b_last - b_c + i_c
        aw = jnp.max(log_w, axis=0, keepdims=True)
        w_loc = jnp.exp(log_w - aw)
        return dict(q=qk["q"], qf=qk["qf"], b_c=b_c, a=a, b_last=b_last, aw=aw,
                    sv=_dot(s_loc.astype(BF16), v), rs=jnp.sum(s_loc, axis=1, keepdims=True),
                    u=_dot(qk["kt"], (w_loc * v.astype(F32)).astype(BF16)),
                    nu=jnp.sum(w_loc * qk["kf"], axis=0, keepdims=True))

    def carried_part(c, h, loc, state):
        r0 = c * L
        ct, n_prev, m_prev = state
        inter = loc["b_c"] + m_prev
        m_t = jnp.maximum(inter, loc["a"])
        w_inter = jnp.exp(inter - m_t)
        f_loc = jnp.exp(loc["a"] - m_t)
        num = w_inter * _dot(loc["q"], ct.astype(BF16)) + f_loc * loc["sv"]
        den = w_inter * jnp.sum(loc["qf"] * n_prev, axis=1, keepdims=True) + f_loc * loc["rs"]
        rden = 1.0 / jnp.maximum(jnp.abs(den), jnp.exp(-m_t))
        ms = jnp.mean(num * num, axis=1, keepdims=True)
        scale = rden * lax.rsqrt(rden * rden * ms + EPS)
        og = jax.nn.sigmoid(slab_ref[r0:r0 + L, COL_O + h * ML_DV:COL_O + (h + 1) * ML_DV].astype(F32))
        y_ref[r0:r0 + L, h * ML_DV:(h + 1) * ML_DV] = (
            og * (num * scale * gn_ref[:, h * ML_DV:(h + 1) * ML_DV])).astype(BF16)
        m_new = jnp.maximum(loc["b_last"] + m_prev, loc["aw"])
        decay = jnp.exp(loc["b_last"] + m_prev - m_new)
        g_loc = jnp.exp(loc["aw"] - m_new)
        return decay * ct + g_loc * loc["u"], decay * n_prev + g_loc * loc["nu"], m_new

    units = [(c, h) for c in range(n_chunks) for h in range(ML_HEADS)]
    n_units = len(units)
    state, pre, qk, loc = {}, {}, {}, {}
    n_ticks = n_units + 2 * ML_SKEW

    def step(i):
        def run():
            if i == 0:
                for h in range(ML_HEADS):
                    state[h] = (c_ref[h], n_ref[h], m_ref[h])
                pre[0] = chunk_prep(0)
            if i + 1 < n_units and units[i + 1][1] == 0:
                pre[units[i + 1][0]] = chunk_prep(units[i + 1][0])
            if i < n_units:
                c, h = units[i]
                qk[i] = qk_part(c, h, pre[c])
            j = i - ML_SKEW
            if 0 <= j < n_units:
                c, h = units[j]
                loc[j] = local_part(c, h, pre[c], qk.pop(j))
            j = i - 2 * ML_SKEW
            if 0 <= j < n_units:
                c, h = units[j]
                state[h] = carried_part(c, h, loc.pop(j), state[h])
            if i == n_ticks - 1:
                for h in range(ML_HEADS):
                    c_ref[h], n_ref[h], m_ref[h] = state[h]
        return run

    return [step(i) for i in range(n_ticks)]


def _swa_steps(slab_ref, kv_tail_ref, first, sinks_ref, y_ref, n_blocks):
    W = SW_WINDOW
    hd = SW_HEAD_DIM
    qi = lax.broadcasted_iota(jnp.int32, (W, 2 * W), 0)
    key = lax.broadcasted_iota(jnp.int32, (W, 2 * W), 1) % W
    from_prev = key > qi
    lo_half = lax.broadcasted_iota(jnp.int32, (W, LANES), 1) < hd
    zero = jnp.zeros((), BF16)
    prev_bias = jnp.where(first, -jnp.inf, 0.0).astype(F32)

    def block_diag(d):
        return jnp.concatenate([jnp.where(lo_half, d, zero), jnp.where(lo_half, zero, d)], axis=0)

    cache = {}

    def kv_block(j, col0, h):
        if (j, col0, h) not in cache:
            if j < 0:
                d = kv_tail_ref[:, col0 - COL_KSW + h * LANES:col0 - COL_KSW + (h + 1) * LANES]
            else:
                d = slab_ref[j * W:(j + 1) * W, col0 + h * LANES:col0 + (h + 1) * LANES]
            cache[j, col0, h] = block_diag(d)
        return cache[j, col0, h]

    units = [(j, h, pair) for j in range(n_blocks) for h in range(SW_KV_HEADS)
             for pair in range(SW_GROUP // 2)]

    def logits_of(j, h, pair):
        c0 = COL_QSW + (h * SW_GROUP + 2 * pair) * hd
        qp = slab_ref[j * W:(j + 1) * W, c0:c0 + LANES] * jnp.asarray(hd ** -0.5, BF16)
        s_prev = _dot_nt(qp, kv_block(j - 1, COL_KSW, h))
        if j == 0:
            s_prev = s_prev + prev_bias
        return jnp.where(from_prev, s_prev, _dot_nt(qp, kv_block(j, COL_KSW, h)))

    def probs_of(lg, h, pair):
        halves = []
        for e in range(2):
            sink = sinks_ref[h * SW_GROUP + 2 * pair + e]
            le = lg[:, e * W:(e + 1) * W]
            m = jnp.maximum(jnp.max(le, axis=1, keepdims=True), sink)
            p = jnp.exp(le - m)
            denom = jnp.sum(p, axis=1, keepdims=True) + jnp.exp(sink - m)
            halves.append((p * (1.0 / denom)).astype(BF16))
        return jnp.concatenate(halves, axis=1)

    def write_out(pb, j, h, pair):
        c0 = (h * SW_GROUP + 2 * pair) * hd
        out = (_dot(jnp.where(from_prev, pb, zero), kv_block(j - 1, COL_VSW, h))
               + _dot(jnp.where(from_prev, zero, pb), kv_block(j, COL_VSW, h)))
        y_ref[j * W:(j + 1) * W, c0:c0 + LANES] = out.astype(BF16)

    logits, probs = {}, {}
    n = len(units)

    def step(i):
        def run():
            if i < n:
                logits[i] = logits_of(*units[i])
            if 0 <= i - SW_SKEW < n:
                _, h, pair = units[i - SW_SKEW]
                probs[i - SW_SKEW] = probs_of(logits.pop(i - SW_SKEW), h, pair)
            if 0 <= i - 2 * SW_SKEW < n:
                write_out(probs.pop(i - 2 * SW_SKEW), *units[i - 2 * SW_SKEW])
        return run

    return [step(i) for i in range(n + 2 * SW_SKEW)]


def _front_kernel(sinks_ref, x_ref, g_ref, w_ref, wg_ref, convw_ref, bif_ref, gn_ref,
                  ya_ref, yb_ref, gab_ref,
                  hn_ref, slab_new_ref, slab_ref, gates_new_ref, gates_ref, kv_tail_ref, qk_tail_ref,
                  c_ref, n_ref, m_ref, *, tiles_per_seq):
    tt = TT_FRONT
    s = pl.program_id(0)
    first = (jnp.maximum(s - 1, 0) % tiles_per_seq) == 0

    @pl.when(s == 0)
    def _():
        slab_ref[...] = jnp.zeros_like(slab_ref)
        gates_ref[...] = jnp.zeros_like(gates_ref)
        kv_tail_ref[...] = jnp.zeros_like(kv_tail_ref)
        qk_tail_ref[...] = jnp.zeros_like(qk_tail_ref)

    @pl.when(first)
    def _():
        c_ref[...] = jnp.zeros_like(c_ref)
        n_ref[...] = jnp.zeros_like(n_ref)
        m_ref[...] = jnp.zeros_like(m_ref)

    norm, pieces = _projection_steps(g_ref, w_ref, wg_ref, hn_ref, slab_new_ref, gates_new_ref, gab_ref)
    ml = _mlstm_steps(slab_ref, gates_ref, qk_tail_ref, first, convw_ref, bif_ref, gn_ref, ya_ref,
                      c_ref, n_ref, m_ref, tt // ML_CHUNK)
    sw = _swa_steps(slab_ref, kv_tail_ref, first, sinks_ref, yb_ref, tt // SW_WINDOW)
    norm(x_ref)
    for thunk in _interleave(pieces, ml, sw):
        thunk()

    kv_tail_ref[...] = slab_ref[tt - SW_WINDOW:tt, COL_KSW:N_SLAB]
    qk_tail_ref[...] = slab_ref[tt - BF16_SUBLANES:tt, COL_QK:COL_QK + 2 * ML_QK_W]
    slab_ref[...] = slab_new_ref[...]
    gates_ref[...] = gates_new_ref[...]


def _front(x2d, sinks, g, w_main, wg, conv_w, b_if, gn, seq):
    t = x2d.shape[0]
    tt = TT_FRONT
    nt = t // tt
    W = SW_WINDOW

    def const(shape):
        return pl.BlockSpec(shape, lambda s: (0, 0), pipeline_mode=pl.Buffered(1))

    proj_tile = lambda s: (jnp.minimum(s, nt - 1), 0)
    mix_tile = lambda s: (jnp.maximum(s - 1, 0), 0)
    return pl.pallas_call(
        functools.partial(_front_kernel, tiles_per_seq=seq // tt),
        out_shape=(jax.ShapeDtypeStruct((t, ML_V_W), BF16),
                   jax.ShapeDtypeStruct((t, SW_Q_W), BF16),
                   jax.ShapeDtypeStruct((t, N_GATES), BF16)),
        grid=(nt + 1,),
        in_specs=[
            pl.BlockSpec(memory_space=pltpu.SMEM),
            pl.BlockSpec((tt, D_MODEL), proj_tile),
            const((1, D_MODEL)),
            const((D_MODEL, N_W)),
            const((D_MODEL, 2 * LANES)),
            const((ML_CONV, 2 * ML_QK_W)),
            const((1, LANES)),
            const((1, ML_V_W)),
        ],
        out_specs=(pl.BlockSpec((tt, ML_V_W), mix_tile),
                   pl.BlockSpec((tt, SW_Q_W), mix_tile),
                   pl.BlockSpec((tt, N_GATES), proj_tile)),
        scratch_shapes=[
            pltpu.VMEM((tt, D_MODEL), BF16),
            pltpu.VMEM((tt, N_SLAB), BF16),
            pltpu.VMEM((tt, N_SLAB), BF16),
            pltpu.VMEM((tt, LANES), F32),
            pltpu.VMEM((tt, LANES), F32),
            pltpu.VMEM((W, N_SLAB - COL_KSW), BF16),
            pltpu.VMEM((BF16_SUBLANES, 2 * ML_QK_W), BF16),
            pltpu.VMEM((ML_HEADS, ML_DQK, ML_DV), F32),
            pltpu.VMEM((ML_HEADS, 1, ML_DQK), F32),
            pltpu.VMEM((ML_HEADS, 1, 1), F32),
        ],
        compiler_params=pltpu.CompilerParams(
            dimension_semantics=("arbitrary",),
            vmem_limit_bytes=56 * 1024 * 1024),
        name="front",
    )(sinks, x2d, g, w_main, wg, conv_w, b_if, gn)


def _merge_kernel(x_ref, ya_ref, yb_ref, ga_ref, gb_ref, wa_ref, wb_ref, wo_ref, out_ref):
    groups = [slice(r, r + ROW_GROUP) for r in range(0, TM_MERGE, ROW_GROUP)]
    merged = {}

    def branches(rows):
        a = jax.nn.sigmoid(ga_ref[rows, :].astype(F32)) * _dot(ya_ref[rows, :], wa_ref[...])
        b = jax.nn.sigmoid(gb_ref[rows, :].astype(F32)) * _dot(yb_ref[rows, :], wb_ref[...])
        merged[rows.start] = (a + b).astype(BF16)

    def project(rows):
        out_ref[rows, :] = x_ref[rows, :] + _dot(merged.pop(rows.start), wo_ref[...])

    for i in range(len(groups) + 1):
        if i < len(groups):
            branches(groups[i])
        if i >= 1:
            project(groups[i - 1])


def _merge(x2d, ya, yb, gab, wa, wb, wo):
    t = x2d.shape[0]
    tm = TM_MERGE
    wspec = pl.BlockSpec((D_MODEL, D_MODEL), lambda i: (0, 0), pipeline_mode=pl.Buffered(1))
    return pl.pallas_call(
        _merge_kernel,
        out_shape=jax.ShapeDtypeStruct((t, D_MODEL), F32),
        grid=(t // tm,),
        in_specs=[
            pl.BlockSpec((tm, D_MODEL), lambda i: (i, 0)),
            pl.BlockSpec((tm, D_MODEL), lambda i: (i, 0)),
            pl.BlockSpec((tm, D_MODEL), lambda i: (i, 0)),
            pl.BlockSpec((tm, D_MODEL), lambda i: (i, 0)),
            pl.BlockSpec((tm, D_MODEL), lambda i: (i, 1)),
            wspec, wspec, wspec,
        ],
        out_specs=pl.BlockSpec((tm, D_MODEL), lambda i: (i, 0)),
        compiler_params=pltpu.CompilerParams(
            dimension_semantics=("parallel",),
            vmem_limit_bytes=48 * 1024 * 1024),
        name="merge_out",
    )(x2d, ya, yb, gab, gab, wa, wb, wo)


def _mlp_ple_kernel(x_ref, p_ref, gm_ref, wup_ref, wdn_ref, gp_ref, wg_ref, wp_ref, gf_ref, out_ref,
                    *, final_norm):
    groups = [slice(r, r + ROW_GROUP) for r in range(0, TM_MLP, ROW_GROUP)]
    n_ff = D_FF // FF_CHUNK
    hn, acc = {}, {}

    def stage(g, k):
        rows = groups[g]
        if k == 0:
            x = x_ref[rows, :]
            hn[g] = _rms(x, gm_ref[...]).astype(BF16)
            acc[g] = x
        elif k <= n_ff:
            c = k - 1
            u = _dot(hn[g], wup_ref[:, c * FF_CHUNK:(c + 1) * FF_CHUNK])
            r = jnp.maximum(u, 0.0)
            acc[g] = acc[g] + _dot((r * r).astype(BF16), wdn_ref[c * FF_CHUNK:(c + 1) * FF_CHUNK, :])
        else:
            x = acc.pop(g)
            gate = jax.nn.sigmoid(_dot(_rms(x, gp_ref[...]).astype(BF16), wg_ref[...]))
            x = x + gate * _dot(p_ref[rows, :].astype(BF16), wp_ref[...])
            if final_norm:
                x = _rms(x, gf_ref[...])
            out_ref[rows, :] = x

    n_stages = n_ff + 2
    for tick in range(n_stages + MLP_STAGE_LAG * (len(groups) - 1)):
        for g in reversed(range(len(groups))):
            k = tick - MLP_STAGE_LAG * g
            if 0 <= k < n_stages:
                stage(g, k)


def _mlp_ple(x2d, p2d, gm, wup, wdn, gp, wg, wp, gf, final_norm):
    t = x2d.shape[0]
    tm = TM_MLP

    def const(shape):
        return pl.BlockSpec(shape, lambda i: (0, 0), pipeline_mode=pl.Buffered(1))

    return pl.pallas_call(
        functools.partial(_mlp_ple_kernel, final_norm=final_norm),
        out_shape=jax.ShapeDtypeStruct((t, D_MODEL), F32),
        grid=(t // tm,),
        in_specs=[
            pl.BlockSpec((tm, D_MODEL), lambda i: (i, 0)),
            pl.BlockSpec((tm, PLE_DIM), lambda i: (i, 0)),
            const((1, D_MODEL)),
            const((D_MODEL, D_FF)),
            const((D_FF, D_MODEL)),
            const((1, D_MODEL)),
            const((D_MODEL, D_MODEL)),
            const((PLE_DIM, D_MODEL)),
            const((1, D_MODEL)),
        ],
        out_specs=pl.BlockSpec((tm, D_MODEL), lambda i: (i, 0)),
        compiler_params=pltpu.CompilerParams(
            dimension_semantics=("parallel",),
            vmem_limit_bytes=56 * 1024 * 1024),
        name="mlp_ple",
    )(x2d, p2d, gm, wup, wdn, gp, wg, wp, gf)


def _prep_w_in_kernel(wa_ref, wb_ref, wif_ref, main_ref, wg_ref):
    i = pl.program_id(0)
    n_a = COL_QSW // PREP_COLS

    @pl.when(i < n_a)
    def _():
        main_ref[...] = wa_ref[...].T.astype(BF16)

    @pl.when(i >= n_a)
    def _():
        main_ref[...] = wb_ref[...].T.astype(BF16)

    @pl.when(i == 0)
    def _():
        lane = lax.broadcasted_iota(jnp.int32, (D_MODEL, LANES), 1)
        wg = jnp.where(lane < 2 * ML_HEADS, wif_ref[...].T, 0.0)
        wg_hi, wg_lo = _split_bf16(wg, 2)
        wg_ref[:, :LANES] = wg_hi
        wg_ref[:, LANES:] = wg_lo


def _prep_w_in(w_in):
    n_if = 2 * ML_HEADS
    cols = PREP_COLS
    n_a = COL_QSW // cols
    w_t = jnp.swapaxes(w_in, 0, 1)
    return pl.pallas_call(
        _prep_w_in_kernel,
        out_shape=(jax.ShapeDtypeStruct((D_MODEL, N_W), BF16),
                   jax.ShapeDtypeStruct((D_MODEL, 2 * LANES), BF16)),
        grid=(N_W // cols,),
        in_specs=[
            pl.BlockSpec((cols, D_MODEL), lambda i: (jnp.minimum(i, n_a - 1), 0)),
            pl.BlockSpec((pl.Element(cols), pl.Element(D_MODEL)),
                         lambda i: (pl.multiple_of(COL_QSW + n_if + jnp.maximum(i - n_a, 0) * cols, n_if), 0)),
            pl.BlockSpec((LANES, D_MODEL), lambda i: (COL_QSW // LANES, 0)),
        ],
        out_specs=(pl.BlockSpec((D_MODEL, cols), lambda i: (0, i)),
                   pl.BlockSpec((D_MODEL, 2 * LANES), lambda i: (0, 0))),
        compiler_params=pltpu.CompilerParams(
            dimension_semantics=("arbitrary",),
            vmem_limit_bytes=32 * 1024 * 1024),
        name="prep_w_in",
    )(w_t, w_t, w_t)


def kernel(x, p, norm_mix_g, w_in, conv_qk, b_if, mlstm_norm_g, sinks, w_branch_a, w_branch_b, w_out,
           norm_mlp_g, w_up, w_down, norm_ple_g, w_ple_gate, w_ple_proj, final_norm_g):
    batch, seq, _ = x.shape
    depth = w_in.shape[0]
    t = batch * seq
    x2d = x.reshape(t, D_MODEL)
    row = lambda v: v.reshape(1, -1)
    for i in range(depth):
        w_main, wg = _prep_w_in(w_in[i])
        bif = jnp.pad(b_if[i], (0, LANES - 2 * ML_HEADS)).reshape(1, LANES)
        ya, yb, gab = _front(x2d, sinks[i], row(norm_mix_g[i]), w_main, wg, conv_qk[i], bif,
                             row(mlstm_norm_g[i]), seq)
        x2d = _merge(x2d, ya, yb, gab, w_branch_a[i].astype(BF16), w_branch_b[i].astype(BF16),
                     w_out[i].astype(BF16))
        x2d = _mlp_ple(x2d, p[i].reshape(t, PLE_DIM), row(norm_mlp_g[i]), w_up[i].astype(BF16),
                       w_down[i].astype(BF16), row(norm_ple_g[i]), w_ple_gate[i].astype(BF16),
                       w_ple_proj[i].astype(BF16), row(final_norm_g), final_norm=(i == depth - 1))
    return x2d.reshape(batch, seq, D_MODEL)
```

```python
import functools

import jax
import jax.numpy as jnp
from jax import lax
from jax.experimental import pallas as pl
from jax.experimental.pallas import tpu as pltpu

D_MODEL = 1024
PLE_DIM = 256
ML_HEADS = 4
ML_DQK = 128
ML_DV = 256
ML_CONV = 4
SW_Q_HEADS = 16
SW_KV_HEADS = 4
SW_HEAD_DIM = 64
SW_WINDOW = 128
SW_GROUP = SW_Q_HEADS // SW_KV_HEADS
D_FF = 4 * D_MODEL
EPS = 1e-6

ML_QK_W = ML_HEADS * ML_DQK
ML_V_W = ML_HEADS * ML_DV
SW_Q_W = SW_Q_HEADS * SW_HEAD_DIM
SW_KV_W = SW_KV_HEADS * SW_HEAD_DIM

LANES = 128
BF16_SUBLANES = 16
MXU_N = 256

COL_QK = 0
COL_V = COL_QK + 2 * ML_QK_W
COL_O = COL_V + ML_V_W
COL_QSW = COL_O + ML_V_W
COL_KSW = COL_QSW + SW_Q_W
COL_VSW = COL_KSW + 2 * SW_KV_W
N_SLAB = COL_VSW + 2 * SW_KV_W
COL_GA = N_SLAB
COL_GB = COL_GA + D_MODEL
N_PROJ = COL_GB + D_MODEL
N_GATES = 2 * D_MODEL
WCOL_KSW = COL_KSW
WCOL_VSW = WCOL_KSW + SW_KV_W
WCOL_GA = WCOL_VSW + SW_KV_W
N_W = WCOL_GA + N_GATES

ML_CHUNK = 128
ML_SKEW = 1
SW_SKEW = 2
TT_FRONT = 512
PROJ_PIECE = 256
MIXERS_END = 0.9
TM_MERGE = 1024
TM_MLP = 1024
ROW_GROUP = 512
MLP_STAGE_LAG = 2
FF_CHUNK = 1024
PREP_COLS = 256

F32 = jnp.float32
BF16 = jnp.bfloat16


def _dot(a, b):
    return jnp.dot(a, b, preferred_element_type=F32)


def _dot_nt(a, b):
    return lax.dot_general(a, b, (((1,), (1,)), ((), ())), preferred_element_type=F32)


def _rms(x, g):
    return x * lax.rsqrt(jnp.mean(x * x, axis=-1, keepdims=True) + EPS) * g


def _split_bf16(x, parts):
    out = []
    r = x
    for _ in range(parts - 1):
        t = r.astype(BF16)
        out.append(t)
        r = r - t.astype(F32)
    out.append(r.astype(BF16))
    return out


def _interleave(*spans):
    tagged = []
    for k, (lst, lo, hi) in enumerate(spans):
        tagged += [(lo + (hi - lo) * (i + 0.5) / len(lst), k, i, f) for i, f in enumerate(lst)]
    return [f for _, _, _, f in sorted(tagged, key=lambda e: e[:3])]


def _projection_steps(g_ref, w_ref, wg_ref, hn_ref, slab_ref, gates_ref, gab_ref):
    lo_half = lax.broadcasted_iota(jnp.int32, (TT_FRONT, LANES), 1) < SW_HEAD_DIM

    def norm(src_ref):
        hn_ref[...] = _rms(src_ref[...], g_ref[...]).astype(BF16)

    def gates():
        hi = _dot(hn_ref[...], wg_ref[...])
        gates_ref[...] = hi[:, :LANES] + hi[:, LANES:]

    def dup_heads(val):
        out = []
        for j in range(val.shape[1] // LANES):
            vj = val[:, j * LANES:(j + 1) * LANES]
            rj = pltpu.roll(vj, SW_HEAD_DIM, axis=1)
            out += [jnp.where(lo_half, vj, rj), jnp.where(lo_half, rj, vj)]
        return jnp.concatenate(out, axis=1)

    def piece(c0):
        def run():
            val = _dot(hn_ref[...], w_ref[:, c0:c0 + PROJ_PIECE])
            if c0 == WCOL_KSW:
                slab_ref[:, COL_KSW:COL_VSW] = dup_heads(val).astype(BF16)
            elif c0 == WCOL_VSW:
                slab_ref[:, COL_VSW:N_SLAB] = dup_heads(val).astype(BF16)
            elif c0 < WCOL_KSW:
                slab_ref[:, c0:c0 + PROJ_PIECE] = val.astype(BF16)
            else:
                gab_ref[:, c0 - WCOL_GA:c0 - WCOL_GA + PROJ_PIECE] = val.astype(BF16)
        return run

    return norm, [gates] + [piece(c0) for c0 in range(0, N_W, PROJ_PIECE)]


def _mlstm_steps(slab_ref, gates_ref, qk_tail_ref, first, convw_ref, bif_ref, gn_ref, y_ref,
                 c_ref, n_ref, m_ref, n_chunks):
    L = ML_CHUNK
    H = BF16_SUBLANES
    taps = ML_CONV - 1
    row = lax.broadcasted_iota(jnp.int32, (L, L), 0)
    col = lax.broadcasted_iota(jnp.int32, (L, L), 1)
    causal = row >= col
    tril = jnp.where(causal, 1.0, 0.0).astype(BF16)
    cw = convw_ref[...]
    bif = bif_ref[...]
    row8 = lax.broadcasted_iota(jnp.int32, (8, 2 * ML_QK_W), 0)

    def chunk_prep(c):
        r0 = c * L
        x_cur = slab_ref[r0:r0 + L, COL_QK:COL_QK + 2 * ML_QK_W].astype(F32)
        if c == 0:
            halo = jnp.where(first, 0.0, qk_tail_ref[...].astype(F32))[H - 8:, :]
        else:
            halo = slab_ref[r0 - H:r0, COL_QK:COL_QK + 2 * ML_QK_W].astype(F32)[H - 8:, :]
        acc = cw[taps:taps + 1, :] * x_cur
        for k in range(1, taps + 1):
            rolled = pltpu.roll(x_cur, k, axis=0)
            top = jnp.where(row8 < k, pltpu.roll(halo, k, axis=0), rolled[:8, :])
            acc = acc + cw[taps - k:taps - k + 1, :] * jnp.concatenate([top, rolled[8:, :]], axis=0)
        act = acc * jax.nn.sigmoid(acc)
        gates = gates_ref[r0:r0 + L, :] + bif
        logf = jnp.minimum(gates, 0.0) - jnp.log1p(jnp.exp(-jnp.abs(gates)))
        bsplit = _dot(tril, jnp.concatenate(_split_bf16(logf, 3), axis=1))
        bcum = bsplit[:, :LANES] + bsplit[:, LANES:2 * LANES] + bsplit[:, 2 * LANES:]
        return dict(q=act[:, :ML_QK_W] * (ML_DQK ** -0.5), k=act[:, ML_QK_W:], gates=gates, bcum=bcum,
                    gates_t=gates.T, bcum_t=bcum.T)

    def qk_part(c, h, pre):
        qf = pre["q"][:, h * ML_DQK:(h + 1) * ML_DQK]
        kf = pre["k"][:, h * ML_DQK:(h + 1) * ML_DQK]
        q = qf.astype(BF16)
        kt = kf.T.astype(BF16)
        return dict(q=q, qf=qf, kf=kf, kt=kt, qk=_dot(q, kt))

    def local_part(c, h, pre, qk):
        r0 = c * L
        fl = ML_HEADS + h
        b_c = pre["bcum"][:, fl:fl + 1]
        i_c = pre["gates"][:, h:h + 1]
        r_r = pre["gates_t"][h:h + 1, :] - pre["bcum_t"][fl:fl + 1, :]
        v = slab_ref[r0:r0 + L, COL_V + h * ML_DV:COL_V + (h + 1) * ML_DV]

        log_d = jnp.where(causal, b_c + r_r, -jnp.inf)
        a = jnp.max(log_d, axis=1, keepdims=True)
        s_loc = qk["qk"] * jnp.exp(log_d - a)
        b_last = pre["bcum"][L - 1:L, fl:fl + 1]
        log_w = b_last - b_c + i_c
        aw = jnp.max(log_w, axis=0, keepdims=True)
        w_loc = jnp.exp(log_w - aw)
        return dict(q=qk["q"], qf=qk["qf"], b_c=b_c, a=a, b_last=b_last, aw=aw,
                    sv=_dot(s_loc.astype(BF16), v), rs=jnp.sum(s_loc, axis=1, keepdims=True),
                    u=_dot(qk["kt"], (w_loc * v.astype(F32)).astype(BF16)),
                    nu=jnp.sum(w_loc * qk["kf"], axis=0, keepdims=True))

    def carried_part(c, h, loc, state):
        r0 = c * L
        ct, n_prev, m_prev = state
        inter = loc["b_c"] + m_prev
        m_t = jnp.maximum(inter, loc["a"])
        w_inter = jnp.exp(inter - m_t)
        f_loc = jnp.exp(loc["a"] - m_t)
        num = w_inter * _dot(loc["q"], ct.astype(BF16)) + f_loc * loc["sv"]
        den = w_inter * jnp.sum(loc["qf"] * n_prev, axis=1, keepdims=True) + f_loc * loc["rs"]
        rden = 1.0 / jnp.maximum(jnp.abs(den), jnp.exp(-m_t))
        ms = jnp.mean(num * num, axis=1, keepdims=True)
        scale = rden * lax.rsqrt(rden * rden * ms + EPS)
        og = jax.nn.sigmoid(slab_ref[r0:r0 + L, COL_O + h * ML_DV:COL_O + (h + 1) * ML_DV].astype(F32))
        y_ref[r0:r0 + L, h * ML_DV:(h + 1) * ML_DV] = (
            og * (num * scale * gn_ref[:, h * ML_DV:(h + 1) * ML_DV])).astype(BF16)
        m_new = jnp.maximum(loc["b_last"] + m_prev, loc["aw"])
        decay = jnp.exp(loc["b_last"] + m_prev - m_new)
        g_loc = jnp.exp(loc["aw"] - m_new)
        return decay * ct + g_loc * loc["u"], decay * n_prev + g_loc * loc["nu"], m_new

    units = [(c, h) for c in range(n_chunks) for h in range(ML_HEADS)]
    n_units = len(units)
    state, pre, qk, loc = {}, {}, {}, {}
    n_ticks = n_units + 2 * ML_SKEW

    def step(i):
        def run():
            if i == 0:
                for h in range(ML_HEADS):
                    state[h] = (c_ref[h], n_ref[h], m_ref[h])
                pre[0] = chunk_prep(0)
            if i + 1 < n_units and units[i + 1][1] == 0:
                pre[units[i + 1][0]] = chunk_prep(units[i + 1][0])
            if i < n_units:
                c, h = units[i]
                qk[i] = qk_part(c, h, pre[c])
            j = i - ML_SKEW
            if 0 <= j < n_units:
                c, h = units[j]
                loc[j] = local_part(c, h, pre[c], qk.pop(j))
            j = i - 2 * ML_SKEW
            if 0 <= j < n_units:
                c, h = units[j]
                state[h] = carried_part(c, h, loc.pop(j), state[h])
            if i == n_ticks - 1:
                for h in range(ML_HEADS):
                    c_ref[h], n_ref[h], m_ref[h] = state[h]
        return run

    return [step(i) for i in range(n_ticks)]


def _swa_steps(slab_ref, kv_tail_ref, first, sinks_ref, y_ref, n_blocks):
    W = SW_WINDOW
    hd = SW_HEAD_DIM
    qi = lax.broadcasted_iota(jnp.int32, (W, 2 * W), 0)
    key = lax.broadcasted_iota(jnp.int32, (W, 2 * W), 1) % W
    from_prev = key > qi
    lo_half = lax.broadcasted_iota(jnp.int32, (W, LANES), 1) < hd
    zero = jnp.zeros((), BF16)
    prev_bias = jnp.where(first, -jnp.inf, 0.0).astype(F32)

    def block_diag(d):
        return jnp.concatenate([jnp.where(lo_half, d, zero), jnp.where(lo_half, zero, d)], axis=0)

    cache = {}

    def kv_block(j, col0, h):
        if (j, col0, h) not in cache:
            if j < 0:
                d = kv_tail_ref[:, col0 - COL_KSW + h * LANES:col0 - COL_KSW + (h + 1) * LANES]
            else:
                d = slab_ref[j * W:(j + 1) * W, col0 + h * LANES:col0 + (h + 1) * LANES]
            cache[j, col0, h] = block_diag(d)
        return cache[j, col0, h]

    units = [(j, h, pair) for j in range(n_blocks) for h in range(SW_KV_HEADS)
             for pair in range(SW_GROUP // 2)]

    def logits_of(j, h, pair):
        c0 = COL_QSW + (h * SW_GROUP + 2 * pair) * hd
        qp = slab_ref[j * W:(j + 1) * W, c0:c0 + LANES] * jnp.asarray(hd ** -0.5, BF16)
        s_prev = _dot_nt(qp, kv_block(j - 1, COL_KSW, h))
        if j == 0:
            s_prev = s_prev + prev_bias
        return jnp.where(from_prev, s_prev, _dot_nt(qp, kv_block(j, COL_KSW, h)))

    def probs_of(lg, h, pair):
        halves = []
        for e in range(2):
            sink = sinks_ref[h * SW_GROUP + 2 * pair + e]
            le = lg[:, e * W:(e + 1) * W]
            m = jnp.maximum(jnp.max(le, axis=1, keepdims=True), sink)
            p = jnp.exp(le - m)
            denom = jnp.sum(p, axis=1, keepdims=True) + jnp.exp(sink - m)
            halves.append((p * (1.0 / denom)).astype(BF16))
        return jnp.concatenate(halves, axis=1)

    def write_out(pb, j, h, pair):
        c0 = (h * SW_GROUP + 2 * pair) * hd
        out = (_dot(jnp.where(from_prev, pb, zero), kv_block(j - 1, COL_VSW, h))
               + _dot(jnp.where(from_prev, zero, pb), kv_block(j, COL_VSW, h)))
        y_ref[j * W:(j + 1) * W, c0:c0 + LANES] = out.astype(BF16)

    logits, probs = {}, {}
    n = len(units)

    def step(i):
        def run():
            if i < n:
                logits[i] = logits_of(*units[i])
            if 0 <= i - SW_SKEW < n:
                _, h, pair = units[i - SW_SKEW]
                probs[i - SW_SKEW] = probs_of(logits.pop(i - SW_SKEW), h, pair)
            if 0 <= i - 2 * SW_SKEW < n:
                write_out(probs.pop(i - 2 * SW_SKEW), *units[i - 2 * SW_SKEW])
        return run

    return [step(i) for i in range(n + 2 * SW_SKEW)]


def _front_kernel(sinks_ref, x_ref, g_ref, w_ref, wg_ref, convw_ref, bif_ref, gn_ref,
                  ya_ref, yb_ref, gab_ref,
                  hn_ref, slab_a_ref, slab_b_ref, gates_a_ref, gates_b_ref, kv_tail_ref, qk_tail_ref,
                  c_ref, n_ref, m_ref, *, tiles_per_seq):
    tt = TT_FRONT
    s = pl.program_id(0)
    first = (jnp.maximum(s - 1, 0) % tiles_per_seq) == 0

    @pl.when(s == 0)
    def _():
        slab_b_ref[...] = jnp.zeros_like(slab_b_ref)
        gates_b_ref[...] = jnp.zeros_like(gates_b_ref)
        kv_tail_ref[...] = jnp.zeros_like(kv_tail_ref)
        qk_tail_ref[...] = jnp.zeros_like(qk_tail_ref)

    @pl.when(first)
    def _():
        c_ref[...] = jnp.zeros_like(c_ref)
        n_ref[...] = jnp.zeros_like(n_ref)
        m_ref[...] = jnp.zeros_like(m_ref)

    def body(slab_new_ref, gates_new_ref, slab_ref, gates_ref):
        norm, pieces = _projection_steps(g_ref, w_ref, wg_ref, hn_ref, slab_new_ref, gates_new_ref, gab_ref)
        ml = _mlstm_steps(slab_ref, gates_ref, qk_tail_ref, first, convw_ref, bif_ref, gn_ref, ya_ref,
                          c_ref, n_ref, m_ref, tt // ML_CHUNK)
        sw = _swa_steps(slab_ref, kv_tail_ref, first, sinks_ref, yb_ref, tt // SW_WINDOW)
        norm(x_ref)
        for thunk in _interleave((pieces, 0.0, 1.0), (ml, 0.0, MIXERS_END), (sw, 0.0, MIXERS_END)):
            thunk()
        kv_tail_ref[...] = slab_ref[tt - SW_WINDOW:tt, COL_KSW:N_SLAB]
        qk_tail_ref[...] = slab_ref[tt - BF16_SUBLANES:tt, COL_QK:COL_QK + 2 * ML_QK_W]

    @pl.when(s % 2 == 0)
    def _():
        body(slab_a_ref, gates_a_ref, slab_b_ref, gates_b_ref)

    @pl.when(s % 2 == 1)
    def _():
        body(slab_b_ref, gates_b_ref, slab_a_ref, gates_a_ref)


def _front(x2d, sinks, g, w_main, wg, conv_w, b_if, gn, seq):
    t = x2d.shape[0]
    tt = TT_FRONT
    nt = t // tt
    W = SW_WINDOW

    def const(shape):
        return pl.BlockSpec(shape, lambda s: (0, 0), pipeline_mode=pl.Buffered(1))

    proj_tile = lambda s: (jnp.minimum(s, nt - 1), 0)
    mix_tile = lambda s: (jnp.maximum(s - 1, 0), 0)
    return pl.pallas_call(
        functools.partial(_front_kernel, tiles_per_seq=seq // tt),
        out_shape=(jax.ShapeDtypeStruct((t, ML_V_W), BF16),
                   jax.ShapeDtypeStruct((t, SW_Q_W), BF16),
                   jax.ShapeDtypeStruct((t, N_GATES), BF16)),
        grid=(nt + 1,),
        in_specs=[
            pl.BlockSpec(memory_space=pltpu.SMEM),
            pl.BlockSpec((tt, D_MODEL), proj_tile),
            const((1, D_MODEL)),
            const((D_MODEL, N_W)),
            const((D_MODEL, 2 * LANES)),
            const((ML_CONV, 2 * ML_QK_W)),
            const((1, LANES)),
            const((1, ML_V_W)),
        ],
        out_specs=(pl.BlockSpec((tt, ML_V_W), mix_tile),
                   pl.BlockSpec((tt, SW_Q_W), mix_tile),
                   pl.BlockSpec((tt, N_GATES), proj_tile)),
        scratch_shapes=[
            pltpu.VMEM((tt, D_MODEL), BF16),
            pltpu.VMEM((tt, N_SLAB), BF16),
            pltpu.VMEM((tt, N_SLAB), BF16),
            pltpu.VMEM((tt, LANES), F32),
            pltpu.VMEM((tt, LANES), F32),
            pltpu.VMEM((W, N_SLAB - COL_KSW), BF16),
            pltpu.VMEM((BF16_SUBLANES, 2 * ML_QK_W), BF16),
            pltpu.VMEM((ML_HEADS, ML_DQK, ML_DV), F32),
            pltpu.VMEM((ML_HEADS, 1, ML_DQK), F32),
            pltpu.VMEM((ML_HEADS, 1, 1), F32),
        ],
        compiler_params=pltpu.CompilerParams(
            dimension_semantics=("arbitrary",),
            vmem_limit_bytes=56 * 1024 * 1024),
        name="front",
    )(sinks, x2d, g, w_main, wg, conv_w, b_if, gn)


def _merge_kernel(x_ref, ya_ref, yb_ref, ga_ref, gb_ref, wa_ref, wb_ref, wo_ref, out_ref):
    groups = [slice(r, r + ROW_GROUP) for r in range(0, TM_MERGE, ROW_GROUP)]
    merged = {}

    def branches(rows):
        a = jax.nn.sigmoid(ga_ref[rows, :].astype(F32)) * _dot(ya_ref[rows, :], wa_ref[...])
        b = jax.nn.sigmoid(gb_ref[rows, :].astype(F32)) * _dot(yb_ref[rows, :], wb_ref[...])
        merged[rows.start] = (a + b).astype(BF16)

    def project(rows):
        out_ref[rows, :] = x_ref[rows, :] + _dot(merged.pop(rows.start), wo_ref[...])

    for i in range(len(groups) + 1):
        if i < len(groups):
            branches(groups[i])
        if i >= 1:
            project(groups[i - 1])


def _merge(x2d, ya, yb, gab, wa, wb, wo):
    t = x2d.shape[0]
    tm = TM_MERGE
    wspec = pl.BlockSpec((D_MODEL, D_MODEL), lambda i: (0, 0), pipeline_mode=pl.Buffered(1))
    return pl.pallas_call(
        _merge_kernel,
        out_shape=jax.ShapeDtypeStruct((t, D_MODEL), F32),
        grid=(t // tm,),
        in_specs=[
            pl.BlockSpec((tm, D_MODEL), lambda i: (i, 0)),
            pl.BlockSpec((tm, D_MODEL), lambda i: (i, 0)),
            pl.BlockSpec((tm, D_MODEL), lambda i: (i, 0)),
            pl.BlockSpec((tm, D_MODEL), lambda i: (i, 0)),
            pl.BlockSpec((tm, D_MODEL), lambda i: (i, 1)),
            wspec, wspec, wspec,
        ],
        out_specs=pl.BlockSpec((tm, D_MODEL), lambda i: (i, 0)),
        compiler_params=pltpu.CompilerParams(
            dimension_semantics=("parallel",),
            vmem_limit_bytes=48 * 1024 * 1024),
        name="merge_out",
    )(x2d, ya, yb, gab, gab, wa, wb, wo)


def _mlp_ple_kernel(x_ref, p_ref, gm_ref, wup_ref, wdn_ref, gp_ref, wg_ref, wp_ref, gf_ref, out_ref,
                    *, final_norm):
    groups = [slice(r, r + ROW_GROUP) for r in range(0, TM_MLP, ROW_GROUP)]
    n_ff = D_FF // FF_CHUNK
    hn, acc = {}, {}

    def stage(g, k):
        rows = groups[g]
        if k == 0:
            x = x_ref[rows, :]
            hn[g] = _rms(x, gm_ref[...]).astype(BF16)
            acc[g] = x
        elif k <= n_ff:
            c = k - 1
            u = _dot(hn[g], wup_ref[:, c * FF_CHUNK:(c + 1) * FF_CHUNK])
            r = jnp.maximum(u, 0.0)
            acc[g] = acc[g] + _dot((r * r).astype(BF16), wdn_ref[c * FF_CHUNK:(c + 1) * FF_CHUNK, :])
        else:
            x = acc.pop(g)
            gate = jax.nn.sigmoid(_dot(_rms(x, gp_ref[...]).astype(BF16), wg_ref[...]))
            x = x + gate * _dot(p_ref[rows, :].astype(BF16), wp_ref[...])
            if final_norm:
                x = _rms(x, gf_ref[...])
            out_ref[rows, :] = x

    n_stages = n_ff + 2
    for tick in range(n_stages + MLP_STAGE_LAG * (len(groups) - 1)):
        for g in reversed(range(len(groups))):
            k = tick - MLP_STAGE_LAG * g
            if 0 <= k < n_stages:
                stage(g, k)


def _mlp_ple(x2d, p2d, gm, wup, wdn, gp, wg, wp, gf, final_norm):
    t = x2d.shape[0]
    tm = TM_MLP

    def const(shape):
        return pl.BlockSpec(shape, lambda i: (0, 0), pipeline_mode=pl.Buffered(1))

    return pl.pallas_call(
        functools.partial(_mlp_ple_kernel, final_norm=final_norm),
        out_shape=jax.ShapeDtypeStruct((t, D_MODEL), F32),
        grid=(t // tm,),
        in_specs=[
            pl.BlockSpec((tm, D_MODEL), lambda i: (i, 0)),
            pl.BlockSpec((tm, PLE_DIM), lambda i: (i, 0)),
            const((1, D_MODEL)),
            const((D_MODEL, D_FF)),
            const((D_FF, D_MODEL)),
            const((1, D_MODEL)),
            const((D_MODEL, D_MODEL)),
            const((PLE_DIM, D_MODEL)),
            const((1, D_MODEL)),
        ],
        out_specs=pl.BlockSpec((tm, D_MODEL), lambda i: (i, 0)),
        compiler_params=pltpu.CompilerParams(
            dimension_semantics=("parallel",),
            vmem_limit_bytes=56 * 1024 * 1024),
        name="mlp_ple",
    )(x2d, p2d, gm, wup, wdn, gp, wg, wp, gf)


def _prep_w_in_kernel(wa_ref, wb_ref, wif_ref, main_ref, wg_ref):
    i = pl.program_id(0)
    n_a = COL_QSW // PREP_COLS

    @pl.when(i < n_a)
    def _():
        main_ref[...] = wa_ref[...].T.astype(BF16)

    @pl.when(i >= n_a)
    def _():
        main_ref[...] = wb_ref[...].T.astype(BF16)

    @pl.when(i == 0)
    def _():
        lane = lax.broadcasted_iota(jnp.int32, (D_MODEL, LANES), 1)
        wg = jnp.where(lane < 2 * ML_HEADS, wif_ref[...].T, 0.0)
        wg_hi, wg_lo = _split_bf16(wg, 2)
        wg_ref[:, :LANES] = wg_hi
        wg_ref[:, LANES:] = wg_lo


def _prep_w_in(w_in):
    n_if = 2 * ML_HEADS
    cols = PREP_COLS
    n_a = COL_QSW // cols
    w_t = jnp.swapaxes(w_in, 0, 1)
    return pl.pallas_call(
        _prep_w_in_kernel,
        out_shape=(jax.ShapeDtypeStruct((D_MODEL, N_W), BF16),
                   jax.ShapeDtypeStruct((D_MODEL, 2 * LANES), BF16)),
        grid=(N_W // cols,),
        in_specs=[
            pl.BlockSpec((cols, D_MODEL), lambda i: (jnp.minimum(i, n_a - 1), 0)),
            pl.BlockSpec((pl.Element(cols), pl.Element(D_MODEL)),
                         lambda i: (pl.multiple_of(COL_QSW + n_if + jnp.maximum(i - n_a, 0) * cols, n_if), 0)),
            pl.BlockSpec((LANES, D_MODEL), lambda i: (COL_QSW // LANES, 0)),
        ],
        out_specs=(pl.BlockSpec((D_MODEL, cols), lambda i: (0, i)),
                   pl.BlockSpec((D_MODEL, 2 * LANES), lambda i: (0, 0))),
        compiler_params=pltpu.CompilerParams(
            dimension_semantics=("arbitrary",),
            vmem_limit_bytes=32 * 1024 * 1024),
        name="prep_w_in",
    )(w_t, w_t, w_t)


def kernel(x, p, norm_mix_g, w_in, conv_qk, b_if, mlstm_norm_g, sinks, w_branch_a, w_branch_b, w_out,
           norm_mlp_g, w_up, w_down, norm_ple_g, w_ple_gate, w_ple_proj, final_norm_g):
    batch, seq, _ = x.shape
    depth = w_in.shape[0]
    t = batch * seq
    x2d = x.reshape(t, D_MODEL)
    row = lambda v: v.reshape(1, -1)
    for i in range(depth):
        w_main, wg = _prep_w_in(w_in[i])
        bif = jnp.pad(b_if[i], (0, LANES - 2 * ML_HEADS)).reshape(1, LANES)
        ya, yb, gab = _front(x2d, sinks[i], row(norm_mix_g[i]), w_main, wg, conv_qk[i], bif,
                             row(mlstm_norm_g[i]), seq)
        x2d = _merge(x2d, ya, yb, gab, w_branch_a[i].astype(BF16), w_branch_b[i].astype(BF16),
                     w_out[i].astype(BF16))
        x2d = _mlp_ple(x2d, p[i].reshape(t, PLE_DIM), row(norm_mlp_g[i]), w_up[i].astype(BF16),
                       w_down[i].astype(BF16), row(norm_ple_g[i]), w_ple_gate[i].astype(BF16),
                       w_ple_proj[i].astype(BF16), row(final_norm_g), final_norm=(i == depth - 1))
    return x2d.reshape(batch, seq, D_MODEL)
```

```python
import functools
import math

import jax
import jax.numpy as jnp
from jax import lax
from jax.experimental import pallas as pl
from jax.experimental.pallas import tpu as pltpu

D_MODEL = 1024
PLE_DIM = 256
ML_HEADS = 4
ML_DQK = 128
ML_DV = 256
ML_CONV = 4
SW_Q_HEADS = 16
SW_KV_HEADS = 4
SW_HEAD_DIM = 64
SW_WINDOW = 128
SW_GROUP = SW_Q_HEADS // SW_KV_HEADS
D_FF = 4 * D_MODEL
EPS = 1e-6

ML_QK_W = ML_HEADS * ML_DQK
ML_V_W = ML_HEADS * ML_DV
SW_Q_W = SW_Q_HEADS * SW_HEAD_DIM
SW_KV_W = SW_KV_HEADS * SW_HEAD_DIM

LANES = 128
BF16_SUBLANES = 16
MXU_N = 256
VMEM_BYTES = 64 * 1024 * 1024
VMEM_COMPILER_SHARE = 16 * 1024 * 1024

COL_QK = 0
COL_V = COL_QK + 2 * ML_QK_W
COL_O = COL_V + ML_V_W
COL_QSW = COL_O + ML_V_W
COL_KSW = COL_QSW + SW_Q_W
COL_VSW = COL_KSW + 2 * SW_KV_W
N_SLAB = COL_VSW + 2 * SW_KV_W
N_GATES = 2 * D_MODEL
WCOL_KSW = COL_KSW
WCOL_VSW = WCOL_KSW + SW_KV_W
WCOL_GA = WCOL_VSW + SW_KV_W
N_W = WCOL_GA + N_GATES

ML_CHUNK = 128
ML_SKEW = 1
SW_SKEW = 2
TT_FRONT = 512
PROJ_PIECE = MXU_N
TM_MERGE = 1024
TM_MLP = 1024
ROW_GROUP = 512
MLP_STAGE_LAG = 2
FF_CHUNK = 1024
PREP_COLS = 512

F32 = jnp.float32
BF16 = jnp.bfloat16


def _vmem_limit(*buffers):
    need = sum(copies * math.prod(shape) * jnp.dtype(dtype).itemsize for shape, dtype, copies in buffers)
    assert need + VMEM_COMPILER_SHARE <= VMEM_BYTES, need
    return need + VMEM_COMPILER_SHARE


def _dot(a, b):
    return jnp.dot(a, b, preferred_element_type=F32)


def _dot_nt(a, b):
    return lax.dot_general(a, b, (((1,), (1,)), ((), ())), preferred_element_type=F32)


def _rms(x, g):
    return x * lax.rsqrt(jnp.mean(x * x, axis=-1, keepdims=True) + EPS) * g


def _split_bf16(x, parts):
    out = []
    r = x
    for _ in range(parts - 1):
        t = r.astype(BF16)
        out.append(t)
        r = r - t.astype(F32)
    out.append(r.astype(BF16))
    return out


def _interleave(*lists):
    tagged = []
    for k, lst in enumerate(lists):
        tagged += [((i + 0.5) / len(lst), k, i, f) for i, f in enumerate(lst)]
    return [f for _, _, _, f in sorted(tagged, key=lambda e: e[:3])]


def _projection_steps(g_ref, w_ref, wg_ref, hn_ref, slab_ref, gates_ref, gab_ref):
    lo_half = lax.broadcasted_iota(jnp.int32, (TT_FRONT, LANES), 1) < SW_HEAD_DIM

    def norm(src_ref):
        hn_ref[...] = _rms(src_ref[...], g_ref[...]).astype(BF16)

    def gates():
        hi = _dot(hn_ref[...], wg_ref[...])
        gates_ref[...] = hi[:, :LANES] + hi[:, LANES:]

    def dup_heads(val):
        out = []
        for j in range(val.shape[1] // LANES):
            vj = val[:, j * LANES:(j + 1) * LANES]
            rj = pltpu.roll(vj, SW_HEAD_DIM, axis=1)
            out += [jnp.where(lo_half, vj, rj), jnp.where(lo_half, rj, vj)]
        return jnp.concatenate(out, axis=1)

    def piece(c0):
        def run():
            val = _dot(hn_ref[...], w_ref[:, c0:c0 + PROJ_PIECE])
            if c0 == WCOL_KSW:
                slab_ref[:, COL_KSW:COL_VSW] = dup_heads(val).astype(BF16)
            elif c0 == WCOL_VSW:
                slab_ref[:, COL_VSW:N_SLAB] = dup_heads(val).astype(BF16)
            elif c0 < WCOL_KSW:
                slab_ref[:, c0:c0 + PROJ_PIECE] = val.astype(BF16)
            else:
                gab_ref[:, c0 - WCOL_GA:c0 - WCOL_GA + PROJ_PIECE] = val.astype(BF16)
        return run

    return norm, [gates] + [piece(c0) for c0 in range(0, N_W, PROJ_PIECE)]


def _mlstm_steps(slab_ref, gates_ref, qk_tail_ref, first, convw_ref, bif_ref, gn_ref, y_ref,
                 c_ref, n_ref, m_ref, n_chunks):
    L = ML_CHUNK
    H = BF16_SUBLANES
    taps = ML_CONV - 1
    row = lax.broadcasted_iota(jnp.int32, (L, L), 0)
    col = lax.broadcasted_iota(jnp.int32, (L, L), 1)
    causal = row >= col
    tril = jnp.where(causal, 1.0, 0.0).astype(BF16)
    cw = convw_ref[...]
    bif = bif_ref[...]
    row8 = lax.broadcasted_iota(jnp.int32, (8, 2 * ML_QK_W), 0)

    def chunk_prep(c):
        r0 = c * L
        x_cur = slab_ref[r0:r0 + L, COL_QK:COL_QK + 2 * ML_QK_W].astype(F32)
        if c == 0:
            halo = jnp.where(first, 0.0, qk_tail_ref[...].astype(F32))[H - 8:, :]
        else:
            halo = slab_ref[r0 - H:r0, COL_QK:COL_QK + 2 * ML_QK_W].astype(F32)[H - 8:, :]
        acc = cw[taps:taps + 1, :] * x_cur
        for k in range(1, taps + 1):
            rolled = pltpu.roll(x_cur, k, axis=0)
            top = jnp.where(row8 < k, pltpu.roll(halo, k, axis=0), rolled[:8, :])
            acc = acc + cw[taps - k:taps - k + 1, :] * jnp.concatenate([top, rolled[8:, :]], axis=0)
        act = acc * jax.nn.sigmoid(acc)
        gates = gates_ref[r0:r0 + L, :] + bif
        logf = jnp.minimum(gates, 0.0) - jnp.log1p(jnp.exp(-jnp.abs(gates)))
        bsplit = _dot(tril, jnp.concatenate(_split_bf16(logf, 3), axis=1))
        bcum = bsplit[:, :LANES] + bsplit[:, LANES:2 * LANES] + bsplit[:, 2 * LANES:]
        return dict(q=act[:, :ML_QK_W] * (ML_DQK ** -0.5), k=act[:, ML_QK_W:], gates=gates, bcum=bcum,
                    gates_t=gates.T, bcum_t=bcum.T)

    def qk_part(c, h, pre):
        qf = pre["q"][:, h * ML_DQK:(h + 1) * ML_DQK]
        kf = pre["k"][:, h * ML_DQK:(h + 1) * ML_DQK]
        q = qf.astype(BF16)
        kt = kf.T.astype(BF16)
        return dict(q=q, qf=qf, kf=kf, kt=kt, qk=_dot(q, kt))

    def local_part(c, h, pre, qk):
        r0 = c * L
        fl = ML_HEADS + h
        b_c = pre["bcum"][:, fl:fl + 1]
        i_c = pre["gates"][:, h:h + 1]
        r_r = pre["gates_t"][h:h + 1, :] - pre["bcum_t"][fl:fl + 1, :]
        v = slab_ref[r0:r0 + L, COL_V + h * ML_DV:COL_V + (h + 1) * ML_DV]

        log_d = jnp.where(causal, b_c + r_r, -jnp.inf)
        a = jnp.max(log_d, axis=1, keepdims=True)
        s_loc = qk["qk"] * jnp.exp(log_d - a)
        b_last = pre["bcum"][L - 1:L, fl:fl + 1]
        log_w = b_last - b_c + i_c
        aw = jnp.max(log_w, axis=0, keepdims=True)
        w_loc = jnp.exp(log_w - aw)
        return dict(q=qk["q"], qf=qk["qf"], b_c=b_c, a=a, b_last=b_last, aw=aw,
                    sv=_dot(s_loc.astype(BF16), v), rs=jnp.sum(s_loc, axis=1, keepdims=True),
                    u=_dot(qk["kt"], (w_loc * v.astype(F32)).astype(BF16)),
                    nu=jnp.sum(w_loc * qk["kf"], axis=0, keepdims=True))

    def carried_part(c, h, loc, state):
        r0 = c * L
        ct, n_prev, m_prev = state
        inter = loc["b_c"] + m_prev
        m_t = jnp.maximum(inter, loc["a"])
        w_inter = jnp.exp(inter - m_t)
        f_loc = jnp.exp(loc["a"] - m_t)
        num = w_inter * _dot(loc["q"], ct.astype(BF16)) + f_loc * loc["sv"]
        den = w_inter * jnp.sum(loc["qf"] * n_prev, axis=1, keepdims=True) + f_loc * loc["rs"]
        rden = 1.0 / jnp.maximum(jnp.abs(den), jnp.exp(-m_t))
        ms = jnp.mean(num * num, axis=1, keepdims=True)
        scale = rden * lax.rsqrt(rden * rden * ms + EPS)
        og = jax.nn.sigmoid(slab_ref[r0:r0 + L, COL_O + h * ML_DV:COL_O + (h + 1) * ML_DV].astype(F32))
        y_ref[r0:r0 + L, h * ML_DV:(h + 1) * ML_DV] = (
            og * (num * scale * gn_ref[:, h * ML_DV:(h + 1) * ML_DV])).astype(BF16)
        m_new = jnp.maximum(loc["b_last"] + m_prev, loc["aw"])
        decay = jnp.exp(loc["b_last"] + m_prev - m_new)
        g_loc = jnp.exp(loc["aw"] - m_new)
        return decay * ct + g_loc * loc["u"], decay * n_prev + g_loc * loc["nu"], m_new

    units = [(c, h) for c in range(n_chunks) for h in range(ML_HEADS)]
    n_units = len(units)
    state, pre, qk, loc = {}, {}, {}, {}
    n_ticks = n_units + 2 * ML_SKEW

    def step(i):
        def run():
            if i == 0:
                for h in range(ML_HEADS):
                    state[h] = (c_ref[h], n_ref[h], m_ref[h])
                pre[0] = chunk_prep(0)
            if i + 1 < n_units and units[i + 1][1] == 0:
                pre[units[i + 1][0]] = chunk_prep(units[i + 1][0])
            if i < n_units:
                c, h = units[i]
                qk[i] = qk_part(c, h, pre[c])
            j = i - ML_SKEW
            if 0 <= j < n_units:
                c, h = units[j]
                loc[j] = local_part(c, h, pre[c], qk.pop(j))
            j = i - 2 * ML_SKEW
            if 0 <= j < n_units:
                c, h = units[j]
                state[h] = carried_part(c, h, loc.pop(j), state[h])
            if i == n_ticks - 1:
                for h in range(ML_HEADS):
                    c_ref[h], n_ref[h], m_ref[h] = state[h]
        return run

    return [step(i) for i in range(n_ticks)]


def _swa_steps(slab_ref, kv_tail_ref, first, sinks_ref, y_ref, n_blocks):
    W = SW_WINDOW
    hd = SW_HEAD_DIM
    qi = lax.broadcasted_iota(jnp.int32, (W, 2 * W), 0)
    key = lax.broadcasted_iota(jnp.int32, (W, 2 * W), 1) % W
    from_prev = key > qi
    lo_half = lax.broadcasted_iota(jnp.int32, (W, LANES), 1) < hd
    zero = jnp.zeros((), BF16)
    prev_bias = jnp.where(first, -jnp.inf, 0.0).astype(F32)

    def block_diag(d):
        return jnp.concatenate([jnp.where(lo_half, d, zero), jnp.where(lo_half, zero, d)], axis=0)

    cache = {}

    def kv_block(j, col0, h):
        if (j, col0, h) not in cache:
            if j < 0:
                d = kv_tail_ref[:, col0 - COL_KSW + h * LANES:col0 - COL_KSW + (h + 1) * LANES]
            else:
                d = slab_ref[j * W:(j + 1) * W, col0 + h * LANES:col0 + (h + 1) * LANES]
            cache[j, col0, h] = block_diag(d)
        return cache[j, col0, h]

    units = [(j, h, pair) for j in range(n_blocks) for h in range(SW_KV_HEADS)
             for pair in range(SW_GROUP // 2)]

    def logits_of(j, h, pair):
        c0 = COL_QSW + (h * SW_GROUP + 2 * pair) * hd
        qp = slab_ref[j * W:(j + 1) * W, c0:c0 + LANES] * jnp.asarray(hd ** -0.5, BF16)
        s_prev = _dot_nt(qp, kv_block(j - 1, COL_KSW, h))
        if j == 0:
            s_prev = s_prev + prev_bias
        return jnp.where(from_prev, s_prev, _dot_nt(qp, kv_block(j, COL_KSW, h)))

    def probs_of(lg, h, pair):
        halves = []
        for e in range(2):
            sink = sinks_ref[h * SW_GROUP + 2 * pair + e]
            le = lg[:, e * W:(e + 1) * W]
            m = jnp.maximum(jnp.max(le, axis=1, keepdims=True), sink)
            p = jnp.exp(le - m)
            denom = jnp.sum(p, axis=1, keepdims=True) + jnp.exp(sink - m)
            halves.append((p * (1.0 / denom)).astype(BF16))
        return jnp.concatenate(halves, axis=1)

    def write_out(pb, j, h, pair):
        c0 = (h * SW_GROUP + 2 * pair) * hd
        out = (_dot(jnp.where(from_prev, pb, zero), kv_block(j - 1, COL_VSW, h))
               + _dot(jnp.where(from_prev, zero, pb), kv_block(j, COL_VSW, h)))
        y_ref[j * W:(j + 1) * W, c0:c0 + LANES] = out.astype(BF16)

    logits, probs = {}, {}
    n = len(units)

    def step(i):
        def run():
            if i < n:
                logits[i] = logits_of(*units[i])
            if 0 <= i - SW_SKEW < n:
                _, h, pair = units[i - SW_SKEW]
                probs[i - SW_SKEW] = probs_of(logits.pop(i - SW_SKEW), h, pair)
            if 0 <= i - 2 * SW_SKEW < n:
                write_out(probs.pop(i - 2 * SW_SKEW), *units[i - 2 * SW_SKEW])
        return run

    return [step(i) for i in range(n + 2 * SW_SKEW)]


def _front_kernel(sinks_ref, x_ref, g_ref, w_ref, wg_ref, convw_ref, bif_ref, gn_ref,
                  ya_ref, yb_ref, gab_ref,
                  hn_ref, slab_new_ref, slab_ref, gates_new_ref, gates_ref, kv_tail_ref, qk_tail_ref,
                  c_ref, n_ref, m_ref, *, tiles_per_seq):
    tt = TT_FRONT
    s = pl.program_id(0)
    first = (jnp.maximum(s - 1, 0) % tiles_per_seq) == 0

    @pl.when(s == 0)
    def _():
        slab_ref[...] = jnp.zeros_like(slab_ref)
        gates_ref[...] = jnp.zeros_like(gates_ref)
        kv_tail_ref[...] = jnp.zeros_like(kv_tail_ref)
        qk_tail_ref[...] = jnp.zeros_like(qk_tail_ref)

    @pl.when(first)
    def _():
        c_ref[...] = jnp.zeros_like(c_ref)
        n_ref[...] = jnp.zeros_like(n_ref)
        m_ref[...] = jnp.zeros_like(m_ref)

    norm, pieces = _projection_steps(g_ref, w_ref, wg_ref, hn_ref, slab_new_ref, gates_new_ref, gab_ref)
    ml = _mlstm_steps(slab_ref, gates_ref, qk_tail_ref, first, convw_ref, bif_ref, gn_ref, ya_ref,
                      c_ref, n_ref, m_ref, tt // ML_CHUNK)
    sw = _swa_steps(slab_ref, kv_tail_ref, first, sinks_ref, yb_ref, tt // SW_WINDOW)
    norm(x_ref)
    for thunk in _interleave(pieces, ml, sw):
        thunk()

    kv_tail_ref[...] = slab_ref[tt - SW_WINDOW:tt, COL_KSW:N_SLAB]
    qk_tail_ref[...] = slab_ref[tt - BF16_SUBLANES:tt, COL_QK:COL_QK + 2 * ML_QK_W]
    slab_ref[...] = slab_new_ref[...]
    gates_ref[...] = gates_new_ref[...]


def _front(x2d, sinks, g, w_main, wg, conv_w, b_if, gn, seq):
    t = x2d.shape[0]
    tt = TT_FRONT
    nt = t // tt
    W = SW_WINDOW

    def const(shape):
        return pl.BlockSpec(shape, lambda s: (0, 0), pipeline_mode=pl.Buffered(1))

    proj_tile = lambda s: (jnp.minimum(s, nt - 1), 0)
    mix_tile = lambda s: (jnp.maximum(s - 1, 0), 0)
    return pl.pallas_call(
        functools.partial(_front_kernel, tiles_per_seq=seq // tt),
        out_shape=(jax.ShapeDtypeStruct((t, ML_V_W), BF16),
                   jax.ShapeDtypeStruct((t, SW_Q_W), BF16),
                   jax.ShapeDtypeStruct((t, N_GATES), BF16)),
        grid=(nt + 1,),
        in_specs=[
            pl.BlockSpec(memory_space=pltpu.SMEM),
            pl.BlockSpec((tt, D_MODEL), proj_tile),
            const((1, D_MODEL)),
            const((D_MODEL, N_W)),
            const((D_MODEL, 2 * LANES)),
            const((ML_CONV, 2 * ML_QK_W)),
            const((1, LANES)),
            const((1, ML_V_W)),
        ],
        out_specs=(pl.BlockSpec((tt, ML_V_W), mix_tile),
                   pl.BlockSpec((tt, SW_Q_W), mix_tile),
                   pl.BlockSpec((tt, N_GATES), proj_tile)),
        scratch_shapes=[
            pltpu.VMEM((tt, D_MODEL), BF16),
            pltpu.VMEM((tt, N_SLAB), BF16),
            pltpu.VMEM((tt, N_SLAB), BF16),
            pltpu.VMEM((tt, LANES), F32),
            pltpu.VMEM((tt, LANES), F32),
            pltpu.VMEM((W, N_SLAB - COL_KSW), BF16),
            pltpu.VMEM((BF16_SUBLANES, 2 * ML_QK_W), BF16),
            pltpu.VMEM((ML_HEADS, ML_DQK, ML_DV), F32),
            pltpu.VMEM((ML_HEADS, 1, ML_DQK), F32),
            pltpu.VMEM((ML_HEADS, 1, 1), F32),
        ],
        compiler_params=pltpu.CompilerParams(
            dimension_semantics=("arbitrary",),
            vmem_limit_bytes=_vmem_limit(
                ((tt, D_MODEL), F32, 2), ((D_MODEL, N_W), BF16, 1), ((D_MODEL, 2 * LANES), BF16, 1),
                ((tt, ML_V_W), BF16, 2), ((tt, SW_Q_W), BF16, 2), ((tt, N_GATES), BF16, 2),
                ((tt, D_MODEL), BF16, 1), ((tt, N_SLAB), BF16, 2), ((tt, LANES), F32, 2),
                ((W, N_SLAB - COL_KSW), BF16, 1), ((ML_HEADS, ML_DQK, ML_DV), F32, 1))),
        name="front",
    )(sinks, x2d, g, w_main, wg, conv_w, b_if, gn)


def _merge_kernel(x_ref, ya_ref, yb_ref, ga_ref, gb_ref, wa_ref, wb_ref, wo_ref, out_ref):
    groups = [slice(r, r + ROW_GROUP) for r in range(0, TM_MERGE, ROW_GROUP)]
    merged = {}

    def branches(rows):
        a = jax.nn.sigmoid(ga_ref[rows, :].astype(F32)) * _dot(ya_ref[rows, :], wa_ref[...])
        b = jax.nn.sigmoid(gb_ref[rows, :].astype(F32)) * _dot(yb_ref[rows, :], wb_ref[...])
        merged[rows.start] = (a + b).astype(BF16)

    def project(rows):
        out_ref[rows, :] = x_ref[rows, :] + _dot(merged.pop(rows.start), wo_ref[...])

    for i in range(len(groups) + 1):
        if i < len(groups):
            branches(groups[i])
        if i >= 1:
            project(groups[i - 1])


def _merge(x2d, ya, yb, gab, wa, wb, wo):
    t = x2d.shape[0]
    tm = TM_MERGE
    wspec = pl.BlockSpec((D_MODEL, D_MODEL), lambda i: (0, 0), pipeline_mode=pl.Buffered(1))
    return pl.pallas_call(
        _merge_kernel,
        out_shape=jax.ShapeDtypeStruct((t, D_MODEL), F32),
        grid=(t // tm,),
        in_specs=[
            pl.BlockSpec((tm, D_MODEL), lambda i: (i, 0)),
            pl.BlockSpec((tm, D_MODEL), lambda i: (i, 0)),
            pl.BlockSpec((tm, D_MODEL), lambda i: (i, 0)),
            pl.BlockSpec((tm, D_MODEL), lambda i: (i, 0)),
            pl.BlockSpec((tm, D_MODEL), lambda i: (i, 1)),
            wspec, wspec, wspec,
        ],
        out_specs=pl.BlockSpec((tm, D_MODEL), lambda i: (i, 0)),
        compiler_params=pltpu.CompilerParams(
            dimension_semantics=("parallel",),
            vmem_limit_bytes=_vmem_limit(((tm, D_MODEL), F32, 4), ((tm, D_MODEL), BF16, 8),
                                         ((D_MODEL, D_MODEL), BF16, 3))),
        name="merge_out",
    )(x2d, ya, yb, gab, gab, wa, wb, wo)


def _mlp_ple_kernel(x_ref, p_ref, gm_ref, wup_ref, wdn_ref, gp_ref, wg_ref, wp_ref, gf_ref, out_ref,
                    *, final_norm):
    groups = [slice(r, r + ROW_GROUP) for r in range(0, TM_MLP, ROW_GROUP)]
    n_ff = D_FF // FF_CHUNK
    hn, acc = {}, {}

    def stage(g, k):
        rows = groups[g]
        if k == 0:
            x = x_ref[rows, :]
            hn[g] = _rms(x, gm_ref[...]).astype(BF16)
            acc[g] = x
        elif k <= n_ff:
            c = k - 1
            u = _dot(hn[g], wup_ref[:, c * FF_CHUNK:(c + 1) * FF_CHUNK])
            r = jnp.maximum(u, 0.0)
            acc[g] = acc[g] + _dot((r * r).astype(BF16), wdn_ref[c * FF_CHUNK:(c + 1) * FF_CHUNK, :])
        else:
            x = acc.pop(g)
            gate = jax.nn.sigmoid(_dot(_rms(x, gp_ref[...]).astype(BF16), wg_ref[...]))
            x = x + gate * _dot(p_ref[rows, :].astype(BF16), wp_ref[...])
            if final_norm:
                x = _rms(x, gf_ref[...])
            out_ref[rows, :] = x

    n_stages = n_ff + 2
    for tick in range(n_stages + MLP_STAGE_LAG * (len(groups) - 1)):
        for g in reversed(range(len(groups))):
            k = tick - MLP_STAGE_LAG * g
            if 0 <= k < n_stages:
                stage(g, k)


def _mlp_ple(x2d, p2d, gm, wup, wdn, gp, wg, wp, gf, final_norm):
    t = x2d.shape[0]
    tm = TM_MLP

    def const(shape):
        return pl.BlockSpec(shape, lambda i: (0, 0), pipeline_mode=pl.Buffered(1))

    return pl.pallas_call(
        functools.partial(_mlp_ple_kernel, final_norm=final_norm),
        out_shape=jax.ShapeDtypeStruct((t, D_MODEL), F32),
        grid=(t // tm,),
        in_specs=[
            pl.BlockSpec((tm, D_MODEL), lambda i: (i, 0)),
            pl.BlockSpec((tm, PLE_DIM), lambda i: (i, 0)),
            const((1, D_MODEL)),
            const((D_MODEL, D_FF)),
            const((D_FF, D_MODEL)),
            const((1, D_MODEL)),
            const((D_MODEL, D_MODEL)),
            const((PLE_DIM, D_MODEL)),
            const((1, D_MODEL)),
        ],
        out_specs=pl.BlockSpec((tm, D_MODEL), lambda i: (i, 0)),
        compiler_params=pltpu.CompilerParams(
            dimension_semantics=("parallel",),
            vmem_limit_bytes=_vmem_limit(((tm, D_MODEL), F32, 4), ((tm, PLE_DIM), F32, 2),
                                         ((D_MODEL, D_FF), BF16, 2), ((D_MODEL, D_MODEL), BF16, 1),
                                         ((PLE_DIM, D_MODEL), BF16, 1))),
        name="mlp_ple",
    )(x2d, p2d, gm, wup, wdn, gp, wg, wp, gf)


def _prep_w_in_kernel(wa_ref, wb_ref, wif_ref, main_ref, wg_ref):
    i = pl.program_id(0)
    n_a = COL_QSW // PREP_COLS

    @pl.when(i < n_a)
    def _():
        main_ref[...] = wa_ref[...].T.astype(BF16)

    @pl.when(i >= n_a)
    def _():
        main_ref[...] = wb_ref[...].T.astype(BF16)

    @pl.when(i == 0)
    def _():
        lane = lax.broadcasted_iota(jnp.int32, (D_MODEL, LANES), 1)
        wg = jnp.where(lane < 2 * ML_HEADS, wif_ref[...].T, 0.0)
        wg_hi, wg_lo = _split_bf16(wg, 2)
        wg_ref[:, :LANES] = wg_hi
        wg_ref[:, LANES:] = wg_lo


def _prep_w_in(w_in):
    n_if = 2 * ML_HEADS
    cols = PREP_COLS
    n_a = COL_QSW // cols
    w_t = jnp.swapaxes(w_in, 0, 1)
    return pl.pallas_call(
        _prep_w_in_kernel,
        out_shape=(jax.ShapeDtypeStruct((D_MODEL, N_W), BF16),
                   jax.ShapeDtypeStruct((D_MODEL, 2 * LANES), BF16)),
        grid=(N_W // cols,),
        in_specs=[
            pl.BlockSpec((cols, D_MODEL), lambda i: (jnp.minimum(i, n_a - 1), 0)),
            pl.BlockSpec((pl.Element(cols), pl.Element(D_MODEL)),
                         lambda i: (pl.multiple_of(COL_QSW + n_if + jnp.maximum(i - n_a, 0) * cols, n_if), 0)),
            pl.BlockSpec((LANES, D_MODEL), lambda i: (COL_QSW // LANES, 0)),
        ],
        out_specs=(pl.BlockSpec((D_MODEL, cols), lambda i: (0, i)),
                   pl.BlockSpec((D_MODEL, 2 * LANES), lambda i: (0, 0))),
        compiler_params=pltpu.CompilerParams(
            dimension_semantics=("arbitrary",),
            vmem_limit_bytes=_vmem_limit(((cols, D_MODEL), F32, 4), ((LANES, D_MODEL), F32, 2),
                                         ((D_MODEL, cols), BF16, 2), ((D_MODEL, 2 * LANES), BF16, 2))),
        name="prep_w_in",
    )(w_t, w_t, w_t)


def kernel(x, p, norm_mix_g, w_in, conv_qk, b_if, mlstm_norm_g, sinks, w_branch_a, w_branch_b, w_out,
           norm_mlp_g, w_up, w_down, norm_ple_g, w_ple_gate, w_ple_proj, final_norm_g):
    batch, seq, _ = x.shape
    depth = w_in.shape[0]
    t = batch * seq
    x2d = x.reshape(t, D_MODEL)
    row = lambda v: v.reshape(1, -1)
    for i in range(depth):
        w_main, wg = _prep_w_in(w_in[i])
        bif = jnp.pad(b_if[i], (0, LANES - 2 * ML_HEADS)).reshape(1, LANES)
        ya, yb, gab = _front(x2d, sinks[i], row(norm_mix_g[i]), w_main, wg, conv_qk[i], bif,
                             row(mlstm_norm_g[i]), seq)
        x2d = _merge(x2d, ya, yb, gab, w_branch_a[i].astype(BF16), w_branch_b[i].astype(BF16),
                     w_out[i].astype(BF16))
        x2d = _mlp_ple(x2d, p[i].reshape(t, PLE_DIM), row(norm_mlp_g[i]), w_up[i].astype(BF16),
                       w_down[i].astype(BF16), row(norm_ple_g[i]), w_ple_gate[i].astype(BF16),
                       w_ple_proj[i].astype(BF16), row(final_norm_g), final_norm=(i == depth - 1))
    return x2d.reshape(batch, seq, D_MODEL)
```

```python
import functools
import math

import jax
import jax.numpy as jnp
from jax import lax
from jax.experimental import pallas as pl
from jax.experimental.pallas import tpu as pltpu

D_MODEL = 1024
PLE_DIM = 256
ML_HEADS = 4
ML_DQK = 128
ML_DV = 256
ML_CONV = 4
SW_Q_HEADS = 16
SW_KV_HEADS = 4
SW_HEAD_DIM = 64
SW_WINDOW = 128
SW_GROUP = SW_Q_HEADS // SW_KV_HEADS
D_FF = 4 * D_MODEL
EPS = 1e-6

ML_QK_W = ML_HEADS * ML_DQK
ML_V_W = ML_HEADS * ML_DV
SW_Q_W = SW_Q_HEADS * SW_HEAD_DIM
SW_KV_W = SW_KV_HEADS * SW_HEAD_DIM

LANES = 128
BF16_SUBLANES = 16
MXU_N = 256
VMEM_BYTES = 64 * 1024 * 1024
VMEM_COMPILER_SHARE = 16 * 1024 * 1024

COL_QK = 0
COL_V = COL_QK + 2 * ML_QK_W
COL_O = COL_V + ML_V_W
COL_QSW = COL_O + ML_V_W
COL_KSW = COL_QSW + SW_Q_W
COL_VSW = COL_KSW + 2 * SW_KV_W
N_SLAB = COL_VSW + SW_KV_W
N_GATES = 2 * D_MODEL
WCOL_KSW = COL_KSW
WCOL_VSW = WCOL_KSW + SW_KV_W
WCOL_GA = WCOL_VSW + SW_KV_W
N_W = WCOL_GA + N_GATES

ML_CHUNK = 128
ML_SKEW = 1
SW_SKEW = 2
TT_FRONT = 512
PROJ_PIECE = MXU_N
TM_MERGE = 1024
TM_MLP = 1024
ROW_GROUP = 512
MLP_STAGE_LAG = 2
FF_CHUNK = 1024
PREP_COLS = 512

F32 = jnp.float32
BF16 = jnp.bfloat16


def _vmem_limit(*buffers):
    need = sum(copies * math.prod(shape) * jnp.dtype(dtype).itemsize for shape, dtype, copies in buffers)
    assert need + VMEM_COMPILER_SHARE <= VMEM_BYTES, need
    return need + VMEM_COMPILER_SHARE


def _dot(a, b):
    return jnp.dot(a, b, preferred_element_type=F32)


def _dot_nt(a, b):
    return lax.dot_general(a, b, (((1,), (1,)), ((), ())), preferred_element_type=F32)


def _rms(x, g):
    return x * lax.rsqrt(jnp.mean(x * x, axis=-1, keepdims=True) + EPS) * g


def _split_bf16(x, parts):
    out = []
    r = x
    for _ in range(parts - 1):
        t = r.astype(BF16)
        out.append(t)
        r = r - t.astype(F32)
    out.append(r.astype(BF16))
    return out


def _interleave(*lists):
    tagged = []
    for k, lst in enumerate(lists):
        tagged += [((i + 0.5) / len(lst), k, i, f) for i, f in enumerate(lst)]
    return [f for _, _, _, f in sorted(tagged, key=lambda e: e[:3])]


def _projection_steps(g_ref, w_ref, wg_ref, hn_ref, slab_ref, gates_ref, gab_ref):
    lo_half = lax.broadcasted_iota(jnp.int32, (TT_FRONT, LANES), 1) < SW_HEAD_DIM

    def norm(src_ref):
        hn_ref[...] = _rms(src_ref[...], g_ref[...]).astype(BF16)

    def gates():
        hi = _dot(hn_ref[...], wg_ref[...])
        gates_ref[...] = hi[:, :LANES] + hi[:, LANES:]

    def dup_heads(val):
        out = []
        for j in range(val.shape[1] // LANES):
            vj = val[:, j * LANES:(j + 1) * LANES]
            rj = pltpu.roll(vj, SW_HEAD_DIM, axis=1)
            out += [jnp.where(lo_half, vj, rj), jnp.where(lo_half, rj, vj)]
        return jnp.concatenate(out, axis=1)

    def piece(c0):
        def run():
            val = _dot(hn_ref[...], w_ref[:, c0:c0 + PROJ_PIECE])
            if c0 == WCOL_KSW:
                slab_ref[:, COL_KSW:COL_VSW] = dup_heads(val).astype(BF16)
            elif c0 == WCOL_VSW:
                slab_ref[:, COL_VSW:N_SLAB] = val.astype(BF16)
            elif c0 < WCOL_KSW:
                slab_ref[:, c0:c0 + PROJ_PIECE] = val.astype(BF16)
            else:
                gab_ref[:, c0 - WCOL_GA:c0 - WCOL_GA + PROJ_PIECE] = val.astype(BF16)
        return run

    return norm, [gates] + [piece(c0) for c0 in range(0, N_W, PROJ_PIECE)]


def _mlstm_steps(slab_ref, gates_ref, qk_tail_ref, first, convw_ref, bif_ref, gn_ref, y_ref,
                 c_ref, n_ref, m_ref, n_chunks):
    L = ML_CHUNK
    H = BF16_SUBLANES
    taps = ML_CONV - 1
    row = lax.broadcasted_iota(jnp.int32, (L, L), 0)
    col = lax.broadcasted_iota(jnp.int32, (L, L), 1)
    causal = row >= col
    tril = jnp.where(causal, 1.0, 0.0).astype(BF16)
    cw = convw_ref[...]
    bif = bif_ref[...]
    row8 = lax.broadcasted_iota(jnp.int32, (8, 2 * ML_QK_W), 0)

    def chunk_prep(c):
        r0 = c * L
        x_cur = slab_ref[r0:r0 + L, COL_QK:COL_QK + 2 * ML_QK_W].astype(F32)
        if c == 0:
            halo = jnp.where(first, 0.0, qk_tail_ref[...].astype(F32))[H - 8:, :]
        else:
            halo = slab_ref[r0 - H:r0, COL_QK:COL_QK + 2 * ML_QK_W].astype(F32)[H - 8:, :]
        acc = cw[taps:taps + 1, :] * x_cur
        for k in range(1, taps + 1):
            rolled = pltpu.roll(x_cur, k, axis=0)
            top = jnp.where(row8 < k, pltpu.roll(halo, k, axis=0), rolled[:8, :])
            acc = acc + cw[taps - k:taps - k + 1, :] * jnp.concatenate([top, rolled[8:, :]], axis=0)
        act = acc * jax.nn.sigmoid(acc)
        gates = gates_ref[r0:r0 + L, :] + bif
        logf = jnp.minimum(gates, 0.0) - jnp.log1p(jnp.exp(-jnp.abs(gates)))
        bsplit = _dot(tril, jnp.concatenate(_split_bf16(logf, 3), axis=1))
        bcum = bsplit[:, :LANES] + bsplit[:, LANES:2 * LANES] + bsplit[:, 2 * LANES:]
        return dict(q=act[:, :ML_QK_W] * (ML_DQK ** -0.5), k=act[:, ML_QK_W:], gates=gates, bcum=bcum,
                    gates_t=gates.T, bcum_t=bcum.T)

    def qk_part(c, h, pre):
        qf = pre["q"][:, h * ML_DQK:(h + 1) * ML_DQK]
        kf = pre["k"][:, h * ML_DQK:(h + 1) * ML_DQK]
        q = qf.astype(BF16)
        kt = kf.T.astype(BF16)
        return dict(q=q, qf=qf, kf=kf, kt=kt, qk=_dot(q, kt))

    def local_part(c, h, pre, qk):
        r0 = c * L
        fl = ML_HEADS + h
        b_c = pre["bcum"][:, fl:fl + 1]
        i_c = pre["gates"][:, h:h + 1]
        r_r = pre["gates_t"][h:h + 1, :] - pre["bcum_t"][fl:fl + 1, :]
        v = slab_ref[r0:r0 + L, COL_V + h * ML_DV:COL_V + (h + 1) * ML_DV]

        log_d = jnp.where(causal, b_c + r_r, -jnp.inf)
        a = jnp.max(log_d, axis=1, keepdims=True)
        s_loc = qk["qk"] * jnp.exp(log_d - a)
        b_last = pre["bcum"][L - 1:L, fl:fl + 1]
        log_w = b_last - b_c + i_c
        aw = jnp.max(log_w, axis=0, keepdims=True)
        w_loc = jnp.exp(log_w - aw)
        return dict(q=qk["q"], qf=qk["qf"], b_c=b_c, a=a, b_last=b_last, aw=aw,
                    sv=_dot(s_loc.astype(BF16), v), rs=jnp.sum(s_loc, axis=1, keepdims=True),
                    u=_dot(qk["kt"], (w_loc * v.astype(F32)).astype(BF16)),
                    nu=jnp.sum(w_loc * qk["kf"], axis=0, keepdims=True))

    def carried_part(c, h, loc, state):
        r0 = c * L
        ct, n_prev, m_prev = state
        inter = loc["b_c"] + m_prev
        m_t = jnp.maximum(inter, loc["a"])
        w_inter = jnp.exp(inter - m_t)
        f_loc = jnp.exp(loc["a"] - m_t)
        num = w_inter * _dot(loc["q"], ct.astype(BF16)) + f_loc * loc["sv"]
        den = w_inter * jnp.sum(loc["qf"] * n_prev, axis=1, keepdims=True) + f_loc * loc["rs"]
        rden = 1.0 / jnp.maximum(jnp.abs(den), jnp.exp(-m_t))
        ms = jnp.mean(num * num, axis=1, keepdims=True)
        scale = rden * lax.rsqrt(rden * rden * ms + EPS)
        og = jax.nn.sigmoid(slab_ref[r0:r0 + L, COL_O + h * ML_DV:COL_O + (h + 1) * ML_DV].astype(F32))
        y_ref[r0:r0 + L, h * ML_DV:(h + 1) * ML_DV] = (
            og * (num * scale * gn_ref[:, h * ML_DV:(h + 1) * ML_DV])).astype(BF16)
        m_new = jnp.maximum(loc["b_last"] + m_prev, loc["aw"])
        decay = jnp.exp(loc["b_last"] + m_prev - m_new)
        g_loc = jnp.exp(loc["aw"] - m_new)
        return decay * ct + g_loc * loc["u"], decay * n_prev + g_loc * loc["nu"], m_new

    units = [(c, h) for c in range(n_chunks) for h in range(ML_HEADS)]
    n_units = len(units)
    state, pre, qk, loc = {}, {}, {}, {}
    n_ticks = n_units + 2 * ML_SKEW

    def step(i):
        def run():
            if i == 0:
                for h in range(ML_HEADS):
                    state[h] = (c_ref[h], n_ref[h], m_ref[h])
                pre[0] = chunk_prep(0)
            if i + 1 < n_units and units[i + 1][1] == 0:
                pre[units[i + 1][0]] = chunk_prep(units[i + 1][0])
            if i < n_units:
                c, h = units[i]
                qk[i] = qk_part(c, h, pre[c])
            j = i - ML_SKEW
            if 0 <= j < n_units:
                c, h = units[j]
                loc[j] = local_part(c, h, pre[c], qk.pop(j))
            j = i - 2 * ML_SKEW
            if 0 <= j < n_units:
                c, h = units[j]
                state[h] = carried_part(c, h, loc.pop(j), state[h])
            if i == n_ticks - 1:
                for h in range(ML_HEADS):
                    c_ref[h], n_ref[h], m_ref[h] = state[h]
        return run

    return [step(i) for i in range(n_ticks)]


def _swa_steps(slab_ref, kv_tail_ref, first, sinks_ref, y_ref, n_blocks):
    W = SW_WINDOW
    hd = SW_HEAD_DIM
    qi = lax.broadcasted_iota(jnp.int32, (W, 2 * W), 0)
    key = lax.broadcasted_iota(jnp.int32, (W, 2 * W), 1) % W
    from_prev = key > qi
    lo_half = lax.broadcasted_iota(jnp.int32, (W, LANES), 1) < hd
    zero = jnp.zeros((), BF16)
    prev_bias = jnp.where(first, -jnp.inf, 0.0).astype(F32)

    def block_diag(d):
        return jnp.concatenate([jnp.where(lo_half, d, zero), jnp.where(lo_half, zero, d)], axis=0)

    k_cache, vt_cache = {}, {}

    def rows_of(j, col0, width):
        if j < 0:
            return kv_tail_ref[:, col0 - COL_KSW:col0 - COL_KSW + width]
        return slab_ref[j * W:(j + 1) * W, col0:col0 + width]

    def k_block(j, h):
        if (j, h) not in k_cache:
            k_cache[j, h] = block_diag(rows_of(j, COL_KSW + h * LANES, LANES))
        return k_cache[j, h]

    def vt_block(j, h):
        if j not in vt_cache:
            vt_cache[j] = rows_of(j, COL_VSW, SW_KV_W).astype(F32).T.astype(BF16)
        return vt_cache[j][h * hd:(h + 1) * hd, :]

    units = [(j, h, pair) for j in range(n_blocks) for h in range(SW_KV_HEADS)
             for pair in range(SW_GROUP // 2)]

    def logits_of(j, h, pair):
        c0 = COL_QSW + (h * SW_GROUP + 2 * pair) * hd
        qp = slab_ref[j * W:(j + 1) * W, c0:c0 + LANES] * jnp.asarray(hd ** -0.5, BF16)
        s_prev = _dot_nt(qp, k_block(j - 1, h))
        if j == 0:
            s_prev = s_prev + prev_bias
        return jnp.where(from_prev, s_prev, _dot_nt(qp, k_block(j, h)))

    def probs_of(lg, h, pair):
        halves = []
        for e in range(2):
            sink = sinks_ref[h * SW_GROUP + 2 * pair + e]
            le = lg[:, e * W:(e + 1) * W]
            m = jnp.maximum(jnp.max(le, axis=1, keepdims=True), sink)
            p = jnp.exp(le - m)
            denom = jnp.sum(p, axis=1, keepdims=True) + jnp.exp(sink - m)
            halves.append((p * (1.0 / denom)).astype(BF16))
        return jnp.concatenate(halves, axis=1)

    def write_out(pb, j, h, pair):
        c0 = (h * SW_GROUP + 2 * pair) * hd

        def stacked(p):
            return jnp.concatenate([p[:, :W], p[:, W:]], axis=0)

        out_t = (_dot_nt(vt_block(j - 1, h), stacked(jnp.where(from_prev, pb, zero)))
                 + _dot_nt(vt_block(j, h), stacked(jnp.where(from_prev, zero, pb))))
        out = jnp.concatenate([out_t[:, :W], out_t[:, W:]], axis=0).T
        y_ref[j * W:(j + 1) * W, c0:c0 + LANES] = out.astype(BF16)

    logits, probs = {}, {}
    n = len(units)

    def step(i):
        def run():
            if i < n:
                logits[i] = logits_of(*units[i])
            if 0 <= i - SW_SKEW < n:
                _, h, pair = units[i - SW_SKEW]
                probs[i - SW_SKEW] = probs_of(logits.pop(i - SW_SKEW), h, pair)
            if 0 <= i - 2 * SW_SKEW < n:
                write_out(probs.pop(i - 2 * SW_SKEW), *units[i - 2 * SW_SKEW])
        return run

    return [step(i) for i in range(n + 2 * SW_SKEW)]


def _front_kernel(sinks_ref, x_ref, g_ref, w_ref, wg_ref, convw_ref, bif_ref, gn_ref,
                  ya_ref, yb_ref, gab_ref,
                  hn_ref, slab_new_ref, slab_ref, gates_new_ref, gates_ref, kv_tail_ref, qk_tail_ref,
                  c_ref, n_ref, m_ref, *, tiles_per_seq):
    tt = TT_FRONT
    s = pl.program_id(0)
    first = (jnp.maximum(s - 1, 0) % tiles_per_seq) == 0

    @pl.when(s == 0)
    def _():
        slab_ref[...] = jnp.zeros_like(slab_ref)
        gates_ref[...] = jnp.zeros_like(gates_ref)
        kv_tail_ref[...] = jnp.zeros_like(kv_tail_ref)
        qk_tail_ref[...] = jnp.zeros_like(qk_tail_ref)

    @pl.when(first)
    def _():
        c_ref[...] = jnp.zeros_like(c_ref)
        n_ref[...] = jnp.zeros_like(n_ref)
        m_ref[...] = jnp.zeros_like(m_ref)

    norm, pieces = _projection_steps(g_ref, w_ref, wg_ref, hn_ref, slab_new_ref, gates_new_ref, gab_ref)
    ml = _mlstm_steps(slab_ref, gates_ref, qk_tail_ref, first, convw_ref, bif_ref, gn_ref, ya_ref,
                      c_ref, n_ref, m_ref, tt // ML_CHUNK)
    sw = _swa_steps(slab_ref, kv_tail_ref, first, sinks_ref, yb_ref, tt // SW_WINDOW)
    norm(x_ref)
    for thunk in _interleave(pieces, ml, sw):
        thunk()

    kv_tail_ref[...] = slab_ref[tt - SW_WINDOW:tt, COL_KSW:N_SLAB]
    qk_tail_ref[...] = slab_ref[tt - BF16_SUBLANES:tt, COL_QK:COL_QK + 2 * ML_QK_W]
    slab_ref[...] = slab_new_ref[...]
    gates_ref[...] = gates_new_ref[...]


def _front(x2d, sinks, g, w_main, wg, conv_w, b_if, gn, seq):
    t = x2d.shape[0]
    tt = TT_FRONT
    nt = t // tt
    W = SW_WINDOW

    def const(shape):
        return pl.BlockSpec(shape, lambda s: (0, 0), pipeline_mode=pl.Buffered(1))

    proj_tile = lambda s: (jnp.minimum(s, nt - 1), 0)
    mix_tile = lambda s: (jnp.maximum(s - 1, 0), 0)
    return pl.pallas_call(
        functools.partial(_front_kernel, tiles_per_seq=seq // tt),
        out_shape=(jax.ShapeDtypeStruct((t, ML_V_W), BF16),
                   jax.ShapeDtypeStruct((t, SW_Q_W), BF16),
                   jax.ShapeDtypeStruct((t, N_GATES), BF16)),
        grid=(nt + 1,),
        in_specs=[
            pl.BlockSpec(memory_space=pltpu.SMEM),
            pl.BlockSpec((tt, D_MODEL), proj_tile),
            const((1, D_MODEL)),
            const((D_MODEL, N_W)),
            const((D_MODEL, 2 * LANES)),
            const((ML_CONV, 2 * ML_QK_W)),
            const((1, LANES)),
            const((1, ML_V_W)),
        ],
        out_specs=(pl.BlockSpec((tt, ML_V_W), mix_tile),
                   pl.BlockSpec((tt, SW_Q_W), mix_tile),
                   pl.BlockSpec((tt, N_GATES), proj_tile)),
        scratch_shapes=[
            pltpu.VMEM((tt, D_MODEL), BF16),
            pltpu.VMEM((tt, N_SLAB), BF16),
            pltpu.VMEM((tt, N_SLAB), BF16),
            pltpu.VMEM((tt, LANES), F32),
            pltpu.VMEM((tt, LANES), F32),
            pltpu.VMEM((W, N_SLAB - COL_KSW), BF16),
            pltpu.VMEM((BF16_SUBLANES, 2 * ML_QK_W), BF16),
            pltpu.VMEM((ML_HEADS, ML_DQK, ML_DV), F32),
            pltpu.VMEM((ML_HEADS, 1, ML_DQK), F32),
            pltpu.VMEM((ML_HEADS, 1, 1), F32),
        ],
        compiler_params=pltpu.CompilerParams(
            dimension_semantics=("arbitrary",),
            vmem_limit_bytes=_vmem_limit(
                ((tt, D_MODEL), F32, 2), ((D_MODEL, N_W), BF16, 1), ((D_MODEL, 2 * LANES), BF16, 1),
                ((tt, ML_V_W), BF16, 2), ((tt, SW_Q_W), BF16, 2), ((tt, N_GATES), BF16, 2),
                ((tt, D_MODEL), BF16, 1), ((tt, N_SLAB), BF16, 2), ((tt, LANES), F32, 2),
                ((W, N_SLAB - COL_KSW), BF16, 1), ((ML_HEADS, ML_DQK, ML_DV), F32, 1))),
        name="front",
    )(sinks, x2d, g, w_main, wg, conv_w, b_if, gn)


def _merge_kernel(x_ref, ya_ref, yb_ref, ga_ref, gb_ref, wa_ref, wb_ref, wo_ref, out_ref):
    groups = [slice(r, r + ROW_GROUP) for r in range(0, TM_MERGE, ROW_GROUP)]
    merged = {}

    def branches(rows):
        a = jax.nn.sigmoid(ga_ref[rows, :].astype(F32)) * _dot(ya_ref[rows, :], wa_ref[...])
        b = jax.nn.sigmoid(gb_ref[rows, :].astype(F32)) * _dot(yb_ref[rows, :], wb_ref[...])
        merged[rows.start] = (a + b).astype(BF16)

    def project(rows):
        out_ref[rows, :] = x_ref[rows, :] + _dot(merged.pop(rows.start), wo_ref[...])

    for i in range(len(groups) + 1):
        if i < len(groups):
            branches(groups[i])
        if i >= 1:
            project(groups[i - 1])


def _merge(x2d, ya, yb, gab, wa, wb, wo):
    t = x2d.shape[0]
    tm = TM_MERGE
    wspec = pl.BlockSpec((D_MODEL, D_MODEL), lambda i: (0, 0), pipeline_mode=pl.Buffered(1))
    return pl.pallas_call(
        _merge_kernel,
        out_shape=jax.ShapeDtypeStruct((t, D_MODEL), F32),
        grid=(t // tm,),
        in_specs=[
            pl.BlockSpec((tm, D_MODEL), lambda i: (i, 0)),
            pl.BlockSpec((tm, D_MODEL), lambda i: (i, 0)),
            pl.BlockSpec((tm, D_MODEL), lambda i: (i, 0)),
            pl.BlockSpec((tm, D_MODEL), lambda i: (i, 0)),
            pl.BlockSpec((tm, D_MODEL), lambda i: (i, 1)),
            wspec, wspec, wspec,
        ],
        out_specs=pl.BlockSpec((tm, D_MODEL), lambda i: (i, 0)),
        compiler_params=pltpu.CompilerParams(
            dimension_semantics=("parallel",),
            vmem_limit_bytes=_vmem_limit(((tm, D_MODEL), F32, 4), ((tm, D_MODEL), BF16, 8),
                                         ((D_MODEL, D_MODEL), BF16, 3))),
        name="merge_out",
    )(x2d, ya, yb, gab, gab, wa, wb, wo)


def _mlp_ple_kernel(x_ref, p_ref, gm_ref, wup_ref, wdn_ref, gp_ref, wg_ref, wp_ref, gf_ref, out_ref,
                    *, final_norm):
    groups = [slice(r, r + ROW_GROUP) for r in range(0, TM_MLP, ROW_GROUP)]
    n_ff = D_FF // FF_CHUNK
    hn, acc = {}, {}

    def stage(g, k):
        rows = groups[g]
        if k == 0:
            x = x_ref[rows, :]
            hn[g] = _rms(x, gm_ref[...]).astype(BF16)
            acc[g] = x
        elif k <= n_ff:
            c = k - 1
            u = _dot(hn[g], wup_ref[:, c * FF_CHUNK:(c + 1) * FF_CHUNK])
            r = jnp.maximum(u, 0.0)
            acc[g] = acc[g] + _dot((r * r).astype(BF16), wdn_ref[c * FF_CHUNK:(c + 1) * FF_CHUNK, :])
        else:
            x = acc.pop(g)
            gate = jax.nn.sigmoid(_dot(_rms(x, gp_ref[...]).astype(BF16), wg_ref[...]))
            x = x + gate * _dot(p_ref[rows, :].astype(BF16), wp_ref[...])
            if final_norm:
                x = _rms(x, gf_ref[...])
            out_ref[rows, :] = x

    n_stages = n_ff + 2
    for tick in range(n_stages + MLP_STAGE_LAG * (len(groups) - 1)):
        for g in reversed(range(len(groups))):
            k = tick - MLP_STAGE_LAG * g
            if 0 <= k < n_stages:
                stage(g, k)


def _mlp_ple(x2d, p2d, gm, wup, wdn, gp, wg, wp, gf, final_norm):
    t = x2d.shape[0]
    tm = TM_MLP

    def const(shape):
        return pl.BlockSpec(shape, lambda i: (0, 0), pipeline_mode=pl.Buffered(1))

    return pl.pallas_call(
        functools.partial(_mlp_ple_kernel, final_norm=final_norm),
        out_shape=jax.ShapeDtypeStruct((t, D_MODEL), F32),
        grid=(t // tm,),
        in_specs=[
            pl.BlockSpec((tm, D_MODEL), lambda i: (i, 0)),
            pl.BlockSpec((tm, PLE_DIM), lambda i: (i, 0)),
            const((1, D_MODEL)),
            const((D_MODEL, D_FF)),
            const((D_FF, D_MODEL)),
            const((1, D_MODEL)),
            const((D_MODEL, D_MODEL)),
            const((PLE_DIM, D_MODEL)),
            const((1, D_MODEL)),
        ],
        out_specs=pl.BlockSpec((tm, D_MODEL), lambda i: (i, 0)),
        compiler_params=pltpu.CompilerParams(
            dimension_semantics=("parallel",),
            vmem_limit_bytes=_vmem_limit(((tm, D_MODEL), F32, 4), ((tm, PLE_DIM), F32, 2),
                                         ((D_MODEL, D_FF), BF16, 2), ((D_MODEL, D_MODEL), BF16, 1),
                                         ((PLE_DIM, D_MODEL), BF16, 1))),
        name="mlp_ple",
    )(x2d, p2d, gm, wup, wdn, gp, wg, wp, gf)


def _prep_w_in_kernel(wa_ref, wb_ref, wif_ref, main_ref, wg_ref):
    i = pl.program_id(0)
    n_a = COL_QSW // PREP_COLS

    @pl.when(i < n_a)
    def _():
        main_ref[...] = wa_ref[...].T.astype(BF16)

    @pl.when(i >= n_a)
    def _():
        main_ref[...] = wb_ref[...].T.astype(BF16)

    @pl.when(i == 0)
    def _():
        lane = lax.broadcasted_iota(jnp.int32, (D_MODEL, LANES), 1)
        wg = jnp.where(lane < 2 * ML_HEADS, wif_ref[...].T, 0.0)
        wg_hi, wg_lo = _split_bf16(wg, 2)
        wg_ref[:, :LANES] = wg_hi
        wg_ref[:, LANES:] = wg_lo


def _prep_w_in(w_in):
    n_if = 2 * ML_HEADS
    cols = PREP_COLS
    n_a = COL_QSW // cols
    w_t = jnp.swapaxes(w_in, 0, 1)
    return pl.pallas_call(
        _prep_w_in_kernel,
        out_shape=(jax.ShapeDtypeStruct((D_MODEL, N_W), BF16),
                   jax.ShapeDtypeStruct((D_MODEL, 2 * LANES), BF16)),
        grid=(N_W // cols,),
        in_specs=[
            pl.BlockSpec((cols, D_MODEL), lambda i: (jnp.minimum(i, n_a - 1), 0)),
            pl.BlockSpec((pl.Element(cols), pl.Element(D_MODEL)),
                         lambda i: (pl.multiple_of(COL_QSW + n_if + jnp.maximum(i - n_a, 0) * cols, n_if), 0)),
            pl.BlockSpec((LANES, D_MODEL), lambda i: (COL_QSW // LANES, 0)),
        ],
        out_specs=(pl.BlockSpec((D_MODEL, cols), lambda i: (0, i)),
                   pl.BlockSpec((D_MODEL, 2 * LANES), lambda i: (0, 0))),
        compiler_params=pltpu.CompilerParams(
            dimension_semantics=("arbitrary",),
            vmem_limit_bytes=_vmem_limit(((cols, D_MODEL), F32, 4), ((LANES, D_MODEL), F32, 2),
                                         ((D_MODEL, cols), BF16, 2), ((D_MODEL, 2 * LANES), BF16, 2))),
        name="prep_w_in",
    )(w_t, w_t, w_t)


def kernel(x, p, norm_mix_g, w_in, conv_qk, b_if, mlstm_norm_g, sinks, w_branch_a, w_branch_b, w_out,
           norm_mlp_g, w_up, w_down, norm_ple_g, w_ple_gate, w_ple_proj, final_norm_g):
    batch, seq, _ = x.shape
    depth = w_in.shape[0]
    t = batch * seq
    x2d = x.reshape(t, D_MODEL)
    row = lambda v: v.reshape(1, -1)
    for i in range(depth):
        w_main, wg = _prep_w_in(w_in[i])
        bif = jnp.pad(b_if[i], (0, LANES - 2 * ML_HEADS)).reshape(1, LANES)
        ya, yb, gab = _front(x2d, sinks[i], row(norm_mix_g[i]), w_main, wg, conv_qk[i], bif,
                             row(mlstm_norm_g[i]), seq)
        x2d = _merge(x2d, ya, yb, gab, w_branch_a[i].astype(BF16), w_branch_b[i].astype(BF16),
                     w_out[i].astype(BF16))
        x2d = _mlp_ple(x2d, p[i].reshape(t, PLE_DIM), row(norm_mlp_g[i]), w_up[i].astype(BF16),
                       w_down[i].astype(BF16), row(norm_ple_g[i]), w_ple_gate[i].astype(BF16),
                       w_ple_proj[i].astype(BF16), row(final_norm_g), final_norm=(i == depth - 1))
    return x2d.reshape(batch, seq, D_MODEL)
```

```python
import functools
import math

import jax
import jax.numpy as jnp
from jax import lax
from jax.experimental import pallas as pl
from jax.experimental.pallas import tpu as pltpu

D_MODEL = 1024
PLE_DIM = 256
ML_HEADS = 4
ML_DQK = 128
ML_DV = 256
ML_CONV = 4
SW_Q_HEADS = 16
SW_KV_HEADS = 4
SW_HEAD_DIM = 64
SW_WINDOW = 128
SW_GROUP = SW_Q_HEADS // SW_KV_HEADS
D_FF = 4 * D_MODEL
EPS = 1e-6

ML_QK_W = ML_HEADS * ML_DQK
ML_V_W = ML_HEADS * ML_DV
SW_Q_W = SW_Q_HEADS * SW_HEAD_DIM
SW_KV_W = SW_KV_HEADS * SW_HEAD_DIM

LANES = 128
BF16_SUBLANES = 16
MXU_N = 256
VMEM_BYTES = 64 * 1024 * 1024
VMEM_COMPILER_SHARE = 16 * 1024 * 1024

COL_QK = 0
COL_V = COL_QK + 2 * ML_QK_W
COL_O = COL_V + ML_V_W
COL_QSW = COL_O + ML_V_W
COL_KSW = COL_QSW + SW_Q_W
COL_VSW = COL_KSW + 2 * SW_KV_W
N_SLAB = COL_VSW + 2 * SW_KV_W
N_GATES = 2 * D_MODEL
WCOL_KSW = COL_KSW
WCOL_VSW = WCOL_KSW + SW_KV_W
WCOL_GA = WCOL_VSW + SW_KV_W
N_W = WCOL_GA + N_GATES

ML_CHUNK = 128
ML_SKEW = 1
SW_SKEW = 2
TT_FRONT = 512
PROJ_PIECE = MXU_N
MIXERS_END = 0.9
TM_MERGE = 1024
TM_MLP = 1024
ROW_GROUP = 512
MLP_STAGE_LAG = 2
FF_CHUNK = 1024
PREP_COLS = 512

F32 = jnp.float32
BF16 = jnp.bfloat16


def _vmem_limit(*buffers):
    need = sum(copies * math.prod(shape) * jnp.dtype(dtype).itemsize for shape, dtype, copies in buffers)
    assert need + VMEM_COMPILER_SHARE <= VMEM_BYTES, need
    return need + VMEM_COMPILER_SHARE


def _dot(a, b):
    return jnp.dot(a, b, preferred_element_type=F32)


def _dot_nt(a, b):
    return lax.dot_general(a, b, (((1,), (1,)), ((), ())), preferred_element_type=F32)


def _rms(x, g):
    return x * lax.rsqrt(jnp.mean(x * x, axis=-1, keepdims=True) + EPS) * g


def _split_bf16(x, parts):
    out = []
    r = x
    for _ in range(parts - 1):
        t = r.astype(BF16)
        out.append(t)
        r = r - t.astype(F32)
    out.append(r.astype(BF16))
    return out


def _interleave(*spans):
    tagged = []
    for k, (lst, lo, hi) in enumerate(spans):
        tagged += [(lo + (hi - lo) * (i + 0.5) / len(lst), k, i, f) for i, f in enumerate(lst)]
    return [f for _, _, _, f in sorted(tagged, key=lambda e: e[:3])]


def _projection_steps(g_ref, w_ref, wg_ref, hn_ref, slab_ref, gates_ref, gab_ref):
    lo_half = lax.broadcasted_iota(jnp.int32, (TT_FRONT, LANES), 1) < SW_HEAD_DIM

    def norm(src_ref):
        hn_ref[...] = _rms(src_ref[...], g_ref[...]).astype(BF16)

    def gates():
        hi = _dot(hn_ref[...], wg_ref[...])
        gates_ref[...] = hi[:, :LANES] + hi[:, LANES:]

    def dup_heads(val):
        out = []
        for j in range(val.shape[1] // LANES):
            vj = val[:, j * LANES:(j + 1) * LANES]
            rj = pltpu.roll(vj, SW_HEAD_DIM, axis=1)
            out += [jnp.where(lo_half, vj, rj), jnp.where(lo_half, rj, vj)]
        return jnp.concatenate(out, axis=1)

    def piece(c0):
        def run():
            val = _dot(hn_ref[...], w_ref[:, c0:c0 + PROJ_PIECE])
            if c0 == WCOL_KSW:
                slab_ref[:, COL_KSW:COL_VSW] = dup_heads(val).astype(BF16)
            elif c0 == WCOL_VSW:
                slab_ref[:, COL_VSW:N_SLAB] = dup_heads(val).astype(BF16)
            elif c0 < WCOL_KSW:
                slab_ref[:, c0:c0 + PROJ_PIECE] = val.astype(BF16)
            else:
                gab_ref[:, c0 - WCOL_GA:c0 - WCOL_GA + PROJ_PIECE] = val.astype(BF16)
        return run

    return norm, [gates] + [piece(c0) for c0 in range(0, N_W, PROJ_PIECE)]


def _mlstm_steps(slab_ref, gates_ref, qk_tail_ref, first, convw_ref, bif_ref, gn_ref, y_ref,
                 c_ref, n_ref, m_ref, n_chunks):
    L = ML_CHUNK
    H = BF16_SUBLANES
    taps = ML_CONV - 1
    row = lax.broadcasted_iota(jnp.int32, (L, L), 0)
    col = lax.broadcasted_iota(jnp.int32, (L, L), 1)
    causal = row >= col
    tril = jnp.where(causal, 1.0, 0.0).astype(BF16)
    cw = convw_ref[...]
    bif = bif_ref[...]
    row8 = lax.broadcasted_iota(jnp.int32, (8, 2 * ML_QK_W), 0)

    def chunk_prep(c):
        r0 = c * L
        x_cur = slab_ref[r0:r0 + L, COL_QK:COL_QK + 2 * ML_QK_W].astype(F32)
        if c == 0:
            halo = jnp.where(first, 0.0, qk_tail_ref[...].astype(F32))[H - 8:, :]
        else:
            halo = slab_ref[r0 - H:r0, COL_QK:COL_QK + 2 * ML_QK_W].astype(F32)[H - 8:, :]
        acc = cw[taps:taps + 1, :] * x_cur
        for k in range(1, taps + 1):
            rolled = pltpu.roll(x_cur, k, axis=0)
            top = jnp.where(row8 < k, pltpu.roll(halo, k, axis=0), rolled[:8, :])
            acc = acc + cw[taps - k:taps - k + 1, :] * jnp.concatenate([top, rolled[8:, :]], axis=0)
        act = acc * jax.nn.sigmoid(acc)
        gates = gates_ref[r0:r0 + L, :] + bif
        logf = jnp.minimum(gates, 0.0) - jnp.log1p(jnp.exp(-jnp.abs(gates)))
        bsplit = _dot(tril, jnp.concatenate(_split_bf16(logf, 3), axis=1))
        bcum = bsplit[:, :LANES] + bsplit[:, LANES:2 * LANES] + bsplit[:, 2 * LANES:]
        return dict(q=act[:, :ML_QK_W] * (ML_DQK ** -0.5), k=act[:, ML_QK_W:], gates=gates, bcum=bcum,
                    gates_t=gates.T, bcum_t=bcum.T)

    def qk_part(c, h, pre):
        qf = pre["q"][:, h * ML_DQK:(h + 1) * ML_DQK]
        kf = pre["k"][:, h * ML_DQK:(h + 1) * ML_DQK]
        q = qf.astype(BF16)
        kt = kf.T.astype(BF16)
        return dict(q=q, qf=qf, kf=kf, kt=kt, qk=_dot(q, kt))

    def local_part(c, h, pre, qk):
        r0 = c * L
        fl = ML_HEADS + h
        b_c = pre["bcum"][:, fl:fl + 1]
        i_c = pre["gates"][:, h:h + 1]
        r_r = pre["gates_t"][h:h + 1, :] - pre["bcum_t"][fl:fl + 1, :]
        v = slab_ref[r0:r0 + L, COL_V + h * ML_DV:COL_V + (h + 1) * ML_DV]

        log_d = jnp.where(causal, b_c + r_r, -jnp.inf)
        a = jnp.max(log_d, axis=1, keepdims=True)
        s_loc = qk["qk"] * jnp.exp(log_d - a)
        b_last = pre["bcum"][L - 1:L, fl:fl + 1]
        log_w = b_last - b_c + i_c
        aw = jnp.max(log_w, axis=0, keepdims=True)
        w_loc = jnp.exp(log_w - aw)
        return dict(q=qk["q"], qf=qk["qf"], b_c=b_c, a=a, b_last=b_last, aw=aw,
                    sv=_dot(s_loc.astype(BF16), v), rs=jnp.sum(s_loc, axis=1, keepdims=True),
                    u=_dot(qk["kt"], (w_loc * v.astype(F32)).astype(BF16)),
                    nu=jnp.sum(w_loc * qk["kf"], axis=0, keepdims=True))

    def carried_part(c, h, loc, state):
        r0 = c * L
        ct, n_prev, m_prev = state
        inter = loc["b_c"] + m_prev
        m_t = jnp.maximum(inter, loc["a"])
        w_inter = jnp.exp(inter - m_t)
        f_loc = jnp.exp(loc["a"] - m_t)
        num = w_inter * _dot(loc["q"], ct.astype(BF16)) + f_loc * loc["sv"]
        den = w_inter * jnp.sum(loc["qf"] * n_prev, axis=1, keepdims=True) + f_loc * loc["rs"]
        rden = 1.0 / jnp.maximum(jnp.abs(den), jnp.exp(-m_t))
        ms = jnp.mean(num * num, axis=1, keepdims=True)
        scale = rden * lax.rsqrt(rden * rden * ms + EPS)
        og = jax.nn.sigmoid(slab_ref[r0:r0 + L, COL_O + h * ML_DV:COL_O + (h + 1) * ML_DV].astype(F32))
        y_ref[r0:r0 + L, h * ML_DV:(h + 1) * ML_DV] = (
            og * (num * scale * gn_ref[:, h * ML_DV:(h + 1) * ML_DV])).astype(BF16)
        m_new = jnp.maximum(loc["b_last"] + m_prev, loc["aw"])
        decay = jnp.exp(loc["b_last"] + m_prev - m_new)
        g_loc = jnp.exp(loc["aw"] - m_new)
        return decay * ct + g_loc * loc["u"], decay * n_prev + g_loc * loc["nu"], m_new

    units = [(c, h) for c in range(n_chunks) for h in range(ML_HEADS)]
    n_units = len(units)
    state, pre, qk, loc = {}, {}, {}, {}
    n_ticks = n_units + 2 * ML_SKEW

    def step(i):
        def run():
            if i == 0:
                for h in range(ML_HEADS):
                    state[h] = (c_ref[h], n_ref[h], m_ref[h])
                pre[0] = chunk_prep(0)
            if i + 1 < n_units and units[i + 1][1] == 0:
                pre[units[i + 1][0]] = chunk_prep(units[i + 1][0])
            if i < n_units:
                c, h = units[i]
                qk[i] = qk_part(c, h, pre[c])
            j = i - ML_SKEW
            if 0 <= j < n_units:
                c, h = units[j]
                loc[j] = local_part(c, h, pre[c], qk.pop(j))
            j = i - 2 * ML_SKEW
            if 0 <= j < n_units:
                c, h = units[j]
                state[h] = carried_part(c, h, loc.pop(j), state[h])
            if i == n_ticks - 1:
                for h in range(ML_HEADS):
                    c_ref[h], n_ref[h], m_ref[h] = state[h]
        return run

    return [step(i) for i in range(n_ticks)]


def _swa_steps(slab_ref, kv_tail_ref, first, sinks_ref, y_ref, n_blocks):
    W = SW_WINDOW
    hd = SW_HEAD_DIM
    qi = lax.broadcasted_iota(jnp.int32, (W, 2 * W), 0)
    key = lax.broadcasted_iota(jnp.int32, (W, 2 * W), 1) % W
    from_prev = key > qi
    lo_half = lax.broadcasted_iota(jnp.int32, (W, LANES), 1) < hd
    zero = jnp.zeros((), BF16)
    prev_bias = jnp.where(first, -jnp.inf, 0.0).astype(F32)

    def block_diag(d):
        return jnp.concatenate([jnp.where(lo_half, d, zero), jnp.where(lo_half, zero, d)], axis=0)

    cache = {}

    def kv_block(j, col0, h):
        if (j, col0, h) not in cache:
            if j < 0:
                d = kv_tail_ref[:, col0 - COL_KSW + h * LANES:col0 - COL_KSW + (h + 1) * LANES]
            else:
                d = slab_ref[j * W:(j + 1) * W, col0 + h * LANES:col0 + (h + 1) * LANES]
            cache[j, col0, h] = block_diag(d)
        return cache[j, col0, h]

    units = [(j, h, pair) for j in range(n_blocks) for h in range(SW_KV_HEADS)
             for pair in range(SW_GROUP // 2)]

    def logits_of(j, h, pair):
        c0 = COL_QSW + (h * SW_GROUP + 2 * pair) * hd
        qp = slab_ref[j * W:(j + 1) * W, c0:c0 + LANES] * jnp.asarray(hd ** -0.5, BF16)
        s_prev = _dot_nt(qp, kv_block(j - 1, COL_KSW, h))
        if j == 0:
            s_prev = s_prev + prev_bias
        return jnp.where(from_prev, s_prev, _dot_nt(qp, kv_block(j, COL_KSW, h)))

    def probs_of(lg, h, pair):
        halves = []
        for e in range(2):
            sink = sinks_ref[h * SW_GROUP + 2 * pair + e]
            le = lg[:, e * W:(e + 1) * W]
            m = jnp.maximum(jnp.max(le, axis=1, keepdims=True), sink)
            p = jnp.exp(le - m)
            denom = jnp.sum(p, axis=1, keepdims=True) + jnp.exp(sink - m)
            halves.append((p * (1.0 / denom)).astype(BF16))
        return jnp.concatenate(halves, axis=1)

    def write_out(pb, j, h, pair):
        c0 = (h * SW_GROUP + 2 * pair) * hd
        out = (_dot(jnp.where(from_prev, pb, zero), kv_block(j - 1, COL_VSW, h))
               + _dot(jnp.where(from_prev, zero, pb), kv_block(j, COL_VSW, h)))
        y_ref[j * W:(j + 1) * W, c0:c0 + LANES] = out.astype(BF16)

    logits, probs = {}, {}
    n = len(units)

    def step(i):
        def run():
            if i < n:
                logits[i] = logits_of(*units[i])
            if 0 <= i - SW_SKEW < n:
                _, h, pair = units[i - SW_SKEW]
                probs[i - SW_SKEW] = probs_of(logits.pop(i - SW_SKEW), h, pair)
            if 0 <= i - 2 * SW_SKEW < n:
                write_out(probs.pop(i - 2 * SW_SKEW), *units[i - 2 * SW_SKEW])
        return run

    return [step(i) for i in range(n + 2 * SW_SKEW)]


def _front_kernel(sinks_ref, x_ref, g_ref, w_ref, wg_ref, convw_ref, bif_ref, gn_ref,
                  ya_ref, yb_ref, gab_ref,
                  hn_ref, slab_new_ref, slab_ref, gates_new_ref, gates_ref, kv_tail_ref, qk_tail_ref,
                  c_ref, n_ref, m_ref, *, tiles_per_seq):
    tt = TT_FRONT
    s = pl.program_id(0)
    first = (jnp.maximum(s - 1, 0) % tiles_per_seq) == 0

    @pl.when(s == 0)
    def _():
        slab_ref[...] = jnp.zeros_like(slab_ref)
        gates_ref[...] = jnp.zeros_like(gates_ref)
        kv_tail_ref[...] = jnp.zeros_like(kv_tail_ref)
        qk_tail_ref[...] = jnp.zeros_like(qk_tail_ref)

    @pl.when(first)
    def _():
        c_ref[...] = jnp.zeros_like(c_ref)
        n_ref[...] = jnp.zeros_like(n_ref)
        m_ref[...] = jnp.zeros_like(m_ref)

    norm, pieces = _projection_steps(g_ref, w_ref, wg_ref, hn_ref, slab_new_ref, gates_new_ref, gab_ref)
    ml = _mlstm_steps(slab_ref, gates_ref, qk_tail_ref, first, convw_ref, bif_ref, gn_ref, ya_ref,
                      c_ref, n_ref, m_ref, tt // ML_CHUNK)
    sw = _swa_steps(slab_ref, kv_tail_ref, first, sinks_ref, yb_ref, tt // SW_WINDOW)
    norm(x_ref)
    for thunk in _interleave((pieces, 0.0, 1.0), (ml, 0.0, MIXERS_END), (sw, 0.0, MIXERS_END)):
        thunk()

    kv_tail_ref[...] = slab_ref[tt - SW_WINDOW:tt, COL_KSW:N_SLAB]
    qk_tail_ref[...] = slab_ref[tt - BF16_SUBLANES:tt, COL_QK:COL_QK + 2 * ML_QK_W]
    slab_ref[...] = slab_new_ref[...]
    gates_ref[...] = gates_new_ref[...]


def _front(x2d, sinks, g, w_main, wg, conv_w, b_if, gn, seq):
    t = x2d.shape[0]
    tt = TT_FRONT
    nt = t // tt
    W = SW_WINDOW

    def const(shape):
        return pl.BlockSpec(shape, lambda s: (0, 0), pipeline_mode=pl.Buffered(1))

    proj_tile = lambda s: (jnp.minimum(s, nt - 1), 0)
    mix_tile = lambda s: (jnp.maximum(s - 1, 0), 0)
    return pl.pallas_call(
        functools.partial(_front_kernel, tiles_per_seq=seq // tt),
        out_shape=(jax.ShapeDtypeStruct((t, ML_V_W), BF16),
                   jax.ShapeDtypeStruct((t, SW_Q_W), BF16),
                   jax.ShapeDtypeStruct((t, N_GATES), BF16)),
        grid=(nt + 1,),
        in_specs=[
            pl.BlockSpec(memory_space=pltpu.SMEM),
            pl.BlockSpec((tt, D_MODEL), proj_tile),
            const((1, D_MODEL)),
            const((D_MODEL, N_W)),
            const((D_MODEL, 2 * LANES)),
            const((ML_CONV, 2 * ML_QK_W)),
            const((1, LANES)),
            const((1, ML_V_W)),
        ],
        out_specs=(pl.BlockSpec((tt, ML_V_W), mix_tile),
                   pl.BlockSpec((tt, SW_Q_W), mix_tile),
                   pl.BlockSpec((tt, N_GATES), proj_tile)),
        scratch_shapes=[
            pltpu.VMEM((tt, D_MODEL), BF16),
            pltpu.VMEM((tt, N_SLAB), BF16),
            pltpu.VMEM((tt, N_SLAB), BF16),
            pltpu.VMEM((tt, LANES), F32),
            pltpu.VMEM((tt, LANES), F32),
            pltpu.VMEM((W, N_SLAB - COL_KSW), BF16),
            pltpu.VMEM((BF16_SUBLANES, 2 * ML_QK_W), BF16),
            pltpu.VMEM((ML_HEADS, ML_DQK, ML_DV), F32),
            pltpu.VMEM((ML_HEADS, 1, ML_DQK), F32),
            pltpu.VMEM((ML_HEADS, 1, 1), F32),
        ],
        compiler_params=pltpu.CompilerParams(
            dimension_semantics=("arbitrary",),
            vmem_limit_bytes=_vmem_limit(
                ((tt, D_MODEL), F32, 2), ((D_MODEL, N_W), BF16, 1), ((D_MODEL, 2 * LANES), BF16, 1),
                ((tt, ML_V_W), BF16, 2), ((tt, SW_Q_W), BF16, 2), ((tt, N_GATES), BF16, 2),
                ((tt, D_MODEL), BF16, 1), ((tt, N_SLAB), BF16, 2), ((tt, LANES), F32, 2),
                ((W, N_SLAB - COL_KSW), BF16, 1), ((ML_HEADS, ML_DQK, ML_DV), F32, 1))),
        name="front",
    )(sinks, x2d, g, w_main, wg, conv_w, b_if, gn)


def _merge_kernel(x_ref, ya_ref, yb_ref, ga_ref, gb_ref, wa_ref, wb_ref, wo_ref, out_ref):
    groups = [slice(r, r + ROW_GROUP) for r in range(0, TM_MERGE, ROW_GROUP)]
    merged = {}

    def branches(rows):
        a = jax.nn.sigmoid(ga_ref[rows, :].astype(F32)) * _dot(ya_ref[rows, :], wa_ref[...])
        b = jax.nn.sigmoid(gb_ref[rows, :].astype(F32)) * _dot(yb_ref[rows, :], wb_ref[...])
        merged[rows.start] = (a + b).astype(BF16)

    def project(rows):
        out_ref[rows, :] = x_ref[rows, :] + _dot(merged.pop(rows.start), wo_ref[...])

    for i in range(len(groups) + 1):
        if i < len(groups):
            branches(groups[i])
        if i >= 1:
            project(groups[i - 1])


def _merge(x2d, ya, yb, gab, wa, wb, wo):
    t = x2d.shape[0]
    tm = TM_MERGE
    wspec = pl.BlockSpec((D_MODEL, D_MODEL), lambda i: (0, 0), pipeline_mode=pl.Buffered(1))
    return pl.pallas_call(
        _merge_kernel,
        out_shape=jax.ShapeDtypeStruct((t, D_MODEL), F32),
        grid=(t // tm,),
        in_specs=[
            pl.BlockSpec((tm, D_MODEL), lambda i: (i, 0)),
            pl.BlockSpec((tm, D_MODEL), lambda i: (i, 0)),
            pl.BlockSpec((tm, D_MODEL), lambda i: (i, 0)),
            pl.BlockSpec((tm, D_MODEL), lambda i: (i, 0)),
            pl.BlockSpec((tm, D_MODEL), lambda i: (i, 1)),
            wspec, wspec, wspec,
        ],
        out_specs=pl.BlockSpec((tm, D_MODEL), lambda i: (i, 0)),
        compiler_params=pltpu.CompilerParams(
            dimension_semantics=("parallel",),
            vmem_limit_bytes=_vmem_limit(((tm, D_MODEL), F32, 4), ((tm, D_MODEL), BF16, 8),
                                         ((D_MODEL, D_MODEL), BF16, 3))),
        name="merge_out",
    )(x2d, ya, yb, gab, gab, wa, wb, wo)


def _mlp_ple_kernel(x_ref, p_ref, gm_ref, wup_ref, wdn_ref, gp_ref, wg_ref, wp_ref, gf_ref, out_ref,
                    *, final_norm):
    groups = [slice(r, r + ROW_GROUP) for r in range(0, TM_MLP, ROW_GROUP)]
    n_ff = D_FF // FF_CHUNK
    hn, acc = {}, {}

    def stage(g, k):
        rows = groups[g]
        if k == 0:
            x = x_ref[rows, :]
            hn[g] = _rms(x, gm_ref[...]).astype(BF16)
            acc[g] = x
        elif k <= n_ff:
            c = k - 1
            u = _dot(hn[g], wup_ref[:, c * FF_CHUNK:(c + 1) * FF_CHUNK])
            r = jnp.maximum(u, 0.0)
            acc[g] = acc[g] + _dot((r * r).astype(BF16), wdn_ref[c * FF_CHUNK:(c + 1) * FF_CHUNK, :])
        else:
            x_all = acc.pop(g)
            half = ROW_GROUP // 2
            for r0 in range(0, ROW_GROUP, half):
                sub = slice(rows.start + r0, rows.start + r0 + half)
                x = x_all[r0:r0 + half, :]
                gate = jax.nn.sigmoid(_dot(_rms(x, gp_ref[...]).astype(BF16), wg_ref[...]))
                x = x + gate * _dot(p_ref[sub, :].astype(BF16), wp_ref[...])
                if final_norm:
                    x = _rms(x, gf_ref[...])
                out_ref[sub, :] = x

    n_stages = n_ff + 2
    for tick in range(n_stages + MLP_STAGE_LAG * (len(groups) - 1)):
        for g in reversed(range(len(groups))):
            k = tick - MLP_STAGE_LAG * g
            if 0 <= k < n_stages:
                stage(g, k)


def _mlp_ple(x2d, p2d, gm, wup, wdn, gp, wg, wp, gf, final_norm):
    t = x2d.shape[0]
    tm = TM_MLP

    def const(shape):
        return pl.BlockSpec(shape, lambda i: (0, 0), pipeline_mode=pl.Buffered(1))

    return pl.pallas_call(
        functools.partial(_mlp_ple_kernel, final_norm=final_norm),
        out_shape=jax.ShapeDtypeStruct((t, D_MODEL), F32),
        grid=(t // tm,),
        in_specs=[
            pl.BlockSpec((tm, D_MODEL), lambda i: (i, 0)),
            pl.BlockSpec((tm, PLE_DIM), lambda i: (i, 0)),
            const((1, D_MODEL)),
            const((D_MODEL, D_FF)),
            const((D_FF, D_MODEL)),
            const((1, D_MODEL)),
            const((D_MODEL, D_MODEL)),
            const((PLE_DIM, D_MODEL)),
            const((1, D_MODEL)),
        ],
        out_specs=pl.BlockSpec((tm, D_MODEL), lambda i: (i, 0)),
        compiler_params=pltpu.CompilerParams(
            dimension_semantics=("parallel",),
            vmem_limit_bytes=_vmem_limit(((tm, D_MODEL), F32, 4), ((tm, PLE_DIM), F32, 2),
                                         ((D_MODEL, D_FF), BF16, 2), ((D_MODEL, D_MODEL), BF16, 1),
                                         ((PLE_DIM, D_MODEL), BF16, 1))),
        name="mlp_ple",
    )(x2d, p2d, gm, wup, wdn, gp, wg, wp, gf)


def _prep_w_in_kernel(wa_ref, wb_ref, wif_ref, main_ref, wg_ref):
    i = pl.program_id(0)
    n_a = COL_QSW // PREP_COLS

    @pl.when(i < n_a)
    def _():
        main_ref[...] = wa_ref[...].T.astype(BF16)

    @pl.when(i >= n_a)
    def _():
        main_ref[...] = wb_ref[...].T.astype(BF16)

    @pl.when(i == 0)
    def _():
        lane = lax.broadcasted_iota(jnp.int32, (D_MODEL, LANES), 1)
        wg = jnp.where(lane < 2 * ML_HEADS, wif_ref[...].T, 0.0)
        wg_hi, wg_lo = _split_bf16(wg, 2)
        wg_ref[:, :LANES] = wg_hi
        wg_ref[:, LANES:] = wg_lo


def _prep_w_in(w_in):
    n_if = 2 * ML_HEADS
    cols = PREP_COLS
    n_a = COL_QSW // cols
    w_t = jnp.swapaxes(w_in, 0, 1)
    return pl.pallas_call(
        _prep_w_in_kernel,
        out_shape=(jax.ShapeDtypeStruct((D_MODEL, N_W), BF16),
                   jax.ShapeDtypeStruct((D_MODEL, 2 * LANES), BF16)),
        grid=(N_W // cols,),
        in_specs=[
            pl.BlockSpec((cols, D_MODEL), lambda i: (jnp.minimum(i, n_a - 1), 0)),
            pl.BlockSpec((pl.Element(cols), pl.Element(D_MODEL)),
                         lambda i: (pl.multiple_of(COL_QSW + n_if + jnp.maximum(i - n_a, 0) * cols, n_if), 0)),
            pl.BlockSpec((LANES, D_MODEL), lambda i: (COL_QSW // LANES, 0)),
        ],
        out_specs=(pl.BlockSpec((D_MODEL, cols), lambda i: (0, i)),
                   pl.BlockSpec((D_MODEL, 2 * LANES), lambda i: (0, 0))),
        compiler_params=pltpu.CompilerParams(
            dimension_semantics=("arbitrary",),
            vmem_limit_bytes=_vmem_limit(((cols, D_MODEL), F32, 4), ((LANES, D_MODEL), F32, 2),
                                         ((D_MODEL, cols), BF16, 2), ((D_MODEL, 2 * LANES), BF16, 2))),
        name="prep_w_in",
    )(w_t, w_t, w_t)


def kernel(x, p, norm_mix_g, w_in, conv_qk, b_if, mlstm_norm_g, sinks, w_branch_a, w_branch_b, w_out,
           norm_mlp_g, w_up, w_down, norm_ple_g, w_ple_gate, w_ple_proj, final_norm_g):
    batch, seq, _ = x.shape
    depth = w_in.shape[0]
    t = batch * seq
    x2d = x.reshape(t, D_MODEL)
    row = lambda v: v.reshape(1, -1)
    for i in range(depth):
        w_main, wg = _prep_w_in(w_in[i])
        bif = jnp.pad(b_if[i], (0, LANES - 2 * ML_HEADS)).reshape(1, LANES)
        ya, yb, gab = _front(x2d, sinks[i], row(norm_mix_g[i]), w_main, wg, conv_qk[i], bif,
                             row(mlstm_norm_g[i]), seq)
        x2d = _merge(x2d, ya, yb, gab, w_branch_a[i].astype(BF16), w_branch_b[i].astype(BF16),
                     w_out[i].astype(BF16))
        x2d = _mlp_ple(x2d, p[i].reshape(t, PLE_DIM), row(norm_mlp_g[i]), w_up[i].astype(BF16),
                       w_down[i].astype(BF16), row(norm_ple_g[i]), w_ple_gate[i].astype(BF16),
                       w_ple_proj[i].astype(BF16), row(final_norm_g), final_norm=(i == depth - 1))
    return x2d.reshape(batch, seq, D_MODEL)
```

```python
import functools
import math

import jax
import jax.numpy as jnp
from jax import lax
from jax.experimental import pallas as pl
from jax.experimental.pallas import tpu as pltpu

D_MODEL = 1024
PLE_DIM = 256
ML_HEADS = 4
ML_DQK = 128
ML_DV = 256
ML_CONV = 4
SW_Q_HEADS = 16
SW_KV_HEADS = 4
SW_HEAD_DIM = 64
SW_WINDOW = 128
SW_GROUP = SW_Q_HEADS // SW_KV_HEADS
D_FF = 4 * D_MODEL
EPS = 1e-6

ML_QK_W = ML_HEADS * ML_DQK
ML_V_W = ML_HEADS * ML_DV
SW_Q_W = SW_Q_HEADS * SW_HEAD_DIM
SW_KV_W = SW_KV_HEADS * SW_HEAD_DIM

LANES = 128
BF16_SUBLANES = 16
MXU_N = 256
VMEM_BYTES = 64 * 1024 * 1024
VMEM_COMPILER_SHARE = 16 * 1024 * 1024

COL_QK = 0
COL_V = COL_QK + 2 * ML_QK_W
COL_O = COL_V + ML_V_W
COL_QSW = COL_O + ML_V_W
COL_KSW = COL_QSW + SW_Q_W
COL_VSW = COL_KSW + 2 * SW_KV_W
N_SLAB = COL_VSW + 2 * SW_KV_W
N_GATES = 2 * D_MODEL
WCOL_KSW = COL_KSW
WCOL_VSW = WCOL_KSW + SW_KV_W
WCOL_GA = WCOL_VSW + SW_KV_W
N_W = WCOL_GA + N_GATES

ML_CHUNK = 128
ML_SKEW = 1
SW_SKEW = 2
TT_FRONT = 512
PROJ_PIECE = MXU_N
MIXERS_END = 0.9
TM_MERGE = 1024
TM_MLP = 1024
ROW_GROUP = 512
MLP_STAGE_LAG = 2
FF_CHUNK = 1024
PREP_COLS = 512

F32 = jnp.float32
BF16 = jnp.bfloat16


def _vmem_limit(*buffers):
    need = sum(copies * math.prod(shape) * jnp.dtype(dtype).itemsize for shape, dtype, copies in buffers)
    assert need + VMEM_COMPILER_SHARE <= VMEM_BYTES, need
    return need + VMEM_COMPILER_SHARE


def _dot(a, b):
    return jnp.dot(a, b, preferred_element_type=F32)


def _dot_nt(a, b):
    return lax.dot_general(a, b, (((1,), (1,)), ((), ())), preferred_element_type=F32)


def _rms(x, g):
    return x * lax.rsqrt(jnp.mean(x * x, axis=-1, keepdims=True) + EPS) * g


def _split_bf16(x, parts):
    out = []
    r = x
    for _ in range(parts - 1):
        t = r.astype(BF16)
        out.append(t)
        r = r - t.astype(F32)
    out.append(r.astype(BF16))
    return out


def _interleave(*spans):
    tagged = []
    for k, (lst, lo, hi) in enumerate(spans):
        tagged += [(lo + (hi - lo) * (i + 0.5) / len(lst), k, i, f) for i, f in enumerate(lst)]
    return [f for _, _, _, f in sorted(tagged, key=lambda e: e[:3])]


def _projection_steps(g_ref, w_ref, wg_ref, hn_ref, slab_ref, gates_ref, gab_ref):
    lo_half = lax.broadcasted_iota(jnp.int32, (TT_FRONT, LANES), 1) < SW_HEAD_DIM

    def norm(src_ref):
        hn_ref[...] = _rms(src_ref[...], g_ref[...]).astype(BF16)

    def gates():
        hi = _dot(hn_ref[...], wg_ref[...])
        gates_ref[...] = hi[:, :LANES] + hi[:, LANES:]

    def dup_heads(val):
        out = []
        for j in range(val.shape[1] // LANES):
            vj = val[:, j * LANES:(j + 1) * LANES]
            rj = pltpu.roll(vj, SW_HEAD_DIM, axis=1)
            out += [jnp.where(lo_half, vj, rj), jnp.where(lo_half, rj, vj)]
        return jnp.concatenate(out, axis=1)

    def piece(c0):
        def run():
            val = _dot(hn_ref[...], w_ref[:, c0:c0 + PROJ_PIECE])
            if c0 == WCOL_KSW:
                slab_ref[:, COL_KSW:COL_VSW] = dup_heads(val).astype(BF16)
            elif c0 == WCOL_VSW:
                slab_ref[:, COL_VSW:N_SLAB] = dup_heads(val).astype(BF16)
            elif c0 < WCOL_KSW:
                slab_ref[:, c0:c0 + PROJ_PIECE] = val.astype(BF16)
            else:
                gab_ref[:, c0 - WCOL_GA:c0 - WCOL_GA + PROJ_PIECE] = val.astype(BF16)
        return run

    return norm, [piece(c0) for c0 in range(0, N_W, PROJ_PIECE)] + [gates]


def _mlstm_steps(slab_ref, gates_ref, qk_tail_ref, first, convw_ref, bif_ref, gn_ref, y_ref,
                 c_ref, n_ref, m_ref, n_chunks):
    L = ML_CHUNK
    H = BF16_SUBLANES
    taps = ML_CONV - 1
    row = lax.broadcasted_iota(jnp.int32, (L, L), 0)
    col = lax.broadcasted_iota(jnp.int32, (L, L), 1)
    causal = row >= col
    tril = jnp.where(causal, 1.0, 0.0).astype(BF16)
    cw = convw_ref[...]
    bif = bif_ref[...]
    row8 = lax.broadcasted_iota(jnp.int32, (8, 2 * ML_QK_W), 0)

    def chunk_prep(c):
        r0 = c * L
        x_cur = slab_ref[r0:r0 + L, COL_QK:COL_QK + 2 * ML_QK_W].astype(F32)
        if c == 0:
            halo = jnp.where(first, 0.0, qk_tail_ref[...].astype(F32))[H - 8:, :]
        else:
            halo = slab_ref[r0 - H:r0, COL_QK:COL_QK + 2 * ML_QK_W].astype(F32)[H - 8:, :]
        acc = cw[taps:taps + 1, :] * x_cur
        for k in range(1, taps + 1):
            rolled = pltpu.roll(x_cur, k, axis=0)
            top = jnp.where(row8 < k, pltpu.roll(halo, k, axis=0), rolled[:8, :])
            acc = acc + cw[taps - k:taps - k + 1, :] * jnp.concatenate([top, rolled[8:, :]], axis=0)
        act = acc * jax.nn.sigmoid(acc)
        gates = gates_ref[r0:r0 + L, :] + bif
        logf = jnp.minimum(gates, 0.0) - jnp.log1p(jnp.exp(-jnp.abs(gates)))
        bsplit = _dot(tril, jnp.concatenate(_split_bf16(logf, 3), axis=1))
        bcum = bsplit[:, :LANES] + bsplit[:, LANES:2 * LANES] + bsplit[:, 2 * LANES:]
        return dict(q=act[:, :ML_QK_W] * (ML_DQK ** -0.5), k=act[:, ML_QK_W:], gates=gates, bcum=bcum,
                    gates_t=gates.T, bcum_t=bcum.T)

    def qk_part(c, h, pre):
        qf = pre["q"][:, h * ML_DQK:(h + 1) * ML_DQK]
        kf = pre["k"][:, h * ML_DQK:(h + 1) * ML_DQK]
        q = qf.astype(BF16)
        kt = kf.T.astype(BF16)
        return dict(q=q, qf=qf, kf=kf, kt=kt, qk=_dot(q, kt))

    def local_part(c, h, pre, qk):
        r0 = c * L
        fl = ML_HEADS + h
        b_c = pre["bcum"][:, fl:fl + 1]
        i_c = pre["gates"][:, h:h + 1]
        r_r = pre["gates_t"][h:h + 1, :] - pre["bcum_t"][fl:fl + 1, :]
        v = slab_ref[r0:r0 + L, COL_V + h * ML_DV:COL_V + (h + 1) * ML_DV]

        log_d = jnp.where(causal, b_c + r_r, -jnp.inf)
        a = jnp.max(log_d, axis=1, keepdims=True)
        s_loc = qk["qk"] * jnp.exp(log_d - a)
        b_last = pre["bcum"][L - 1:L, fl:fl + 1]
        log_w = b_last - b_c + i_c
        aw = jnp.max(log_w, axis=0, keepdims=True)
        w_loc = jnp.exp(log_w - aw)
        return dict(q=qk["q"], qf=qk["qf"], b_c=b_c, a=a, b_last=b_last, aw=aw,
                    sv=_dot(s_loc.astype(BF16), v), rs=jnp.sum(s_loc, axis=1, keepdims=True),
                    u=_dot(qk["kt"], (w_loc * v.astype(F32)).astype(BF16)),
                    nu=jnp.sum(w_loc * qk["kf"], axis=0, keepdims=True))

    def carried_part(c, h, loc, state):
        r0 = c * L
        ct, n_prev, m_prev = state
        inter = loc["b_c"] + m_prev
        m_t = jnp.maximum(inter, loc["a"])
        w_inter = jnp.exp(inter - m_t)
        f_loc = jnp.exp(loc["a"] - m_t)
        num = w_inter * _dot(loc["q"], ct.astype(BF16)) + f_loc * loc["sv"]
        den = w_inter * jnp.sum(loc["qf"] * n_prev, axis=1, keepdims=True) + f_loc * loc["rs"]
        rden = 1.0 / jnp.maximum(jnp.abs(den), jnp.exp(-m_t))
        ms = jnp.mean(num * num, axis=1, keepdims=True)
        scale = rden * lax.rsqrt(rden * rden * ms + EPS)
        og = jax.nn.sigmoid(slab_ref[r0:r0 + L, COL_O + h * ML_DV:COL_O + (h + 1) * ML_DV].astype(F32))
        y_ref[r0:r0 + L, h * ML_DV:(h + 1) * ML_DV] = (
            og * (num * scale * gn_ref[:, h * ML_DV:(h + 1) * ML_DV])).astype(BF16)
        m_new = jnp.maximum(loc["b_last"] + m_prev, loc["aw"])
        decay = jnp.exp(loc["b_last"] + m_prev - m_new)
        g_loc = jnp.exp(loc["aw"] - m_new)
        return decay * ct + g_loc * loc["u"], decay * n_prev + g_loc * loc["nu"], m_new

    units = [(c, h) for c in range(n_chunks) for h in range(ML_HEADS)]
    n_units = len(units)
    state, pre, qk, loc = {}, {}, {}, {}
    n_ticks = n_units + 2 * ML_SKEW

    def step(i):
        def run():
            if i == 0:
                for h in range(ML_HEADS):
                    state[h] = (c_ref[h], n_ref[h], m_ref[h])
                pre[0] = chunk_prep(0)
            if i + 1 < n_units and units[i + 1][1] == 0:
                pre[units[i + 1][0]] = chunk_prep(units[i + 1][0])
            if i < n_units:
                c, h = units[i]
                qk[i] = qk_part(c, h, pre[c])
            j = i - ML_SKEW
            if 0 <= j < n_units:
                c, h = units[j]
                loc[j] = local_part(c, h, pre[c], qk.pop(j))
            j = i - 2 * ML_SKEW
            if 0 <= j < n_units:
                c, h = units[j]
                state[h] = carried_part(c, h, loc.pop(j), state[h])
            if i == n_ticks - 1:
                for h in range(ML_HEADS):
                    c_ref[h], n_ref[h], m_ref[h] = state[h]
        return run

    return [step(i) for i in range(n_ticks)]


def _swa_steps(slab_ref, kv_tail_ref, first, sinks_ref, y_ref, n_blocks):
    W = SW_WINDOW
    hd = SW_HEAD_DIM
    qi = lax.broadcasted_iota(jnp.int32, (W, 2 * W), 0)
    key = lax.broadcasted_iota(jnp.int32, (W, 2 * W), 1) % W
    from_prev = key > qi
    lo_half = lax.broadcasted_iota(jnp.int32, (W, LANES), 1) < hd
    zero = jnp.zeros((), BF16)
    prev_bias = jnp.where(first, -jnp.inf, 0.0).astype(F32)

    def block_diag(d):
        return jnp.concatenate([jnp.where(lo_half, d, zero), jnp.where(lo_half, zero, d)], axis=0)

    cache = {}

    def kv_block(j, col0, h):
        if (j, col0, h) not in cache:
            if j < 0:
                d = kv_tail_ref[:, col0 - COL_KSW + h * LANES:col0 - COL_KSW + (h + 1) * LANES]
            else:
                d = slab_ref[j * W:(j + 1) * W, col0 + h * LANES:col0 + (h + 1) * LANES]
            cache[j, col0, h] = block_diag(d)
        return cache[j, col0, h]

    units = [(j, h, pair) for j in range(n_blocks) for h in range(SW_KV_HEADS)
             for pair in range(SW_GROUP // 2)]

    def logits_of(j, h, pair):
        c0 = COL_QSW + (h * SW_GROUP + 2 * pair) * hd
        qp = slab_ref[j * W:(j + 1) * W, c0:c0 + LANES] * jnp.asarray(hd ** -0.5, BF16)
        s_prev = _dot_nt(qp, kv_block(j - 1, COL_KSW, h))
        if j == 0:
            s_prev = s_prev + prev_bias
        return jnp.where(from_prev, s_prev, _dot_nt(qp, kv_block(j, COL_KSW, h)))

    def probs_of(lg, h, pair):
        halves = []
        for e in range(2):
            sink = sinks_ref[h * SW_GROUP + 2 * pair + e]
            le = lg[:, e * W:(e + 1) * W]
            m = jnp.maximum(jnp.max(le, axis=1, keepdims=True), sink)
            p = jnp.exp(le - m)
            denom = jnp.sum(p, axis=1, keepdims=True) + jnp.exp(sink - m)
            halves.append((p * (1.0 / denom)).astype(BF16))
        return jnp.concatenate(halves, axis=1)

    def write_out(pb, j, h, pair):
        c0 = (h * SW_GROUP + 2 * pair) * hd
        out = (_dot(jnp.where(from_prev, pb, zero), kv_block(j - 1, COL_VSW, h))
               + _dot(jnp.where(from_prev, zero, pb), kv_block(j, COL_VSW, h)))
        y_ref[j * W:(j + 1) * W, c0:c0 + LANES] = out.astype(BF16)

    logits, probs = {}, {}
    n = len(units)

    def step(i):
        def run():
            if i < n:
                logits[i] = logits_of(*units[i])
            if 0 <= i - SW_SKEW < n:
                _, h, pair = units[i - SW_SKEW]
                probs[i - SW_SKEW] = probs_of(logits.pop(i - SW_SKEW), h, pair)
            if 0 <= i - 2 * SW_SKEW < n:
                write_out(probs.pop(i - 2 * SW_SKEW), *units[i - 2 * SW_SKEW])
        return run

    return [step(i) for i in range(n + 2 * SW_SKEW)]


def _front_kernel(sinks_ref, x_ref, g_ref, w_ref, wg_ref, convw_ref, bif_ref, gn_ref,
                  ya_ref, yb_ref, gab_ref,
                  hn_ref, slab_new_ref, slab_ref, gates_new_ref, gates_ref, kv_tail_ref, qk_tail_ref,
                  c_ref, n_ref, m_ref, *, tiles_per_seq):
    tt = TT_FRONT
    s = pl.program_id(0)
    first = (jnp.maximum(s - 1, 0) % tiles_per_seq) == 0

    @pl.when(s == 0)
    def _():
        slab_ref[...] = jnp.zeros_like(slab_ref)
        gates_ref[...] = jnp.zeros_like(gates_ref)
        kv_tail_ref[...] = jnp.zeros_like(kv_tail_ref)
        qk_tail_ref[...] = jnp.zeros_like(qk_tail_ref)

    @pl.when(first)
    def _():
        c_ref[...] = jnp.zeros_like(c_ref)
        n_ref[...] = jnp.zeros_like(n_ref)
        m_ref[...] = jnp.zeros_like(m_ref)

    norm, pieces = _projection_steps(g_ref, w_ref, wg_ref, hn_ref, slab_new_ref, gates_new_ref, gab_ref)
    ml = _mlstm_steps(slab_ref, gates_ref, qk_tail_ref, first, convw_ref, bif_ref, gn_ref, ya_ref,
                      c_ref, n_ref, m_ref, tt // ML_CHUNK)
    sw = _swa_steps(slab_ref, kv_tail_ref, first, sinks_ref, yb_ref, tt // SW_WINDOW)
    norm(x_ref)
    for thunk in _interleave((pieces, 0.0, 1.0), (ml, 0.0, MIXERS_END), (sw, 0.0, MIXERS_END)):
        thunk()

    kv_tail_ref[...] = slab_ref[tt - SW_WINDOW:tt, COL_KSW:N_SLAB]
    qk_tail_ref[...] = slab_ref[tt - BF16_SUBLANES:tt, COL_QK:COL_QK + 2 * ML_QK_W]
    slab_ref[...] = slab_new_ref[...]
    gates_ref[...] = gates_new_ref[...]


def _front(x2d, sinks, g, w_main, wg, conv_w, b_if, gn, seq):
    t = x2d.shape[0]
    tt = TT_FRONT
    nt = t // tt
    W = SW_WINDOW

    def const(shape):
        return pl.BlockSpec(shape, lambda s: (0, 0), pipeline_mode=pl.Buffered(1))

    proj_tile = lambda s: (jnp.minimum(s, nt - 1), 0)
    mix_tile = lambda s: (jnp.maximum(s - 1, 0), 0)
    return pl.pallas_call(
        functools.partial(_front_kernel, tiles_per_seq=seq // tt),
        out_shape=(jax.ShapeDtypeStruct((t, ML_V_W), BF16),
                   jax.ShapeDtypeStruct((t, SW_Q_W), BF16),
                   jax.ShapeDtypeStruct((t, N_GATES), BF16)),
        grid=(nt + 1,),
        in_specs=[
            pl.BlockSpec(memory_space=pltpu.SMEM),
            pl.BlockSpec((tt, D_MODEL), proj_tile),
            const((1, D_MODEL)),
            const((D_MODEL, N_W)),
            const((D_MODEL, 2 * LANES)),
            const((ML_CONV, 2 * ML_QK_W)),
            const((1, LANES)),
            const((1, ML_V_W)),
        ],
        out_specs=(pl.BlockSpec((tt, ML_V_W), mix_tile),
                   pl.BlockSpec((tt, SW_Q_W), mix_tile),
                   pl.BlockSpec((tt, N_GATES), proj_tile)),
        scratch_shapes=[
            pltpu.VMEM((tt, D_MODEL), BF16),
            pltpu.VMEM((tt, N_SLAB), BF16),
            pltpu.VMEM((tt, N_SLAB), BF16),
            pltpu.VMEM((tt, LANES), F32),
            pltpu.VMEM((tt, LANES), F32),
            pltpu.VMEM((W, N_SLAB - COL_KSW), BF16),
            pltpu.VMEM((BF16_SUBLANES, 2 * ML_QK_W), BF16),
            pltpu.VMEM((ML_HEADS, ML_DQK, ML_DV), F32),
            pltpu.VMEM((ML_HEADS, 1, ML_DQK), F32),
            pltpu.VMEM((ML_HEADS, 1, 1), F32),
        ],
        compiler_params=pltpu.CompilerParams(
            dimension_semantics=("arbitrary",),
            vmem_limit_bytes=_vmem_limit(
                ((tt, D_MODEL), F32, 2), ((D_MODEL, N_W), BF16, 1), ((D_MODEL, 2 * LANES), BF16, 1),
                ((tt, ML_V_W), BF16, 2), ((tt, SW_Q_W), BF16, 2), ((tt, N_GATES), BF16, 2),
                ((tt, D_MODEL), BF16, 1), ((tt, N_SLAB), BF16, 2), ((tt, LANES), F32, 2),
                ((W, N_SLAB - COL_KSW), BF16, 1), ((ML_HEADS, ML_DQK, ML_DV), F32, 1))),
        name="front",
    )(sinks, x2d, g, w_main, wg, conv_w, b_if, gn)


def _merge_kernel(x_ref, ya_ref, yb_ref, ga_ref, gb_ref, wa_ref, wb_ref, wo_ref, out_ref):
    groups = [slice(r, r + ROW_GROUP) for r in range(0, TM_MERGE, ROW_GROUP)]
    merged = {}

    def branches(rows):
        a = jax.nn.sigmoid(ga_ref[rows, :].astype(F32)) * _dot(ya_ref[rows, :], wa_ref[...])
        b = jax.nn.sigmoid(gb_ref[rows, :].astype(F32)) * _dot(yb_ref[rows, :], wb_ref[...])
        merged[rows.start] = (a + b).astype(BF16)

    def project(rows):
        out_ref[rows, :] = x_ref[rows, :] + _dot(merged.pop(rows.start), wo_ref[...])

    for i in range(len(groups) + 1):
        if i < len(groups):
            branches(groups[i])
        if i >= 1:
            project(groups[i - 1])


def _merge(x2d, ya, yb, gab, wa, wb, wo):
    t = x2d.shape[0]
    tm = TM_MERGE
    wspec = pl.BlockSpec((D_MODEL, D_MODEL), lambda i: (0, 0), pipeline_mode=pl.Buffered(1))
    return pl.pallas_call(
        _merge_kernel,
        out_shape=jax.ShapeDtypeStruct((t, D_MODEL), F32),
        grid=(t // tm,),
        in_specs=[
            pl.BlockSpec((tm, D_MODEL), lambda i: (i, 0)),
            pl.BlockSpec((tm, D_MODEL), lambda i: (i, 0)),
            pl.BlockSpec((tm, D_MODEL), lambda i: (i, 0)),
            pl.BlockSpec((tm, D_MODEL), lambda i: (i, 0)),
            pl.BlockSpec((tm, D_MODEL), lambda i: (i, 1)),
            wspec, wspec, wspec,
        ],
        out_specs=pl.BlockSpec((tm, D_MODEL), lambda i: (i, 0)),
        compiler_params=pltpu.CompilerParams(
            dimension_semantics=("parallel",),
            vmem_limit_bytes=_vmem_limit(((tm, D_MODEL), F32, 4), ((tm, D_MODEL), BF16, 8),
                                         ((D_MODEL, D_MODEL), BF16, 3))),
        name="merge_out",
    )(x2d, ya, yb, gab, gab, wa, wb, wo)


def _mlp_ple_kernel(x_ref, p_ref, gm_ref, wup_ref, wdn_ref, gp_ref, wg_ref, wp_ref, gf_ref, out_ref,
                    *, final_norm):
    groups = [slice(r, r + ROW_GROUP) for r in range(0, TM_MLP, ROW_GROUP)]
    n_ff = D_FF // FF_CHUNK
    hn, acc = {}, {}

    def stage(g, k):
        rows = groups[g]
        if k == 0:
            x = x_ref[rows, :]
            hn[g] = _rms(x, gm_ref[...]).astype(BF16)
            acc[g] = x
        elif k <= n_ff:
            c = k - 1
            u = _dot(hn[g], wup_ref[:, c * FF_CHUNK:(c + 1) * FF_CHUNK])
            r = jnp.maximum(u, 0.0)
            acc[g] = acc[g] + _dot((r * r).astype(BF16), wdn_ref[c * FF_CHUNK:(c + 1) * FF_CHUNK, :])
        else:
            x_all = acc.pop(g)
            half = ROW_GROUP // 2
            for r0 in range(0, ROW_GROUP, half):
                sub = slice(rows.start + r0, rows.start + r0 + half)
                x = x_all[r0:r0 + half, :]
                gate = jax.nn.sigmoid(_dot(_rms(x, gp_ref[...]).astype(BF16), wg_ref[...]))
                x = x + gate * _dot(p_ref[sub, :].astype(BF16), wp_ref[...])
                if final_norm:
                    x = _rms(x, gf_ref[...])
                out_ref[sub, :] = x

    n_stages = n_ff + 2
    for tick in range(n_stages + MLP_STAGE_LAG * (len(groups) - 1)):
        for g in reversed(range(len(groups))):
            k = tick - MLP_STAGE_LAG * g
            if 0 <= k < n_stages:
                stage(g, k)


def _mlp_ple(x2d, p2d, gm, wup, wdn, gp, wg, wp, gf, final_norm):
    t = x2d.shape[0]
    tm = TM_MLP

    def const(shape):
        return pl.BlockSpec(shape, lambda i: (0, 0), pipeline_mode=pl.Buffered(1))

    return pl.pallas_call(
        functools.partial(_mlp_ple_kernel, final_norm=final_norm),
        out_shape=jax.ShapeDtypeStruct((t, D_MODEL), F32),
        grid=(t // tm,),
        in_specs=[
            pl.BlockSpec((tm, D_MODEL), lambda i: (i, 0)),
            pl.BlockSpec((tm, PLE_DIM), lambda i: (i, 0)),
            const((1, D_MODEL)),
            const((D_MODEL, D_FF)),
            const((D_FF, D_MODEL)),
            const((1, D_MODEL)),
            const((D_MODEL, D_MODEL)),
            const((PLE_DIM, D_MODEL)),
            const((1, D_MODEL)),
        ],
        out_specs=pl.BlockSpec((tm, D_MODEL), lambda i: (i, 0)),
        compiler_params=pltpu.CompilerParams(
            dimension_semantics=("parallel",),
            vmem_limit_bytes=_vmem_limit(((tm, D_MODEL), F32, 4), ((tm, PLE_DIM), F32, 2),
                                         ((D_MODEL, D_FF), BF16, 2), ((D_MODEL, D_MODEL), BF16, 1),
                                         ((PLE_DIM, D_MODEL), BF16, 1))),
        name="mlp_ple",
    )(x2d, p2d, gm, wup, wdn, gp, wg, wp, gf)


def _prep_w_in_kernel(wa_ref, wb_ref, wif_ref, main_ref, wg_ref):
    i = pl.program_id(0)
    n_a = COL_QSW // PREP_COLS

    @pl.when(i < n_a)
    def _():
        main_ref[...] = wa_ref[...].T.astype(BF16)

    @pl.when(i >= n_a)
    def _():
        main_ref[...] = wb_ref[...].T.astype(BF16)

    @pl.when(i == 0)
    def _():
        lane = lax.broadcasted_iota(jnp.int32, (D_MODEL, LANES), 1)
        wg = jnp.where(lane < 2 * ML_HEADS, wif_ref[...].T, 0.0)
        wg_hi, wg_lo = _split_bf16(wg, 2)
        wg_ref[:, :LANES] = wg_hi
        wg_ref[:, LANES:] = wg_lo


def _prep_w_in(w_in):
    n_if = 2 * ML_HEADS
    cols = PREP_COLS
    n_a = COL_QSW // cols
    w_t = jnp.swapaxes(w_in, 0, 1)
    return pl.pallas_call(
        _prep_w_in_kernel,
        out_shape=(jax.ShapeDtypeStruct((D_MODEL, N_W), BF16),
                   jax.ShapeDtypeStruct((D_MODEL, 2 * LANES), BF16)),
        grid=(N_W // cols,),
        in_specs=[
            pl.BlockSpec((cols, D_MODEL), lambda i: (jnp.minimum(i, n_a - 1), 0)),
            pl.BlockSpec((pl.Element(cols), pl.Element(D_MODEL)),
                         lambda i: (pl.multiple_of(COL_QSW + n_if + jnp.maximum(i - n_a, 0) * cols, n_if), 0)),
            pl.BlockSpec((LANES, D_MODEL), lambda i: (COL_QSW // LANES, 0)),
        ],
        out_specs=(pl.BlockSpec((D_MODEL, cols), lambda i: (0, i)),
                   pl.BlockSpec((D_MODEL, 2 * LANES), lambda i: (0, 0))),
        compiler_params=pltpu.CompilerParams(
            dimension_semantics=("arbitrary",),
            vmem_limit_bytes=_vmem_limit(((cols, D_MODEL), F32, 4), ((LANES, D_MODEL), F32, 2),
                                         ((D_MODEL, cols), BF16, 2), ((D_MODEL, 2 * LANES), BF16, 2))),
        name="prep_w_in",
    )(w_t, w_t, w_t)


def kernel(x, p, norm_mix_g, w_in, conv_qk, b_if, mlstm_norm_g, sinks, w_branch_a, w_branch_b, w_out,
           norm_mlp_g, w_up, w_down, norm_ple_g, w_ple_gate, w_ple_proj, final_norm_g):
    batch, seq, _ = x.shape
    depth = w_in.shape[0]
    t = batch * seq
    x2d = x.reshape(t, D_MODEL)
    row = lambda v: v.reshape(1, -1)
    for i in range(depth):
        w_main, wg = _prep_w_in(w_in[i])
        bif = jnp.pad(b_if[i], (0, LANES - 2 * ML_HEADS)).reshape(1, LANES)
        ya, yb, gab = _front(x2d, sinks[i], row(norm_mix_g[i]), w_main, wg, conv_qk[i], bif,
                             row(mlstm_norm_g[i]), seq)
        x2d = _merge(x2d, ya, yb, gab, w_branch_a[i].astype(BF16), w_branch_b[i].astype(BF16),
                     w_out[i].astype(BF16))
        x2d = _mlp_ple(x2d, p[i].reshape(t, PLE_DIM), row(norm_mlp_g[i]), w_up[i].astype(BF16),
                       w_down[i].astype(BF16), row(norm_ple_g[i]), w_ple_gate[i].astype(BF16),
                       w_ple_proj[i].astype(BF16), row(final_norm_g), final_norm=(i == depth - 1))
    return x2d.reshape(batch, seq, D_MODEL)
```

```python
import functools
import math

import jax
import jax.numpy as jnp
from jax import lax
from jax.experimental import pallas as pl
from jax.experimental.pallas import tpu as pltpu

D_MODEL = 1024
PLE_DIM = 256
ML_HEADS = 4
ML_DQK = 128
ML_DV = 256
ML_CONV = 4
SW_Q_HEADS = 16
SW_KV_HEADS = 4
SW_HEAD_DIM = 64
SW_WINDOW = 128
SW_GROUP = SW_Q_HEADS // SW_KV_HEADS
D_FF = 4 * D_MODEL
EPS = 1e-6

ML_QK_W = ML_HEADS * ML_DQK
ML_V_W = ML_HEADS * ML_DV
SW_Q_W = SW_Q_HEADS * SW_HEAD_DIM
SW_KV_W = SW_KV_HEADS * SW_HEAD_DIM

LANES = 128
BF16_SUBLANES = 16
MXU_N = 256
VMEM_BYTES = 64 * 1024 * 1024
VMEM_COMPILER_SHARE = 16 * 1024 * 1024

COL_QK = 0
COL_V = COL_QK + 2 * ML_QK_W
COL_O = COL_V + ML_V_W
COL_QSW = COL_O + ML_V_W
COL_KSW = COL_QSW + SW_Q_W
COL_VSW = COL_KSW + 2 * SW_KV_W
N_SLAB = COL_VSW + 2 * SW_KV_W
N_GATES = 2 * D_MODEL
WCOL_KSW = COL_KSW
WCOL_VSW = WCOL_KSW + SW_KV_W
WCOL_GA = WCOL_VSW + SW_KV_W
N_W = WCOL_GA + N_GATES

ML_CHUNK = 128
ML_SKEW = 1
SW_SKEW = 2
TT_FRONT = 512
PROJ_PIECE = MXU_N
MIXERS_END = 0.86
TM_MERGE = 1024
TM_MLP = 1024
ROW_GROUP = 512
MLP_STAGE_LAG = 2
FF_CHUNK = 1024
PREP_COLS = 512

F32 = jnp.float32
BF16 = jnp.bfloat16


def _vmem_limit(*buffers):
    need = sum(copies * math.prod(shape) * jnp.dtype(dtype).itemsize for shape, dtype, copies in buffers)
    assert need + VMEM_COMPILER_SHARE <= VMEM_BYTES, need
    return need + VMEM_COMPILER_SHARE


def _dot(a, b):
    return jnp.dot(a, b, preferred_element_type=F32)


def _dot_nt(a, b):
    return lax.dot_general(a, b, (((1,), (1,)), ((), ())), preferred_element_type=F32)


def _rms(x, g):
    return x * lax.rsqrt(jnp.mean(x * x, axis=-1, keepdims=True) + EPS) * g


def _split_bf16(x, parts):
    out = []
    r = x
    for _ in range(parts - 1):
        t = r.astype(BF16)
        out.append(t)
        r = r - t.astype(F32)
    out.append(r.astype(BF16))
    return out


def _interleave(*spans):
    tagged = []
    for k, (lst, lo, hi) in enumerate(spans):
        tagged += [(lo + (hi - lo) * (i + 0.5) / len(lst), k, i, f) for i, f in enumerate(lst)]
    return [f for _, _, _, f in sorted(tagged, key=lambda e: e[:3])]


def _projection_steps(g_ref, w_ref, wg_ref, hn_ref, slab_ref, gates_ref, gab_ref):
    lo_half = lax.broadcasted_iota(jnp.int32, (TT_FRONT, LANES), 1) < SW_HEAD_DIM

    def norm(src_ref):
        hn_ref[...] = _rms(src_ref[...], g_ref[...]).astype(BF16)

    def gates():
        hi = _dot(hn_ref[...], wg_ref[...])
        gates_ref[...] = hi[:, :LANES] + hi[:, LANES:]

    def dup_heads(val):
        out = []
        for j in range(val.shape[1] // LANES):
            vj = val[:, j * LANES:(j + 1) * LANES]
            rj = pltpu.roll(vj, SW_HEAD_DIM, axis=1)
            out += [jnp.where(lo_half, vj, rj), jnp.where(lo_half, rj, vj)]
        return jnp.concatenate(out, axis=1)

    def piece(c0):
        def run():
            val = _dot(hn_ref[...], w_ref[:, c0:c0 + PROJ_PIECE])
            if c0 == WCOL_KSW:
                slab_ref[:, COL_KSW:COL_VSW] = dup_heads(val).astype(BF16)
            elif c0 == WCOL_VSW:
                slab_ref[:, COL_VSW:N_SLAB] = dup_heads(val).astype(BF16)
            elif c0 < WCOL_KSW:
                slab_ref[:, c0:c0 + PROJ_PIECE] = val.astype(BF16)
            else:
                gab_ref[:, c0 - WCOL_GA:c0 - WCOL_GA + PROJ_PIECE] = val.astype(BF16)
        return run

    return norm, [piece(c0) for c0 in range(0, N_W, PROJ_PIECE)] + [gates]


def _mlstm_steps(slab_ref, gates_ref, qk_tail_ref, first, convw_ref, bif_ref, gn_ref, y_ref,
                 c_ref, n_ref, m_ref, n_chunks):
    L = ML_CHUNK
    H = BF16_SUBLANES
    taps = ML_CONV - 1
    row = lax.broadcasted_iota(jnp.int32, (L, L), 0)
    col = lax.broadcasted_iota(jnp.int32, (L, L), 1)
    causal = row >= col
    tril = jnp.where(causal, 1.0, 0.0).astype(BF16)
    cw = convw_ref[...]
    bif = bif_ref[...]
    row8 = lax.broadcasted_iota(jnp.int32, (8, 2 * ML_QK_W), 0)

    def chunk_prep(c):
        r0 = c * L
        x_cur = slab_ref[r0:r0 + L, COL_QK:COL_QK + 2 * ML_QK_W].astype(F32)
        if c == 0:
            halo = jnp.where(first, 0.0, qk_tail_ref[...].astype(F32))[H - 8:, :]
        else:
            halo = slab_ref[r0 - H:r0, COL_QK:COL_QK + 2 * ML_QK_W].astype(F32)[H - 8:, :]
        acc = cw[taps:taps + 1, :] * x_cur
        for k in range(1, taps + 1):
            rolled = pltpu.roll(x_cur, k, axis=0)
            top = jnp.where(row8 < k, pltpu.roll(halo, k, axis=0), rolled[:8, :])
            acc = acc + cw[taps - k:taps - k + 1, :] * jnp.concatenate([top, rolled[8:, :]], axis=0)
        act = acc * jax.nn.sigmoid(acc)
        gates = gates_ref[r0:r0 + L, :] + bif
        logf = jnp.minimum(gates, 0.0) - jnp.log1p(jnp.exp(-jnp.abs(gates)))
        bsplit = _dot(tril, jnp.concatenate(_split_bf16(logf, 3), axis=1))
        bcum = bsplit[:, :LANES] + bsplit[:, LANES:2 * LANES] + bsplit[:, 2 * LANES:]
        return dict(q=act[:, :ML_QK_W] * (ML_DQK ** -0.5), k=act[:, ML_QK_W:], gates=gates, bcum=bcum,
                    gates_t=gates.T, bcum_t=bcum.T)

    def qk_part(c, h, pre):
        qf = pre["q"][:, h * ML_DQK:(h + 1) * ML_DQK]
        kf = pre["k"][:, h * ML_DQK:(h + 1) * ML_DQK]
        q = qf.astype(BF16)
        kt = kf.T.astype(BF16)
        return dict(q=q, qf=qf, kf=kf, kt=kt, qk=_dot(q, kt))

    def local_part(c, h, pre, qk):
        r0 = c * L
        fl = ML_HEADS + h
        b_c = pre["bcum"][:, fl:fl + 1]
        i_c = pre["gates"][:, h:h + 1]
        r_r = pre["gates_t"][h:h + 1, :] - pre["bcum_t"][fl:fl + 1, :]
        v = slab_ref[r0:r0 + L, COL_V + h * ML_DV:COL_V + (h + 1) * ML_DV]

        log_d = jnp.where(causal, b_c + r_r, -jnp.inf)
        a = jnp.max(log_d, axis=1, keepdims=True)
        s_loc = qk["qk"] * jnp.exp(log_d - a)
        b_last = pre["bcum"][L - 1:L, fl:fl + 1]
        log_w = b_last - b_c + i_c
        aw = jnp.max(log_w, axis=0, keepdims=True)
        w_loc = jnp.exp(log_w - aw)
        return dict(q=qk["q"], qf=qk["qf"], b_c=b_c, a=a, b_last=b_last, aw=aw,
                    sv=_dot(s_loc.astype(BF16), v), rs=jnp.sum(s_loc, axis=1, keepdims=True),
                    u=_dot(qk["kt"], (w_loc * v.astype(F32)).astype(BF16)),
                    nu=jnp.sum(w_loc * qk["kf"], axis=0, keepdims=True))

    def carried_part(c, h, loc, state):
        r0 = c * L
        ct, n_prev, m_prev = state
        inter = loc["b_c"] + m_prev
        m_t = jnp.maximum(inter, loc["a"])
        w_inter = jnp.exp(inter - m_t)
        f_loc = jnp.exp(loc["a"] - m_t)
        num = w_inter * _dot(loc["q"], ct.astype(BF16)) + f_loc * loc["sv"]
        den = w_inter * jnp.sum(loc["qf"] * n_prev, axis=1, keepdims=True) + f_loc * loc["rs"]
        rden = 1.0 / jnp.maximum(jnp.abs(den), jnp.exp(-m_t))
        ms = jnp.mean(num * num, axis=1, keepdims=True)
        scale = rden * lax.rsqrt(rden * rden * ms + EPS)
        og = jax.nn.sigmoid(slab_ref[r0:r0 + L, COL_O + h * ML_DV:COL_O + (h + 1) * ML_DV].astype(F32))
        y_ref[r0:r0 + L, h * ML_DV:(h + 1) * ML_DV] = (
            og * (num * scale * gn_ref[:, h * ML_DV:(h + 1) * ML_DV])).astype(BF16)
        m_new = jnp.maximum(loc["b_last"] + m_prev, loc["aw"])
        decay = jnp.exp(loc["b_last"] + m_prev - m_new)
        g_loc = jnp.exp(loc["aw"] - m_new)
        return decay * ct + g_loc * loc["u"], decay * n_prev + g_loc * loc["nu"], m_new

    units = [(c, h) for c in range(n_chunks) for h in range(ML_HEADS)]
    n_units = len(units)
    state, pre, qk, loc = {}, {}, {}, {}
    n_ticks = n_units + 2 * ML_SKEW

    def step(i):
        def run():
            if i == 0:
                for h in range(ML_HEADS):
                    state[h] = (c_ref[h], n_ref[h], m_ref[h])
                pre[0] = chunk_prep(0)
            if i + 1 < n_units and units[i + 1][1] == 0:
                pre[units[i + 1][0]] = chunk_prep(units[i + 1][0])
            if i < n_units:
                c, h = units[i]
                qk[i] = qk_part(c, h, pre[c])
            j = i - ML_SKEW
            if 0 <= j < n_units:
                c, h = units[j]
                loc[j] = local_part(c, h, pre[c], qk.pop(j))
            j = i - 2 * ML_SKEW
            if 0 <= j < n_units:
                c, h = units[j]
                state[h] = carried_part(c, h, loc.pop(j), state[h])
            if i == n_ticks - 1:
                for h in range(ML_HEADS):
                    c_ref[h], n_ref[h], m_ref[h] = state[h]
        return run

    return [step(i) for i in range(n_ticks)]


def _swa_steps(slab_ref, kv_tail_ref, first, sinks_ref, y_ref, n_blocks):
    W = SW_WINDOW
    hd = SW_HEAD_DIM
    qi = lax.broadcasted_iota(jnp.int32, (W, 2 * W), 0)
    key = lax.broadcasted_iota(jnp.int32, (W, 2 * W), 1) % W
    from_prev = key > qi
    lo_half = lax.broadcasted_iota(jnp.int32, (W, LANES), 1) < hd
    zero = jnp.zeros((), BF16)
    prev_bias = jnp.where(first, -jnp.inf, 0.0).astype(F32)

    def block_diag(d):
        return jnp.concatenate([jnp.where(lo_half, d, zero), jnp.where(lo_half, zero, d)], axis=0)

    cache = {}

    def kv_block(j, col0, h):
        if (j, col0, h) not in cache:
            if j < 0:
                d = kv_tail_ref[:, col0 - COL_KSW + h * LANES:col0 - COL_KSW + (h + 1) * LANES]
            else:
                d = slab_ref[j * W:(j + 1) * W, col0 + h * LANES:col0 + (h + 1) * LANES]
            cache[j, col0, h] = block_diag(d)
        return cache[j, col0, h]

    units = [(j, h, pair) for j in range(n_blocks) for h in range(SW_KV_HEADS)
             for pair in range(SW_GROUP // 2)]

    def logits_of(j, h, pair):
        c0 = COL_QSW + (h * SW_GROUP + 2 * pair) * hd
        qp = slab_ref[j * W:(j + 1) * W, c0:c0 + LANES] * jnp.asarray(hd ** -0.5, BF16)
        s_prev = _dot_nt(qp, kv_block(j - 1, COL_KSW, h))
        if j == 0:
            s_prev = s_prev + prev_bias
        return jnp.where(from_prev, s_prev, _dot_nt(qp, kv_block(j, COL_KSW, h)))

    def probs_of(lg, h, pair):
        halves = []
        for e in range(2):
            sink = sinks_ref[h * SW_GROUP + 2 * pair + e]
            le = lg[:, e * W:(e + 1) * W]
            m = jnp.maximum(jnp.max(le, axis=1, keepdims=True), sink)
            p = jnp.exp(le - m)
            denom = jnp.sum(p, axis=1, keepdims=True) + jnp.exp(sink - m)
            halves.append((p * (1.0 / denom)).astype(BF16))
        return jnp.concatenate(halves, axis=1)

    def write_out(pb, j, h, pair):
        c0 = (h * SW_GROUP + 2 * pair) * hd
        out = (_dot(jnp.where(from_prev, pb, zero), kv_block(j - 1, COL_VSW, h))
               + _dot(jnp.where(from_prev, zero, pb), kv_block(j, COL_VSW, h)))
        y_ref[j * W:(j + 1) * W, c0:c0 + LANES] = out.astype(BF16)

    logits, probs = {}, {}
    n = len(units)

    def step(i):
        def run():
            if i < n:
                logits[i] = logits_of(*units[i])
            if 0 <= i - SW_SKEW < n:
                _, h, pair = units[i - SW_SKEW]
                probs[i - SW_SKEW] = probs_of(logits.pop(i - SW_SKEW), h, pair)
            if 0 <= i - 2 * SW_SKEW < n:
                write_out(probs.pop(i - 2 * SW_SKEW), *units[i - 2 * SW_SKEW])
        return run

    return [step(i) for i in range(n + 2 * SW_SKEW)]


def _front_kernel(sinks_ref, x_ref, g_ref, w_ref, wg_ref, convw_ref, bif_ref, gn_ref,
                  ya_ref, yb_ref, gab_ref,
                  hn_ref, slab_new_ref, slab_ref, gates_new_ref, gates_ref, kv_tail_ref, qk_tail_ref,
                  c_ref, n_ref, m_ref, *, tiles_per_seq):
    tt = TT_FRONT
    s = pl.program_id(0)
    first = (jnp.maximum(s - 1, 0) % tiles_per_seq) == 0

    @pl.when(s == 0)
    def _():
        slab_ref[...] = jnp.zeros_like(slab_ref)
        gates_ref[...] = jnp.zeros_like(gates_ref)
        kv_tail_ref[...] = jnp.zeros_like(kv_tail_ref)
        qk_tail_ref[...] = jnp.zeros_like(qk_tail_ref)

    @pl.when(first)
    def _():
        c_ref[...] = jnp.zeros_like(c_ref)
        n_ref[...] = jnp.zeros_like(n_ref)
        m_ref[...] = jnp.zeros_like(m_ref)

    norm, pieces = _projection_steps(g_ref, w_ref, wg_ref, hn_ref, slab_new_ref, gates_new_ref, gab_ref)
    ml = _mlstm_steps(slab_ref, gates_ref, qk_tail_ref, first, convw_ref, bif_ref, gn_ref, ya_ref,
                      c_ref, n_ref, m_ref, tt // ML_CHUNK)
    sw = _swa_steps(slab_ref, kv_tail_ref, first, sinks_ref, yb_ref, tt // SW_WINDOW)
    norm(x_ref)
    for thunk in _interleave((pieces, 0.0, 1.0), (ml, 0.0, MIXERS_END), (sw, 0.0, MIXERS_END)):
        thunk()

    kv_tail_ref[...] = slab_ref[tt - SW_WINDOW:tt, COL_KSW:N_SLAB]
    qk_tail_ref[...] = slab_ref[tt - BF16_SUBLANES:tt, COL_QK:COL_QK + 2 * ML_QK_W]
    slab_ref[...] = slab_new_ref[...]
    gates_ref[...] = gates_new_ref[...]


def _front(x2d, sinks, g, w_main, wg, conv_w, b_if, gn, seq):
    t = x2d.shape[0]
    tt = TT_FRONT
    nt = t // tt
    W = SW_WINDOW

    def const(shape):
        return pl.BlockSpec(shape, lambda s: (0, 0), pipeline_mode=pl.Buffered(1))

    proj_tile = lambda s: (jnp.minimum(s, nt - 1), 0)
    mix_tile = lambda s: (jnp.maximum(s - 1, 0), 0)
    return pl.pallas_call(
        functools.partial(_front_kernel, tiles_per_seq=seq // tt),
        out_shape=(jax.ShapeDtypeStruct((t, ML_V_W), BF16),
                   jax.ShapeDtypeStruct((t, SW_Q_W), BF16),
                   jax.ShapeDtypeStruct((t, N_GATES), BF16)),
        grid=(nt + 1,),
        in_specs=[
            pl.BlockSpec(memory_space=pltpu.SMEM),
            pl.BlockSpec((tt, D_MODEL), proj_tile),
            const((1, D_MODEL)),
            const((D_MODEL, N_W)),
            const((D_MODEL, 2 * LANES)),
            const((ML_CONV, 2 * ML_QK_W)),
            const((1, LANES)),
            const((1, ML_V_W)),
        ],
        out_specs=(pl.BlockSpec((tt, ML_V_W), mix_tile),
                   pl.BlockSpec((tt, SW_Q_W), mix_tile),
                   pl.BlockSpec((tt, N_GATES), proj_tile)),
        scratch_shapes=[
            pltpu.VMEM((tt, D_MODEL), BF16),
            pltpu.VMEM((tt, N_SLAB), BF16),
            pltpu.VMEM((tt, N_SLAB), BF16),
            pltpu.VMEM((tt, LANES), F32),
            pltpu.VMEM((tt, LANES), F32),
            pltpu.VMEM((W, N_SLAB - COL_KSW), BF16),
            pltpu.VMEM((BF16_SUBLANES, 2 * ML_QK_W), BF16),
            pltpu.VMEM((ML_HEADS, ML_DQK, ML_DV), F32),
            pltpu.VMEM((ML_HEADS, 1, ML_DQK), F32),
            pltpu.VMEM((ML_HEADS, 1, 1), F32),
        ],
        compiler_params=pltpu.CompilerParams(
            dimension_semantics=("arbitrary",),
            vmem_limit_bytes=_vmem_limit(
                ((tt, D_MODEL), F32, 2), ((D_MODEL, N_W), BF16, 1), ((D_MODEL, 2 * LANES), BF16, 1),
                ((tt, ML_V_W), BF16, 2), ((tt, SW_Q_W), BF16, 2), ((tt, N_GATES), BF16, 2),
                ((tt, D_MODEL), BF16, 1), ((tt, N_SLAB), BF16, 2), ((tt, LANES), F32, 2),
                ((W, N_SLAB - COL_KSW), BF16, 1), ((ML_HEADS, ML_DQK, ML_DV), F32, 1))),
        name="front",
    )(sinks, x2d, g, w_main, wg, conv_w, b_if, gn)


def _merge_kernel(x_ref, ya_ref, yb_ref, ga_ref, gb_ref, wa_ref, wb_ref, wo_ref, out_ref):
    groups = [slice(r, r + ROW_GROUP) for r in range(0, TM_MERGE, ROW_GROUP)]
    merged = {}

    def branches(rows):
        a = jax.nn.sigmoid(ga_ref[rows, :].astype(F32)) * _dot(ya_ref[rows, :], wa_ref[...])
        b = jax.nn.sigmoid(gb_ref[rows, :].astype(F32)) * _dot(yb_ref[rows, :], wb_ref[...])
        merged[rows.start] = (a + b).astype(BF16)

    def project(rows):
        out_ref[rows, :] = x_ref[rows, :] + _dot(merged.pop(rows.start), wo_ref[...])

    for i in range(len(groups) + 1):
        if i < len(groups):
            branches(groups[i])
        if i >= 1:
            project(groups[i - 1])


def _merge(x2d, ya, yb, gab, wa, wb, wo):
    t = x2d.shape[0]
    tm = TM_MERGE
    wspec = pl.BlockSpec((D_MODEL, D_MODEL), lambda i: (0, 0), pipeline_mode=pl.Buffered(1))
    return pl.pallas_call(
        _merge_kernel,
        out_shape=jax.ShapeDtypeStruct((t, D_MODEL), F32),
        grid=(t // tm,),
        in_specs=[
            pl.BlockSpec((tm, D_MODEL), lambda i: (i, 0)),
            pl.BlockSpec((tm, D_MODEL), lambda i: (i, 0)),
            pl.BlockSpec((tm, D_MODEL), lambda i: (i, 0)),
            pl.BlockSpec((tm, D_MODEL), lambda i: (i, 0)),
            pl.BlockSpec((tm, D_MODEL), lambda i: (i, 1)),
            wspec, wspec, wspec,
        ],
        out_specs=pl.BlockSpec((tm, D_MODEL), lambda i: (i, 0)),
        compiler_params=pltpu.CompilerParams(
            dimension_semantics=("parallel",),
            vmem_limit_bytes=_vmem_limit(((tm, D_MODEL), F32, 4), ((tm, D_MODEL), BF16, 8),
                                         ((D_MODEL, D_MODEL), BF16, 3))),
        name="merge_out",
    )(x2d, ya, yb, gab, gab, wa, wb, wo)


def _mlp_ple_kernel(x_ref, p_ref, gm_ref, wup_ref, wdn_ref, gp_ref, wg_ref, wp_ref, gf_ref, out_ref,
                    *, final_norm):
    groups = [slice(r, r + ROW_GROUP) for r in range(0, TM_MLP, ROW_GROUP)]
    n_ff = D_FF // FF_CHUNK
    hn, acc = {}, {}

    def stage(g, k):
        rows = groups[g]
        if k == 0:
            x = x_ref[rows, :]
            hn[g] = _rms(x, gm_ref[...]).astype(BF16)
            acc[g] = x
        elif k <= n_ff:
            c = k - 1
            u = _dot(hn[g], wup_ref[:, c * FF_CHUNK:(c + 1) * FF_CHUNK])
            r = jnp.maximum(u, 0.0)
            acc[g] = acc[g] + _dot((r * r).astype(BF16), wdn_ref[c * FF_CHUNK:(c + 1) * FF_CHUNK, :])
        else:
            x_all = acc.pop(g)
            half = ROW_GROUP // 2
            for r0 in range(0, ROW_GROUP, half):
                sub = slice(rows.start + r0, rows.start + r0 + half)
                x = x_all[r0:r0 + half, :]
                gate = jax.nn.sigmoid(_dot(_rms(x, gp_ref[...]).astype(BF16), wg_ref[...]))
                x = x + gate * _dot(p_ref[sub, :].astype(BF16), wp_ref[...])
                if final_norm:
                    x = _rms(x, gf_ref[...])
                out_ref[sub, :] = x

    n_stages = n_ff + 2
    for tick in range(n_stages + MLP_STAGE_LAG * (len(groups) - 1)):
        for g in reversed(range(len(groups))):
            k = tick - MLP_STAGE_LAG * g
            if 0 <= k < n_stages:
                stage(g, k)


def _mlp_ple(x2d, p2d, gm, wup, wdn, gp, wg, wp, gf, final_norm):
    t = x2d.shape[0]
    tm = TM_MLP

    def const(shape):
        return pl.BlockSpec(shape, lambda i: (0, 0), pipeline_mode=pl.Buffered(1))

    return pl.pallas_call(
        functools.partial(_mlp_ple_kernel, final_norm=final_norm),
        out_shape=jax.ShapeDtypeStruct((t, D_MODEL), F32),
        grid=(t // tm,),
        in_specs=[
            pl.BlockSpec((tm, D_MODEL), lambda i: (i, 0)),
            pl.BlockSpec((tm, PLE_DIM), lambda i: (i, 0)),
            const((1, D_MODEL)),
            const((D_MODEL, D_FF)),
            const((D_FF, D_MODEL)),
            const((1, D_MODEL)),
            const((D_MODEL, D_MODEL)),
            const((PLE_DIM, D_MODEL)),
            const((1, D_MODEL)),
        ],
        out_specs=pl.BlockSpec((tm, D_MODEL), lambda i: (i, 0)),
        compiler_params=pltpu.CompilerParams(
            dimension_semantics=("parallel",),
            vmem_limit_bytes=_vmem_limit(((tm, D_MODEL), F32, 4), ((tm, PLE_DIM), F32, 2),
                                         ((D_MODEL, D_FF), BF16, 2), ((D_MODEL, D_MODEL), BF16, 1),
                                         ((PLE_DIM, D_MODEL), BF16, 1))),
        name="mlp_ple",
    )(x2d, p2d, gm, wup, wdn, gp, wg, wp, gf)


def _prep_w_in_kernel(wa_ref, wb_ref, wif_ref, main_ref, wg_ref):
    i = pl.program_id(0)
    n_a = COL_QSW // PREP_COLS

    @pl.when(i < n_a)
    def _():
        main_ref[...] = wa_ref[...].T.astype(BF16)

    @pl.when(i >= n_a)
    def _():
        main_ref[...] = wb_ref[...].T.astype(BF16)

    @pl.when(i == 0)
    def _():
        lane = lax.broadcasted_iota(jnp.int32, (D_MODEL, LANES), 1)
        wg = jnp.where(lane < 2 * ML_HEADS, wif_ref[...].T, 0.0)
        wg_hi, wg_lo = _split_bf16(wg, 2)
        wg_ref[:, :LANES] = wg_hi
        wg_ref[:, LANES:] = wg_lo


def _prep_w_in(w_in):
    n_if = 2 * ML_HEADS
    cols = PREP_COLS
    n_a = COL_QSW // cols
    w_t = jnp.swapaxes(w_in, 0, 1)
    return pl.pallas_call(
        _prep_w_in_kernel,
        out_shape=(jax.ShapeDtypeStruct((D_MODEL, N_W), BF16),
                   jax.ShapeDtypeStruct((D_MODEL, 2 * LANES), BF16)),
        grid=(N_W // cols,),
        in_specs=[
            pl.BlockSpec((cols, D_MODEL), lambda i: (jnp.minimum(i, n_a - 1), 0)),
            pl.BlockSpec((pl.Element(cols), pl.Element(D_MODEL)),
                         lambda i: (pl.multiple_of(COL_QSW + n_if + jnp.maximum(i - n_a, 0) * cols, n_if), 0)),
            pl.BlockSpec((LANES, D_MODEL), lambda i: (COL_QSW // LANES, 0)),
        ],
        out_specs=(pl.BlockSpec((D_MODEL, cols), lambda i: (0, i)),
                   pl.BlockSpec((D_MODEL, 2 * LANES), lambda i: (0, 0))),
        compiler_params=pltpu.CompilerParams(
            dimension_semantics=("arbitrary",),
            vmem_limit_bytes=_vmem_limit(((cols, D_MODEL), F32, 4), ((LANES, D_MODEL), F32, 2),
                                         ((D_MODEL, cols), BF16, 2), ((D_MODEL, 2 * LANES), BF16, 2))),
        name="prep_w_in",
    )(w_t, w_t, w_t)


def kernel(x, p, norm_mix_g, w_in, conv_qk, b_if, mlstm_norm_g, sinks, w_branch_a, w_branch_b, w_out,
           norm_mlp_g, w_up, w_down, norm_ple_g, w_ple_gate, w_ple_proj, final_norm_g):
    batch, seq, _ = x.shape
    depth = w_in.shape[0]
    t = batch * seq
    x2d = x.reshape(t, D_MODEL)
    row = lambda v: v.reshape(1, -1)
    for i in range(depth):
        w_main, wg = _prep_w_in(w_in[i])
        bif = jnp.pad(b_if[i], (0, LANES - 2 * ML_HEADS)).reshape(1, LANES)
        ya, yb, gab = _front(x2d, sinks[i], row(norm_mix_g[i]), w_main, wg, conv_qk[i], bif,
                             row(mlstm_norm_g[i]), seq)
        x2d = _merge(x2d, ya, yb, gab, w_branch_a[i].astype(BF16), w_branch_b[i].astype(BF16),
                     w_out[i].astype(BF16))
        x2d = _mlp_ple(x2d, p[i].reshape(t, PLE_DIM), row(norm_mlp_g[i]), w_up[i].astype(BF16),
                       w_down[i].astype(BF16), row(norm_ple_g[i]), w_ple_gate[i].astype(BF16),
                       w_ple_proj[i].astype(BF16), row(final_norm_g), final_norm=(i == depth - 1))
    return x2d.reshape(batch, seq, D_MODEL)
```

```python
import functools
import math

import jax
import jax.numpy as jnp
from jax import lax
from jax.experimental import pallas as pl
from jax.experimental.pallas import tpu as pltpu

D_MODEL = 1024
PLE_DIM = 256
ML_HEADS = 4
ML_DQK = 128
ML_DV = 256
ML_CONV = 4
SW_Q_HEADS = 16
SW_KV_HEADS = 4
SW_HEAD_DIM = 64
SW_WINDOW = 128
SW_GROUP = SW_Q_HEADS // SW_KV_HEADS
D_FF = 4 * D_MODEL
EPS = 1e-6

ML_QK_W = ML_HEADS * ML_DQK
ML_V_W = ML_HEADS * ML_DV
SW_Q_W = SW_Q_HEADS * SW_HEAD_DIM
SW_KV_W = SW_KV_HEADS * SW_HEAD_DIM

LANES = 128
BF16_SUBLANES = 16
MXU_N = 256
VMEM_BYTES = 64 * 1024 * 1024
VMEM_COMPILER_SHARE = 16 * 1024 * 1024

COL_QK = 0
COL_V = COL_QK + 2 * ML_QK_W
COL_O = COL_V + ML_V_W
COL_QSW = COL_O + ML_V_W
COL_KSW = COL_QSW + SW_Q_W
COL_VSW = COL_KSW + 2 * SW_KV_W
N_SLAB = COL_VSW + 2 * SW_KV_W
N_GATES = 2 * D_MODEL
WCOL_KSW = COL_KSW
WCOL_VSW = WCOL_KSW + SW_KV_W
WCOL_GA = WCOL_VSW + SW_KV_W
N_W = WCOL_GA + N_GATES

ML_CHUNK = 128
ML_SKEW = 1
SW_SKEW = 2
TT_FRONT = 512
PROJ_PIECE = MXU_N
ACT_BUFFERS = 2
MIXERS_END = 0.9
TM_MERGE = 1024
TM_MLP = 1024
ROW_GROUP = 512
MLP_STAGE_LAG = 4
FF_CHUNK = 512
PREP_COLS = 512

F32 = jnp.float32
BF16 = jnp.bfloat16


def _vmem_limit(*buffers):
    need = sum(copies * math.prod(shape) * jnp.dtype(dtype).itemsize for shape, dtype, copies in buffers)
    assert need + VMEM_COMPILER_SHARE <= VMEM_BYTES, need
    return need + VMEM_COMPILER_SHARE


def _dot(a, b):
    return jnp.dot(a, b, preferred_element_type=F32)


def _dot_nt(a, b):
    return lax.dot_general(a, b, (((1,), (1,)), ((), ())), preferred_element_type=F32)


def _rms(x, g):
    return x * lax.rsqrt(jnp.mean(x * x, axis=-1, keepdims=True) + EPS) * g


def _split_bf16(x, parts):
    out = []
    r = x
    for _ in range(parts - 1):
        t = r.astype(BF16)
        out.append(t)
        r = r - t.astype(F32)
    out.append(r.astype(BF16))
    return out


def _interleave(*spans):
    tagged = []
    for k, (lst, lo, hi) in enumerate(spans):
        tagged += [(lo + (hi - lo) * (i + 0.5) / len(lst), k, i, f) for i, f in enumerate(lst)]
    return [f for _, _, _, f in sorted(tagged, key=lambda e: e[:3])]


def _projection_steps(g_ref, w_ref, wg_ref, hn_ref, slab_ref, gates_ref, gab_ref):
    lo_half = lax.broadcasted_iota(jnp.int32, (TT_FRONT, LANES), 1) < SW_HEAD_DIM

    def norm(src_ref):
        hn_ref[...] = _rms(src_ref[...], g_ref[...]).astype(BF16)

    def gates():
        hi = _dot(hn_ref[...], wg_ref[...])
        gates_ref[...] = hi[:, :LANES] + hi[:, LANES:]

    def dup_heads(val):
        out = []
        for j in range(val.shape[1] // LANES):
            vj = val[:, j * LANES:(j + 1) * LANES]
            rj = pltpu.roll(vj, SW_HEAD_DIM, axis=1)
            out += [jnp.where(lo_half, vj, rj), jnp.where(lo_half, rj, vj)]
        return jnp.concatenate(out, axis=1)

    def piece(c0):
        def run():
            val = _dot(hn_ref[...], w_ref[:, c0:c0 + PROJ_PIECE])
            if c0 == WCOL_KSW:
                slab_ref[:, COL_KSW:COL_VSW] = dup_heads(val).astype(BF16)
            elif c0 == WCOL_VSW:
                slab_ref[:, COL_VSW:N_SLAB] = dup_heads(val).astype(BF16)
            elif c0 < WCOL_KSW:
                slab_ref[:, c0:c0 + PROJ_PIECE] = val.astype(BF16)
            else:
                gab_ref[:, c0 - WCOL_GA:c0 - WCOL_GA + PROJ_PIECE] = val.astype(BF16)
        return run

    return norm, [piece(c0) for c0 in range(0, N_W, PROJ_PIECE)] + [gates]


def _mlstm_steps(slab_ref, gates_ref, qk_tail_ref, first, convw_ref, bif_ref, gn_ref, y_ref,
                 c_ref, n_ref, m_ref, n_chunks):
    L = ML_CHUNK
    H = BF16_SUBLANES
    taps = ML_CONV - 1
    row = lax.broadcasted_iota(jnp.int32, (L, L), 0)
    col = lax.broadcasted_iota(jnp.int32, (L, L), 1)
    causal = row >= col
    tril = jnp.where(causal, 1.0, 0.0).astype(BF16)
    cw = convw_ref[...]
    bif = bif_ref[...]
    row8 = lax.broadcasted_iota(jnp.int32, (8, 2 * ML_QK_W), 0)

    def chunk_prep(c):
        r0 = c * L
        x_cur = slab_ref[r0:r0 + L, COL_QK:COL_QK + 2 * ML_QK_W].astype(F32)
        if c == 0:
            halo = jnp.where(first, 0.0, qk_tail_ref[...].astype(F32))[H - 8:, :]
        else:
            halo = slab_ref[r0 - H:r0, COL_QK:COL_QK + 2 * ML_QK_W].astype(F32)[H - 8:, :]
        acc = cw[taps:taps + 1, :] * x_cur
        for k in range(1, taps + 1):
            rolled = pltpu.roll(x_cur, k, axis=0)
            top = jnp.where(row8 < k, pltpu.roll(halo, k, axis=0), rolled[:8, :])
            acc = acc + cw[taps - k:taps - k + 1, :] * jnp.concatenate([top, rolled[8:, :]], axis=0)
        act = acc * jax.nn.sigmoid(acc)
        gates = gates_ref[r0:r0 + L, :] + bif
        logf = jnp.minimum(gates, 0.0) - jnp.log1p(jnp.exp(-jnp.abs(gates)))
        bsplit = _dot(tril, jnp.concatenate(_split_bf16(logf, 3), axis=1))
        bcum = bsplit[:, :LANES] + bsplit[:, LANES:2 * LANES] + bsplit[:, 2 * LANES:]
        return dict(q=act[:, :ML_QK_W] * (ML_DQK ** -0.5), k=act[:, ML_QK_W:], gates=gates, bcum=bcum,
                    gates_t=gates.T, bcum_t=bcum.T)

    def qk_part(c, h, pre):
        qf = pre["q"][:, h * ML_DQK:(h + 1) * ML_DQK]
        kf = pre["k"][:, h * ML_DQK:(h + 1) * ML_DQK]
        q = qf.astype(BF16)
        kt = kf.T.astype(BF16)
        return dict(q=q, qf=qf, kf=kf, kt=kt, qk=_dot(q, kt))

    def local_part(c, h, pre, qk):
        r0 = c * L
        fl = ML_HEADS + h
        b_c = pre["bcum"][:, fl:fl + 1]
        i_c = pre["gates"][:, h:h + 1]
        r_r = pre["gates_t"][h:h + 1, :] - pre["bcum_t"][fl:fl + 1, :]
        v = slab_ref[r0:r0 + L, COL_V + h * ML_DV:COL_V + (h + 1) * ML_DV]

        log_d = jnp.where(causal, b_c + r_r, -jnp.inf)
        a = jnp.max(log_d, axis=1, keepdims=True)
        s_loc = qk["qk"] * jnp.exp(log_d - a)
        b_last = pre["bcum"][L - 1:L, fl:fl + 1]
        log_w = b_last - b_c + i_c
        aw = jnp.max(log_w, axis=0, keepdims=True)
        w_loc = jnp.exp(log_w - aw)
        return dict(q=qk["q"], qf=qk["qf"], b_c=b_c, a=a, b_last=b_last, aw=aw,
                    sv=_dot(s_loc.astype(BF16), v), rs=jnp.sum(s_loc, axis=1, keepdims=True),
                    u=_dot(qk["kt"], (w_loc * v.astype(F32)).astype(BF16)),
                    nu=jnp.sum(w_loc * qk["kf"], axis=0, keepdims=True))

    def carried_part(c, h, loc, state):
        r0 = c * L
        ct, n_prev, m_prev = state
        inter = loc["b_c"] + m_prev
        m_t = jnp.maximum(inter, loc["a"])
        w_inter = jnp.exp(inter - m_t)
        f_loc = jnp.exp(loc["a"] - m_t)
        num = w_inter * _dot(loc["q"], ct.astype(BF16)) + f_loc * loc["sv"]
        den = w_inter * jnp.sum(loc["qf"] * n_prev, axis=1, keepdims=True) + f_loc * loc["rs"]
        rden = 1.0 / jnp.maximum(jnp.abs(den), jnp.exp(-m_t))
        ms = jnp.mean(num * num, axis=1, keepdims=True)
        scale = rden * lax.rsqrt(rden * rden * ms + EPS)
        og = jax.nn.sigmoid(slab_ref[r0:r0 + L, COL_O + h * ML_DV:COL_O + (h + 1) * ML_DV].astype(F32))
        y_ref[r0:r0 + L, h * ML_DV:(h + 1) * ML_DV] = (
            og * (num * scale * gn_ref[:, h * ML_DV:(h + 1) * ML_DV])).astype(BF16)
        m_new = jnp.maximum(loc["b_last"] + m_prev, loc["aw"])
        decay = jnp.exp(loc["b_last"] + m_prev - m_new)
        g_loc = jnp.exp(loc["aw"] - m_new)
        return decay * ct + g_loc * loc["u"], decay * n_prev + g_loc * loc["nu"], m_new

    units = [(c, h) for c in range(n_chunks) for h in range(ML_HEADS)]
    n_units = len(units)
    state, pre, qk, loc = {}, {}, {}, {}
    n_ticks = n_units + 2 * ML_SKEW

    def step(i):
        def run():
            if i == 0:
                for h in range(ML_HEADS):
                    state[h] = (c_ref[h], n_ref[h], m_ref[h])
                pre[0] = chunk_prep(0)
            if i + 1 < n_units and units[i + 1][1] == 0:
                pre[units[i + 1][0]] = chunk_prep(units[i + 1][0])
            if i < n_units:
                c, h = units[i]
                qk[i] = qk_part(c, h, pre[c])
            j = i - ML_SKEW
            if 0 <= j < n_units:
                c, h = units[j]
                loc[j] = local_part(c, h, pre[c], qk.pop(j))
            j = i - 2 * ML_SKEW
            if 0 <= j < n_units:
                c, h = units[j]
                state[h] = carried_part(c, h, loc.pop(j), state[h])
            if i == n_ticks - 1:
                for h in range(ML_HEADS):
                    c_ref[h], n_ref[h], m_ref[h] = state[h]
        return run

    return [step(i) for i in range(n_ticks)]


def _swa_steps(slab_ref, kv_tail_ref, first, sinks_ref, y_ref, n_blocks):
    W = SW_WINDOW
    hd = SW_HEAD_DIM
    qi = lax.broadcasted_iota(jnp.int32, (W, 2 * W), 0)
    key = lax.broadcasted_iota(jnp.int32, (W, 2 * W), 1) % W
    from_prev = key > qi
    lo_half = lax.broadcasted_iota(jnp.int32, (W, LANES), 1) < hd
    zero = jnp.zeros((), BF16)
    prev_bias = jnp.where(first, -jnp.inf, 0.0).astype(F32)

    def block_diag(d):
        return jnp.concatenate([jnp.where(lo_half, d, zero), jnp.where(lo_half, zero, d)], axis=0)

    cache = {}

    def kv_block(j, col0, h):
        if (j, col0, h) not in cache:
            if j < 0:
                d = kv_tail_ref[:, col0 - COL_KSW + h * LANES:col0 - COL_KSW + (h + 1) * LANES]
            else:
                d = slab_ref[j * W:(j + 1) * W, col0 + h * LANES:col0 + (h + 1) * LANES]
            cache[j, col0, h] = block_diag(d)
        return cache[j, col0, h]

    units = [(j, h, pair) for j in range(n_blocks) for h in range(SW_KV_HEADS)
             for pair in range(SW_GROUP // 2)]

    def logits_of(j, h, pair):
        c0 = COL_QSW + (h * SW_GROUP + 2 * pair) * hd
        qp = slab_ref[j * W:(j + 1) * W, c0:c0 + LANES] * jnp.asarray(hd ** -0.5, BF16)
        s_prev = _dot_nt(qp, kv_block(j - 1, COL_KSW, h))
        if j == 0:
            s_prev = s_prev + prev_bias
        return jnp.where(from_prev, s_prev, _dot_nt(qp, kv_block(j, COL_KSW, h)))

    def probs_of(lg, h, pair):
        halves = []
        for e in range(2):
            sink = sinks_ref[h * SW_GROUP + 2 * pair + e]
            le = lg[:, e * W:(e + 1) * W]
            m = jnp.maximum(jnp.max(le, axis=1, keepdims=True), sink)
            p = jnp.exp(le - m)
            denom = jnp.sum(p, axis=1, keepdims=True) + jnp.exp(sink - m)
            halves.append((p * (1.0 / denom)).astype(BF16))
        return jnp.concatenate(halves, axis=1)

    def write_out(pb, j, h, pair):
        c0 = (h * SW_GROUP + 2 * pair) * hd
        out = (_dot(jnp.where(from_prev, pb, zero), kv_block(j - 1, COL_VSW, h))
               + _dot(jnp.where(from_prev, zero, pb), kv_block(j, COL_VSW, h)))
        y_ref[j * W:(j + 1) * W, c0:c0 + LANES] = out.astype(BF16)

    logits, probs = {}, {}
    n = len(units)

    def step(i):
        def run():
            if i < n:
                logits[i] = logits_of(*units[i])
            if 0 <= i - SW_SKEW < n:
                _, h, pair = units[i - SW_SKEW]
                probs[i - SW_SKEW] = probs_of(logits.pop(i - SW_SKEW), h, pair)
            if 0 <= i - 2 * SW_SKEW < n:
                write_out(probs.pop(i - 2 * SW_SKEW), *units[i - 2 * SW_SKEW])
        return run

    return [step(i) for i in range(n + 2 * SW_SKEW)]


def _front_kernel(sinks_ref, x_ref, g_ref, w_ref, wg_ref, convw_ref, bif_ref, gn_ref,
                  ya_ref, yb_ref, gab_ref,
                  hn_ref, slab_new_ref, slab_ref, gates_new_ref, gates_ref, kv_tail_ref, qk_tail_ref,
                  c_ref, n_ref, m_ref, *, tiles_per_seq):
    tt = TT_FRONT
    s = pl.program_id(0)
    first = (jnp.maximum(s - 1, 0) % tiles_per_seq) == 0

    @pl.when(s == 0)
    def _():
        slab_ref[...] = jnp.zeros_like(slab_ref)
        gates_ref[...] = jnp.zeros_like(gates_ref)
        kv_tail_ref[...] = jnp.zeros_like(kv_tail_ref)
        qk_tail_ref[...] = jnp.zeros_like(qk_tail_ref)

    @pl.when(first)
    def _():
        c_ref[...] = jnp.zeros_like(c_ref)
        n_ref[...] = jnp.zeros_like(n_ref)
        m_ref[...] = jnp.zeros_like(m_ref)

    norm, pieces = _projection_steps(g_ref, w_ref, wg_ref, hn_ref, slab_new_ref, gates_new_ref, gab_ref)
    ml = _mlstm_steps(slab_ref, gates_ref, qk_tail_ref, first, convw_ref, bif_ref, gn_ref, ya_ref,
                      c_ref, n_ref, m_ref, tt // ML_CHUNK)
    sw = _swa_steps(slab_ref, kv_tail_ref, first, sinks_ref, yb_ref, tt // SW_WINDOW)
    norm(x_ref)
    for thunk in _interleave((pieces, 0.0, 1.0), (ml, 0.0, MIXERS_END), (sw, 0.0, MIXERS_END)):
        thunk()

    kv_tail_ref[...] = slab_ref[tt - SW_WINDOW:tt, COL_KSW:N_SLAB]
    qk_tail_ref[...] = slab_ref[tt - BF16_SUBLANES:tt, COL_QK:COL_QK + 2 * ML_QK_W]
    slab_ref[...] = slab_new_ref[...]
    gates_ref[...] = gates_new_ref[...]


def _front(x2d, sinks, g, w_main, wg, conv_w, b_if, gn, seq):
    t = x2d.shape[0]
    tt = TT_FRONT
    nt = t // tt
    W = SW_WINDOW

    def const(shape):
        return pl.BlockSpec(shape, lambda s: (0, 0), pipeline_mode=pl.Buffered(1))

    proj_tile = lambda s: (jnp.minimum(s, nt - 1), 0)
    mix_tile = lambda s: (jnp.maximum(s - 1, 0), 0)
    return pl.pallas_call(
        functools.partial(_front_kernel, tiles_per_seq=seq // tt),
        out_shape=(jax.ShapeDtypeStruct((t, ML_V_W), BF16),
                   jax.ShapeDtypeStruct((t, SW_Q_W), BF16),
                   jax.ShapeDtypeStruct((t, N_GATES), BF16)),
        grid=(nt + 1,),
        in_specs=[
            pl.BlockSpec(memory_space=pltpu.SMEM),
            pl.BlockSpec((tt, D_MODEL), proj_tile, pipeline_mode=pl.Buffered(ACT_BUFFERS)),
            const((1, D_MODEL)),
            const((D_MODEL, N_W)),
            const((D_MODEL, 2 * LANES)),
            const((ML_CONV, 2 * ML_QK_W)),
            const((1, LANES)),
            const((1, ML_V_W)),
        ],
        out_specs=(pl.BlockSpec((tt, ML_V_W), mix_tile),
                   pl.BlockSpec((tt, SW_Q_W), mix_tile),
                   pl.BlockSpec((tt, N_GATES), proj_tile)),
        scratch_shapes=[
            pltpu.VMEM((tt, D_MODEL), BF16),
            pltpu.VMEM((tt, N_SLAB), BF16),
            pltpu.VMEM((tt, N_SLAB), BF16),
            pltpu.VMEM((tt, LANES), F32),
            pltpu.VMEM((tt, LANES), F32),
            pltpu.VMEM((W, N_SLAB - COL_KSW), BF16),
            pltpu.VMEM((BF16_SUBLANES, 2 * ML_QK_W), BF16),
            pltpu.VMEM((ML_HEADS, ML_DQK, ML_DV), F32),
            pltpu.VMEM((ML_HEADS, 1, ML_DQK), F32),
            pltpu.VMEM((ML_HEADS, 1, 1), F32),
        ],
        compiler_params=pltpu.CompilerParams(
            dimension_semantics=("arbitrary",),
            vmem_limit_bytes=_vmem_limit(
                ((tt, D_MODEL), F32, ACT_BUFFERS), ((D_MODEL, N_W), BF16, 1), ((D_MODEL, 2 * LANES), BF16, 1),
                ((tt, ML_V_W), BF16, 2), ((tt, SW_Q_W), BF16, 2), ((tt, N_GATES), BF16, 2),
                ((tt, D_MODEL), BF16, 1), ((tt, N_SLAB), BF16, 2), ((tt, LANES), F32, 2),
                ((W, N_SLAB - COL_KSW), BF16, 1), ((ML_HEADS, ML_DQK, ML_DV), F32, 1))),
        name="front",
    )(sinks, x2d, g, w_main, wg, conv_w, b_if, gn)


def _merge_kernel(x_ref, ya_ref, yb_ref, ga_ref, gb_ref, wa_ref, wb_ref, wo_ref, out_ref):
    groups = [slice(r, r + ROW_GROUP) for r in range(0, TM_MERGE, ROW_GROUP)]
    merged = {}

    def branches(rows):
        a = jax.nn.sigmoid(ga_ref[rows, :].astype(F32)) * _dot(ya_ref[rows, :], wa_ref[...])
        b = jax.nn.sigmoid(gb_ref[rows, :].astype(F32)) * _dot(yb_ref[rows, :], wb_ref[...])
        merged[rows.start] = (a + b).astype(BF16)

    def project(rows):
        out_ref[rows, :] = x_ref[rows, :] + _dot(merged.pop(rows.start), wo_ref[...])

    for i in range(len(groups) + 1):
        if i < len(groups):
            branches(groups[i])
        if i >= 1:
            project(groups[i - 1])


def _merge(x2d, ya, yb, gab, wa, wb, wo):
    t = x2d.shape[0]
    tm = TM_MERGE
    wspec = pl.BlockSpec((D_MODEL, D_MODEL), lambda i: (0, 0), pipeline_mode=pl.Buffered(1))
    return pl.pallas_call(
        _merge_kernel,
        out_shape=jax.ShapeDtypeStruct((t, D_MODEL), F32),
        grid=(t // tm,),
        in_specs=[
            pl.BlockSpec((tm, D_MODEL), lambda i: (i, 0), pipeline_mode=pl.Buffered(ACT_BUFFERS)),
            pl.BlockSpec((tm, D_MODEL), lambda i: (i, 0)),
            pl.BlockSpec((tm, D_MODEL), lambda i: (i, 0)),
            pl.BlockSpec((tm, D_MODEL), lambda i: (i, 0)),
            pl.BlockSpec((tm, D_MODEL), lambda i: (i, 1)),
            wspec, wspec, wspec,
        ],
        out_specs=pl.BlockSpec((tm, D_MODEL), lambda i: (i, 0)),
        compiler_params=pltpu.CompilerParams(
            dimension_semantics=("parallel",),
            vmem_limit_bytes=_vmem_limit(((tm, D_MODEL), F32, 2 + ACT_BUFFERS), ((tm, D_MODEL), BF16, 8),
                                         ((D_MODEL, D_MODEL), BF16, 3))),
        name="merge_out",
    )(x2d, ya, yb, gab, gab, wa, wb, wo)


def _mlp_ple_kernel(x_ref, p_ref, gm_ref, wup_ref, wdn_ref, gp_ref, wg_ref, wp_ref, gf_ref, out_ref,
                    *, final_norm):
    groups = [slice(r, r + ROW_GROUP) for r in range(0, TM_MLP, ROW_GROUP)]
    n_ff = D_FF // FF_CHUNK
    hn, acc = {}, {}

    def stage(g, k):
        rows = groups[g]
        if k == 0:
            x = x_ref[rows, :]
            hn[g] = _rms(x, gm_ref[...]).astype(BF16)
            acc[g] = x
        elif k <= n_ff:
            c = k - 1
            u = _dot(hn[g], wup_ref[:, c * FF_CHUNK:(c + 1) * FF_CHUNK])
            r = jnp.maximum(u, 0.0)
            acc[g] = acc[g] + _dot((r * r).astype(BF16), wdn_ref[c * FF_CHUNK:(c + 1) * FF_CHUNK, :])
        else:
            x_all = acc.pop(g)
            half = ROW_GROUP // 2
            for r0 in range(0, ROW_GROUP, half):
                sub = slice(rows.start + r0, rows.start + r0 + half)
                x = x_all[r0:r0 + half, :]
                gate = jax.nn.sigmoid(_dot(_rms(x, gp_ref[...]).astype(BF16), wg_ref[...]))
                x = x + gate * _dot(p_ref[sub, :].astype(BF16), wp_ref[...])
                if final_norm:
                    x = _rms(x, gf_ref[...])
                out_ref[sub, :] = x

    n_stages = n_ff + 2
    for tick in range(n_stages + MLP_STAGE_LAG * (len(groups) - 1)):
        for g in reversed(range(len(groups))):
            k = tick - MLP_STAGE_LAG * g
            if 0 <= k < n_stages:
                stage(g, k)


def _mlp_ple(x2d, p2d, gm, wup, wdn, gp, wg, wp, gf, final_norm):
    t = x2d.shape[0]
    tm = TM_MLP

    def const(shape):
        return pl.BlockSpec(shape, lambda i: (0, 0), pipeline_mode=pl.Buffered(1))

    return pl.pallas_call(
        functools.partial(_mlp_ple_kernel, final_norm=final_norm),
        out_shape=jax.ShapeDtypeStruct((t, D_MODEL), F32),
        grid=(t // tm,),
        in_specs=[
            pl.BlockSpec((tm, D_MODEL), lambda i: (i, 0), pipeline_mode=pl.Buffered(ACT_BUFFERS)),
            pl.BlockSpec((tm, PLE_DIM), lambda i: (i, 0)),
            const((1, D_MODEL)),
            const((D_MODEL, D_FF)),
            const((D_FF, D_MODEL)),
            const((1, D_MODEL)),
            const((D_MODEL, D_MODEL)),
            const((PLE_DIM, D_MODEL)),
            const((1, D_MODEL)),
        ],
        out_specs=pl.BlockSpec((tm, D_MODEL), lambda i: (i, 0)),
        compiler_params=pltpu.CompilerParams(
            dimension_semantics=("parallel",),
            vmem_limit_bytes=_vmem_limit(((tm, D_MODEL), F32, 2 + ACT_BUFFERS), ((tm, PLE_DIM), F32, 2),
                                         ((D_MODEL, D_FF), BF16, 2), ((D_MODEL, D_MODEL), BF16, 1),
                                         ((PLE_DIM, D_MODEL), BF16, 1))),
        name="mlp_ple",
    )(x2d, p2d, gm, wup, wdn, gp, wg, wp, gf)


def _prep_w_in_kernel(wa_ref, wb_ref, wif_ref, main_ref, wg_ref):
    i = pl.program_id(0)
    n_a = COL_QSW // PREP_COLS

    @pl.when(i < n_a)
    def _():
        main_ref[...] = wa_ref[...].T.astype(BF16)

    @pl.when(i >= n_a)
    def _():
        main_ref[...] = wb_ref[...].T.astype(BF16)

    @pl.when(i == 0)
    def _():
        lane = lax.broadcasted_iota(jnp.int32, (D_MODEL, LANES), 1)
        wg = jnp.where(lane < 2 * ML_HEADS, wif_ref[...].T, 0.0)
        wg_hi, wg_lo = _split_bf16(wg, 2)
        wg_ref[:, :LANES] = wg_hi
        wg_ref[:, LANES:] = wg_lo


def _prep_w_in(w_in):
    n_if = 2 * ML_HEADS
    cols = PREP_COLS
    n_a = COL_QSW // cols
    w_t = jnp.swapaxes(w_in, 0, 1)
    return pl.pallas_call(
        _prep_w_in_kernel,
        out_shape=(jax.ShapeDtypeStruct((D_MODEL, N_W), BF16),
                   jax.ShapeDtypeStruct((D_MODEL, 2 * LANES), BF16)),
        grid=(N_W // cols,),
        in_specs=[
            pl.BlockSpec((cols, D_MODEL), lambda i: (jnp.minimum(i, n_a - 1), 0)),
            pl.BlockSpec((pl.Element(cols), pl.Element(D_MODEL)),
                         lambda i: (pl.multiple_of(COL_QSW + n_if + jnp.maximum(i - n_a, 0) * cols, n_if), 0)),
            pl.BlockSpec((LANES, D_MODEL), lambda i: (COL_QSW // LANES, 0)),
        ],
        out_specs=(pl.BlockSpec((D_MODEL, cols), lambda i: (0, i)),
                   pl.BlockSpec((D_MODEL, 2 * LANES), lambda i: (0, 0))),
        compiler_params=pltpu.CompilerParams(
            dimension_semantics=("arbitrary",),
            vmem_limit_bytes=_vmem_limit(((cols, D_MODEL), F32, 4), ((LANES, D_MODEL), F32, 2),
                                         ((D_MODEL, cols), BF16, 2), ((D_MODEL, 2 * LANES), BF16, 2))),
        name="prep_w_in",
    )(w_t, w_t, w_t)


def kernel(x, p, norm_mix_g, w_in, conv_qk, b_if, mlstm_norm_g, sinks, w_branch_a, w_branch_b, w_out,
           norm_mlp_g, w_up, w_down, norm_ple_g, w_ple_gate, w_ple_proj, final_norm_g):
    batch, seq, _ = x.shape
    depth = w_in.shape[0]
    t = batch * seq
    x2d = x.reshape(t, D_MODEL)
    row = lambda v: v.reshape(1, -1)
    for i in range(depth):
        w_main, wg = _prep_w_in(w_in[i])
        bif = jnp.pad(b_if[i], (0, LANES - 2 * ML_HEADS)).reshape(1, LANES)
        ya, yb, gab = _front(x2d, sinks[i], row(norm_mix_g[i]), w_main, wg, conv_qk[i], bif,
                             row(mlstm_norm_g[i]), seq)
        x2d = _merge(x2d, ya, yb, gab, w_branch_a[i].astype(BF16), w_branch_b[i].astype(BF16),
                     w_out[i].astype(BF16))
        x2d = _mlp_ple(x2d, p[i].reshape(t, PLE_DIM), row(norm_mlp_g[i]), w_up[i].astype(BF16),
                       w_down[i].astype(BF16), row(norm_ple_g[i]), w_ple_gate[i].astype(BF16),
                       w_ple_proj[i].astype(BF16), row(final_norm_g), final_norm=(i == depth - 1))
    return x2d.reshape(batch, seq, D_MODEL)
```
